```python
import jax, jax.numpy as jnp
from jax import lax
import numpy as np

D_MODEL = 2048
BATCH = 4
SEQ = 4096
DEPTH = 1

HEAD_DIM = 128
N_Q_HEADS = 8
N_KV_HEADS = 2
GQA_GROUP = N_Q_HEADS // N_KV_HEADS
ATTN_WIDTH = N_Q_HEADS * HEAD_DIM
KV_WIDTH = N_KV_HEADS * HEAD_DIM
CONV_WIDTH = D_MODEL - ATTN_WIDTH
CONV_GROUPS = 8
MIX_WIDTH = ATTN_WIDTH + CONV_WIDTH
CONV_K = 3
GRID_W = 64
Q_BLOCK = 128
ROPE_THETA = 10000.0
ROPE_AXIS_DIM = HEAD_DIM // 2
EPS = 1e-6
SPLIT_SIZES = (ATTN_WIDTH, KV_WIDTH, KV_WIDTH, ATTN_WIDTH,
               CONV_WIDTH, CONV_WIDTH, CONV_WIDTH, CONV_WIDTH)
IN_PROJ_WIDTH = sum(SPLIT_SIZES)

kernel_name = "hymba_gqa_axialrope_shortconv_block"


def rmsnorm(x, g):
    xf = x.astype(jnp.float32)
    y = xf * lax.rsqrt(jnp.mean(xf * xf, axis=-1, keepdims=True) + EPS)
    return (y * g.astype(jnp.float32)).astype(x.dtype)


def axial_rope_tables(seq_len):
    rows = seq_len // GRID_W
    row = jnp.repeat(jnp.arange(rows, dtype=jnp.float32), GRID_W)
    col = jnp.tile(jnp.arange(GRID_W, dtype=jnp.float32), rows)
    inv_freq = ROPE_THETA ** (-jnp.arange(0, ROPE_AXIS_DIM, 2, dtype=jnp.float32) / ROPE_AXIS_DIM)
    ang_r = row[:, None] * inv_freq[None, :]
    ang_c = col[:, None] * inv_freq[None, :]
    return jnp.cos(ang_r), jnp.sin(ang_r), jnp.cos(ang_c), jnp.sin(ang_c)


def rotate(x, cos, sin):
    x1, x2 = jnp.split(x, 2, axis=-1)
    c = cos.astype(x.dtype)
    s = sin.astype(x.dtype)
    return jnp.concatenate([x1 * c - x2 * s, x2 * c + x1 * s], axis=-1)


def apply_axial_rope(x, tables):
    cr, sr, cc, sc = tables
    xr, xc = jnp.split(x, 2, axis=-1)
    return jnp.concatenate([rotate(xr, cr, sr), rotate(xc, cc, sc)], axis=-1)


def gqa_attention(q, k, v, q_norm, k_norm, tables):
    B, S, _ = q.shape
    q = rmsnorm(q.reshape(B, S, N_Q_HEADS, HEAD_DIM), q_norm).transpose(0, 2, 1, 3)
    k = rmsnorm(k.reshape(B, S, N_KV_HEADS, HEAD_DIM), k_norm).transpose(0, 2, 1, 3)
    v = v.reshape(B, S, N_KV_HEADS, HEAD_DIM).transpose(0, 2, 1, 3)
    q = apply_axial_rope(q, tables) * (HEAD_DIM ** -0.5)
    k = apply_axial_rope(k, tables)
    nb = S // Q_BLOCK
    qb = q.reshape(B, N_KV_HEADS, GQA_GROUP, nb, Q_BLOCK, HEAD_DIM).transpose(3, 0, 1, 2, 4, 5)

    def attend_block(qblk):
        s = jnp.einsum('bkgqd,bksd->bkgqs', qblk, k).astype(jnp.float32)
        p = jax.nn.softmax(s, axis=-1).astype(v.dtype)
        return jnp.einsum('bkgqs,bksd->bkgqd', p, v)

    o = lax.map(attend_block, qb)
    o = o.transpose(1, 0, 4, 2, 3, 5).reshape(B, S, ATTN_WIDTH)
    return o


def short_conv(xb, xc, xin, conv_w, conv_b):
    u = xc * xin
    S = u.shape[1]
    half = CONV_K // 2
    up = jnp.pad(u, ((0, 0), (half, half), (0, 0)))
    y = conv_b
    for j in range(CONV_K):
        y = y + up[:, j:j + S, :] * conv_w[j]
    return xb * y


def setup_inputs(seed: int = 0) -> dict:
    key = jax.random.key(seed)
    ks = jax.random.split(key, 10)
    f32 = jnp.float32
    x = jax.random.normal(ks[0], (BATCH, SEQ, D_MODEL), f32)
    norm_in = 1.0 + 0.02 * jax.random.normal(ks[1], (DEPTH, D_MODEL), f32)
    w_in = jax.random.normal(ks[2], (DEPTH, D_MODEL, IN_PROJ_WIDTH), f32) * D_MODEL ** -0.5
    q_norm = 1.0 + 0.02 * jax.random.normal(ks[3], (DEPTH, HEAD_DIM), f32)
    k_norm = 1.0 + 0.02 * jax.random.normal(ks[4], (DEPTH, HEAD_DIM), f32)
    conv_w = jax.random.normal(ks[5], (DEPTH, CONV_K, CONV_WIDTH), f32) * CONV_K ** -0.5
    conv_b = 0.02 * jax.random.normal(ks[6], (DEPTH, CONV_WIDTH), f32)
    w_out = jax.random.normal(ks[7], (DEPTH, MIX_WIDTH, D_MODEL), f32) * MIX_WIDTH ** -0.5
    norm_final = 1.0 + 0.02 * jax.random.normal(ks[8], (D_MODEL,), f32)
    return {"x": x, "norm_in": norm_in, "w_in": w_in, "q_norm": q_norm, "k_norm": k_norm,
            "conv_w": conv_w, "conv_b": conv_b, "w_out": w_out, "norm_final": norm_final}


def reference(x, norm_in, w_in, q_norm, k_norm, conv_w, conv_b, w_out, norm_final):
    S = x.shape[1]
    tables = axial_rope_tables(S)
    split_idx = list(np.cumsum(SPLIT_SIZES)[:-1])
    h = x
    for layer in range(DEPTH):
        hn = rmsnorm(h, norm_in[layer])
        proj = jnp.einsum('bsd,de->bse', hn, w_in[layer])
        q, k, v, g_attn, cb, cc, cx, g_conv = jnp.split(proj, split_idx, axis=-1)
        attn = gqa_attention(q, k, v, q_norm[layer], k_norm[layer], tables) * jax.nn.silu(g_attn)
        conv = short_conv(cb, cc, cx, conv_w[layer], conv_b[layer]) * jax.nn.silu(g_conv)
        mixed = jnp.concatenate([attn, conv], axis=-1)
        h = h + jnp.einsum('bse,ed->bsd', mixed, w_out[layer])
    return rmsnorm(h, norm_final)
```

```python
import functools

import jax
import jax.numpy as jnp
from jax import lax
from jax.experimental import pallas as pl
from jax.experimental.pallas import tpu as pltpu

HEAD_DIM = 128
N_Q_HEADS = 8
N_KV_HEADS = 2
GQA_GROUP = N_Q_HEADS // N_KV_HEADS
ATTN_WIDTH = N_Q_HEADS * HEAD_DIM
CONV_K = 3
GRID_W = 64
ROPE_THETA = 10000.0
ROPE_AXIS_DIM = HEAD_DIM // 2
EPS = 1e-6

_V7X_VMEM_BYTES = 64 * 2**20
_VMEM_LIMIT = _V7X_VMEM_BYTES - 8 * 2**20
_BF16_SUBLANES = 16
_W_BLOCK = 256

_PROJ_ROWS = 1024
_ATTN_TQ = 128
_ATTN_TK = 512
_OUT_ROWS = 512

f32 = jnp.float32
bf16 = jnp.bfloat16


def _silu(x):
    return x / (1.0 + jnp.exp(-x))


def _head_norm(x, g):
    ms = jnp.mean(x * x, axis=-1, keepdims=True)
    return x * lax.rsqrt(ms + EPS) * g


def _rope(x, c, s):
    lane = lax.broadcasted_iota(jnp.int32, x.shape, 1)
    first_half = (lane & (ROPE_AXIS_DIM // 2)) == 0
    swapped = jnp.where(first_half,
                        pltpu.roll(x, HEAD_DIM - ROPE_AXIS_DIM // 2, 1),
                        pltpu.roll(x, ROPE_AXIS_DIM // 2, 1))
    return x * c + swapped * s


def _rope_tables(seq_len):
    rows = seq_len // GRID_W
    row = jnp.repeat(jnp.arange(rows, dtype=f32), GRID_W)
    col = jnp.tile(jnp.arange(GRID_W, dtype=f32), rows)
    inv_freq = ROPE_THETA ** (-jnp.arange(0, ROPE_AXIS_DIM, 2, dtype=f32) / ROPE_AXIS_DIM)
    ang_r = row[:, None] * inv_freq[None, :]
    ang_c = col[:, None] * inv_freq[None, :]
    cr, sr, cc, sc = jnp.cos(ang_r), jnp.sin(ang_r), jnp.cos(ang_c), jnp.sin(ang_c)
    cos_t = jnp.concatenate([cr, cr, cc, cc], axis=-1)
    sin_t = jnp.concatenate([-sr, sr, -sc, sc], axis=-1)
    return cos_t, sin_t


_N_PROJ_STEPS = 7


def _wa_idx(j):
    return jnp.where(j < 2, 2 * j, jnp.where(j == 2, 4, 7 + j))


def _wb_idx(j):
    return jnp.where(j < 2, 2 * j + 1, jnp.where(j == 2, 5, 11 + j))


def _wc_idx(j):
    return jnp.where(j < 2, 6 + 2 * j, 15 + jnp.maximum(j, 3))


def _wd_idx(j):
    return jnp.where(j < 2, 7 + 2 * j, 19 + jnp.maximum(j, 3))


def _proj_kernel(x_ref, nin_ref, wa_ref, wb_ref, wc_ref, wd_ref, qn_ref, kn_ref,
                 cos_ref, sin_ref,
                 q_ref, k_ref, v_ref, sg_ref, u_ref, e_ref, hn_ref):
    j = pl.program_id(1)

    @pl.when(j == 0)
    def _():
        x = x_ref[...]
        ms = jnp.mean(x * x, axis=-1, keepdims=True)
        hn_ref[...] = (x * lax.rsqrt(ms + EPS) * nin_ref[...]).astype(bf16)

    def proj(w_ref):
        return jnp.dot(hn_ref[...], w_ref[...], preferred_element_type=f32)

    @pl.when(j < 2)
    def _():
        cos_t, sin_t = cos_ref[...], sin_ref[...]
        scale = HEAD_DIM ** -0.5
        for half, w_ref in enumerate((wa_ref, wb_ref)):
            qq = proj(w_ref)
            for hh in range(2):
                qh = qq[:, hh * HEAD_DIM:(hh + 1) * HEAD_DIM]
                qh = _rope(_head_norm(qh, qn_ref[...]), cos_t, sin_t) * scale
                q_ref[0, 2 * half + hh] = qh.astype(bf16)
        sg_ref[:, :_W_BLOCK] = _silu(proj(wc_ref)).astype(bf16)
        sg_ref[:, _W_BLOCK:] = _silu(proj(wd_ref)).astype(bf16)

    @pl.when(j == 2)
    def _():
        cos_t, sin_t = cos_ref[...], sin_ref[...]
        kk = proj(wa_ref)
        vv = proj(wb_ref)
        for hh in range(N_KV_HEADS):
            kh = kk[:, hh * HEAD_DIM:(hh + 1) * HEAD_DIM]
            k_ref[0, hh] = _rope(_head_norm(kh, kn_ref[...]), cos_t, sin_t).astype(bf16)
            v_ref[0, hh] = vv[:, hh * HEAD_DIM:(hh + 1) * HEAD_DIM].astype(bf16)

    @pl.when(j > 2)
    def _():
        cb = proj(wa_ref)
        cc = proj(wb_ref)
        cx = proj(wc_ref)
        gc = proj(wd_ref)
        u_ref[...] = (cc * cx).astype(bf16)
        e_ref[...] = (cb * _silu(gc)).astype(bf16)


def _in_projection(x2, norm_in, w_bf, q_norm, k_norm, cos_t, sin_t, batch, seq):
    rows, d_model = x2.shape
    tm = _PROJ_ROWS
    nt = seq // tm
    w_spec = lambda idx: pl.BlockSpec((d_model, _W_BLOCK), lambda i, j: (0, idx(j)))
    full = lambda shape: pl.BlockSpec(shape, lambda i, j: (0,) * len(shape))
    table = pl.BlockSpec((tm, HEAD_DIM), lambda i, j: (i % nt, 0))
    conv_width = 4 * _W_BLOCK
    out_shape = (
        jax.ShapeDtypeStruct((batch, N_Q_HEADS, seq, HEAD_DIM), bf16),
        jax.ShapeDtypeStruct((batch, N_KV_HEADS, seq, HEAD_DIM), bf16),
        jax.ShapeDtypeStruct((batch, N_KV_HEADS, seq, HEAD_DIM), bf16),
        jax.ShapeDtypeStruct((rows, ATTN_WIDTH), bf16),
        jax.ShapeDtypeStruct((rows, conv_width), bf16),
        jax.ShapeDtypeStruct((rows, conv_width), bf16),
    )
    out_specs = (
        pl.BlockSpec((1, GQA_GROUP, tm, HEAD_DIM), lambda i, j: (i // nt, jnp.minimum(j, 1), i % nt, 0)),
        pl.BlockSpec((1, N_KV_HEADS, tm, HEAD_DIM), lambda i, j: (i // nt, 0, i % nt, 0)),
        pl.BlockSpec((1, N_KV_HEADS, tm, HEAD_DIM), lambda i, j: (i // nt, 0, i % nt, 0)),
        pl.BlockSpec((tm, 2 * _W_BLOCK), lambda i, j: (i, jnp.minimum(j, 1))),
        pl.BlockSpec((tm, _W_BLOCK), lambda i, j: (i, jnp.maximum(j - 3, 0))),
        pl.BlockSpec((tm, _W_BLOCK), lambda i, j: (i, jnp.maximum(j - 3, 0))),
    )
    return pl.pallas_call(
        _proj_kernel,
        grid=(rows // tm, _N_PROJ_STEPS),
        in_specs=[
            pl.BlockSpec((tm, d_model), lambda i, j: (i, 0)),
            full((1, d_model)),
            w_spec(_wa_idx), w_spec(_wb_idx), w_spec(_wc_idx), w_spec(_wd_idx),
            full((1, HEAD_DIM)), full((1, HEAD_DIM)),
            table, table,
        ],
        out_specs=out_specs,
        out_shape=out_shape,
        scratch_shapes=[pltpu.VMEM((tm, d_model), bf16)],
        compiler_params=pltpu.CompilerParams(
            dimension_semantics=("arbitrary", "arbitrary"),
            vmem_limit_bytes=_VMEM_LIMIT),
        name="in_projection",
    )(x2, norm_in, w_bf, w_bf, w_bf, w_bf, q_norm, k_norm, cos_t, sin_t)


def _attn_kernel(q_ref, k_ref, v_ref, sg_ref, o_ref, m_ref, l_ref, acc_ref, *, tq, tk, nk):
    rows = GQA_GROUP * tq
    q = q_ref[0].reshape(rows, HEAD_DIM)
    m_ref[...] = jnp.full(m_ref.shape, -jnp.inf, f32)
    l_ref[...] = jnp.zeros(l_ref.shape, f32)
    acc_ref[...] = jnp.zeros(acc_ref.shape, f32)

    def body(j, carry):
        off = pl.multiple_of(j * tk, tk)
        k = k_ref[0, 0, pl.ds(off, tk), :]
        v = v_ref[0, 0, pl.ds(off, tk), :]
        s = lax.dot_general(q, k, (((1,), (1,)), ((), ())), preferred_element_type=f32)
        m_prev = m_ref[...]
        m_new = jnp.maximum(m_prev, jnp.max(s, axis=1, keepdims=True))
        alpha = jnp.exp(m_prev - m_new)
        p = jnp.exp(s - m_new)
        l_ref[...] = alpha * l_ref[...] + jnp.sum(p, axis=1, keepdims=True)
        acc_ref[...] = alpha * acc_ref[...] + jnp.dot(p.astype(bf16), v, preferred_element_type=f32)
        m_ref[...] = m_new
        return carry

    lax.fori_loop(0, nk, body, 0)
    o = acc_ref[...] / l_ref[...]
    for h in range(GQA_GROUP):
        lanes = slice(h * HEAD_DIM, (h + 1) * HEAD_DIM)
        gate = sg_ref[:, lanes].astype(f32)
        o_ref[:, lanes] = (o[h * tq:(h + 1) * tq] * gate).astype(bf16)


def _attention(q, k, v, sg, batch, seq):
    tq, tk = _ATTN_TQ, _ATTN_TK
    nq = seq // tq
    rows = GQA_GROUP * tq
    group_width = GQA_GROUP * HEAD_DIM
    kv_spec = pl.BlockSpec((1, 1, seq, HEAD_DIM), lambda b, h, i: (b, h, 0, 0))
    gate_spec = pl.BlockSpec((tq, group_width), lambda b, h, i: (b * nq + i, h))
    return pl.pallas_call(
        functools.partial(_attn_kernel, tq=tq, tk=tk, nk=seq // tk),
        grid=(batch, N_KV_HEADS, nq),
        in_specs=[
            pl.BlockSpec((1, GQA_GROUP, tq, HEAD_DIM), lambda b, h, i: (b, h, i, 0)),
            kv_spec, kv_spec, gate_spec,
        ],
        out_specs=gate_spec,
        out_shape=jax.ShapeDtypeStruct((batch * seq, ATTN_WIDTH), bf16),
        scratch_shapes=[pltpu.VMEM((rows, 1), f32), pltpu.VMEM((rows, 1), f32),
                        pltpu.VMEM((rows, HEAD_DIM), f32)],
        compiler_params=pltpu.CompilerParams(
            dimension_semantics=("arbitrary", "arbitrary", "arbitrary"),
            vmem_limit_bytes=_VMEM_LIMIT),
        name="gqa_attention",
    )(q, k, v, sg)


def _out_kernel(attn_ref, u_ref, uprev_ref, unext_ref, e_ref, x_ref, w_ref, cw_ref, cb_ref, nf_ref,
                o_ref, *, tm, seq):
    i = pl.program_id(0)
    t0 = (i * tm) % seq
    u = u_ref[...].astype(f32)
    prev_row = uprev_ref[...].astype(f32)[_BF16_SUBLANES - 1:_BF16_SUBLANES, :]
    next_row = unext_ref[...].astype(f32)[0:1, :]
    prev_row = jnp.where(t0 == 0, 0.0, prev_row)
    next_row = jnp.where(t0 + tm == seq, 0.0, next_row)
    row = lax.broadcasted_iota(jnp.int32, u.shape, 0)
    u_m1 = jnp.where(row == 0, prev_row, pltpu.roll(u, 1, 0))
    u_p1 = jnp.where(row == tm - 1, next_row, pltpu.roll(u, tm - 1, 0))
    cw = cw_ref[...]
    y = cb_ref[...] + u_m1 * cw[0:1, :]
    y = y + u * cw[1:2, :]
    y = y + u_p1 * cw[2:3, :]
    conv = (e_ref[...].astype(f32) * y).astype(bf16)
    acc = jnp.dot(attn_ref[...], w_ref[:ATTN_WIDTH, :], preferred_element_type=f32)
    acc = acc + jnp.dot(conv, w_ref[ATTN_WIDTH:, :], preferred_element_type=f32)
    h = x_ref[...] + acc
    ms = jnp.mean(h * h, axis=-1, keepdims=True)
    o_ref[...] = h * lax.rsqrt(ms + EPS) * nf_ref[...]


def _out_projection(attn, u, e, x2, w_bf, conv_w, conv_b, norm_final, seq):
    rows, d_model = x2.shape
    tm = _OUT_ROWS
    conv_width = u.shape[1]
    halo_blocks = rows // _BF16_SUBLANES
    per_tile = tm // _BF16_SUBLANES
    row_tile = lambda width: pl.BlockSpec((tm, width), lambda i: (i, 0))
    full = lambda shape: pl.BlockSpec(shape, lambda i: (0,) * len(shape))
    return pl.pallas_call(
        functools.partial(_out_kernel, tm=tm, seq=seq),
        grid=(rows // tm,),
        in_specs=[
            row_tile(ATTN_WIDTH),
            row_tile(conv_width),
            pl.BlockSpec((_BF16_SUBLANES, conv_width), lambda i: (jnp.maximum(i * per_tile - 1, 0), 0)),
            pl.BlockSpec((_BF16_SUBLANES, conv_width),
                         lambda i: (jnp.minimum((i + 1) * per_tile, halo_blocks - 1), 0)),
            row_tile(conv_width),
            row_tile(d_model),
            full(w_bf.shape), full(conv_w.shape), full(conv_b.shape), full(norm_final.shape),
        ],
        out_specs=row_tile(d_model),
        out_shape=jax.ShapeDtypeStruct((rows, d_model), f32),
        compiler_params=pltpu.CompilerParams(
            dimension_semantics=("arbitrary",),
            vmem_limit_bytes=_VMEM_LIMIT),
        name="out_projection",
    )(attn, u, u, u, e, x2, w_bf, conv_w, conv_b, norm_final)


def kernel(x, norm_in, w_in, q_norm, k_norm, conv_w, conv_b, w_out, norm_final):
    batch, seq, d_model = x.shape
    assert norm_in.shape[0] == 1, "single-layer block"
    assert seq % _PROJ_ROWS == 0 and seq % _OUT_ROWS == 0 and seq % _ATTN_TK == 0 and seq % _ATTN_TQ == 0
    x2 = x.reshape(batch * seq, d_model)
    cos_t, sin_t = _rope_tables(seq)
    q, k, v, sg, u, e = _in_projection(
        x2, norm_in, w_in[0].astype(bf16), q_norm, k_norm, cos_t, sin_t, batch, seq)
    attn = _attention(q, k, v, sg, batch, seq)
    out = _out_projection(attn, u, e, x2, w_out[0].astype(bf16), conv_w[0], conv_b,
                          norm_final.reshape(1, d_model), seq)
    return out.reshape(batch, seq, d_model)
```

```python
import functools

import jax
import jax.numpy as jnp
from jax import lax
from jax.experimental import pallas as pl
from jax.experimental.pallas import tpu as pltpu

HEAD_DIM = 128
N_Q_HEADS = 8
N_KV_HEADS = 2
GQA_GROUP = N_Q_HEADS // N_KV_HEADS
ATTN_WIDTH = N_Q_HEADS * HEAD_DIM
CONV_K = 3
GRID_W = 64
ROPE_THETA = 10000.0
ROPE_AXIS_DIM = HEAD_DIM // 2
EPS = 1e-6
_LOG2_E = 1.4426950408889634

_V7X_VMEM_BYTES = 64 * 2**20
_VMEM_LIMIT = _V7X_VMEM_BYTES - 8 * 2**20
_BF16_SUBLANES = 16
_W_BLOCK = 256

_PROJ_ROWS = 1024
_ATTN_TQ = 128
_ATTN_TK = 1024
_OUT_ROWS = 512

f32 = jnp.float32
bf16 = jnp.bfloat16


def _silu(x):
    return x / (1.0 + jnp.exp(-x))


def _head_norm(x, g):
    ms = jnp.mean(x * x, axis=-1, keepdims=True)
    return x * lax.rsqrt(ms + EPS) * g


def _rope(x, c, s):
    lane = lax.broadcasted_iota(jnp.int32, x.shape, 1)
    first_half = (lane & (ROPE_AXIS_DIM // 2)) == 0
    swapped = jnp.where(first_half,
                        pltpu.roll(x, HEAD_DIM - ROPE_AXIS_DIM // 2, 1),
                        pltpu.roll(x, ROPE_AXIS_DIM // 2, 1))
    return x * c + swapped * s


def _rope_tables(seq_len):
    rows = seq_len // GRID_W
    row = jnp.repeat(jnp.arange(rows, dtype=f32), GRID_W)
    col = jnp.tile(jnp.arange(GRID_W, dtype=f32), rows)
    inv_freq = ROPE_THETA ** (-jnp.arange(0, ROPE_AXIS_DIM, 2, dtype=f32) / ROPE_AXIS_DIM)
    ang_r = row[:, None] * inv_freq[None, :]
    ang_c = col[:, None] * inv_freq[None, :]
    cr, sr, cc, sc = jnp.cos(ang_r), jnp.sin(ang_r), jnp.cos(ang_c), jnp.sin(ang_c)
    cos_t = jnp.concatenate([cr, cr, cc, cc], axis=-1)
    sin_t = jnp.concatenate([-sr, sr, -sc, sc], axis=-1)
    return cos_t, sin_t


_N_PROJ_STEPS = 7


def _wa_idx(j):
    return jnp.where(j < 2, 2 * j, jnp.where(j == 2, 4, 7 + j))


def _wb_idx(j):
    return jnp.where(j < 2, 2 * j + 1, jnp.where(j == 2, 5, 11 + j))


def _wc_idx(j):
    return jnp.where(j < 2, 6 + 2 * j, 15 + jnp.maximum(j, 3))


def _wd_idx(j):
    return jnp.where(j < 2, 7 + 2 * j, 19 + jnp.maximum(j, 3))


def _proj_kernel(x_ref, nin_ref, wa_ref, wb_ref, wc_ref, wd_ref, qn_ref, kn_ref,
                 cos_ref, sin_ref,
                 q_ref, k_ref, v_ref, sg_ref, u_ref, e_ref, hn_ref):
    j = pl.program_id(1)

    @pl.when(j == 0)
    def _():
        x = x_ref[...]
        ms = jnp.mean(x * x, axis=-1, keepdims=True)
        hn_ref[...] = (x * lax.rsqrt(ms + EPS) * nin_ref[...]).astype(bf16)

    def proj(w_ref):
        return jnp.dot(hn_ref[...], w_ref[...], preferred_element_type=f32)

    @pl.when(j < 2)
    def _():
        cos_t, sin_t = cos_ref[...], sin_ref[...]
        scale = HEAD_DIM ** -0.5 * _LOG2_E
        for half, w_ref in enumerate((wa_ref, wb_ref)):
            qq = proj(w_ref)
            for hh in range(2):
                qh = qq[:, hh * HEAD_DIM:(hh + 1) * HEAD_DIM]
                qh = _rope(_head_norm(qh, qn_ref[...]), cos_t, sin_t) * scale
                q_ref[0, 2 * half + hh] = qh.T.astype(bf16)
        sg_ref[:, :_W_BLOCK] = _silu(proj(wc_ref)).astype(bf16)
        sg_ref[:, _W_BLOCK:] = _silu(proj(wd_ref)).astype(bf16)

    @pl.when(j == 2)
    def _():
        cos_t, sin_t = cos_ref[...], sin_ref[...]
        kk = proj(wa_ref)
        vv = proj(wb_ref)
        for hh in range(N_KV_HEADS):
            kh = kk[:, hh * HEAD_DIM:(hh + 1) * HEAD_DIM]
            k_ref[0, hh] = _rope(_head_norm(kh, kn_ref[...]), cos_t, sin_t).astype(bf16)
            v_ref[0, hh] = vv[:, hh * HEAD_DIM:(hh + 1) * HEAD_DIM].T.astype(bf16)

    @pl.when(j > 2)
    def _():
        cb = proj(wa_ref)
        cc = proj(wb_ref)
        cx = proj(wc_ref)
        gc = proj(wd_ref)
        u_ref[...] = (cc * cx).astype(bf16)
        e_ref[...] = (cb * _silu(gc)).astype(bf16)


def _in_projection(x2, norm_in, w_bf, q_norm, k_norm, cos_t, sin_t, batch, seq):
    rows, d_model = x2.shape
    tm = _PROJ_ROWS
    nt = seq // tm
    w_spec = lambda idx: pl.BlockSpec((d_model, _W_BLOCK), lambda i, j: (0, idx(j)))
    full = lambda shape: pl.BlockSpec(shape, lambda i, j: (0,) * len(shape))
    table = pl.BlockSpec((tm, HEAD_DIM), lambda i, j: (i % nt, 0))
    conv_width = 4 * _W_BLOCK
    out_shape = (
        jax.ShapeDtypeStruct((batch, N_Q_HEADS, HEAD_DIM, seq), bf16),
        jax.ShapeDtypeStruct((batch, N_KV_HEADS, seq, HEAD_DIM), bf16),
        jax.ShapeDtypeStruct((batch, N_KV_HEADS, HEAD_DIM, seq), bf16),
        jax.ShapeDtypeStruct((rows, ATTN_WIDTH), bf16),
        jax.ShapeDtypeStruct((rows, conv_width), bf16),
        jax.ShapeDtypeStruct((rows, conv_width), bf16),
    )
    out_specs = (
        pl.BlockSpec((1, GQA_GROUP, HEAD_DIM, tm), lambda i, j: (i // nt, jnp.minimum(j, 1), 0, i % nt)),
        pl.BlockSpec((1, N_KV_HEADS, tm, HEAD_DIM), lambda i, j: (i // nt, 0, i % nt, 0)),
        pl.BlockSpec((1, N_KV_HEADS, HEAD_DIM, tm), lambda i, j: (i // nt, 0, 0, i % nt)),
        pl.BlockSpec((tm, 2 * _W_BLOCK), lambda i, j: (i, jnp.minimum(j, 1))),
        pl.BlockSpec((tm, _W_BLOCK), lambda i, j: (i, jnp.maximum(j - 3, 0))),
        pl.BlockSpec((tm, _W_BLOCK), lambda i, j: (i, jnp.maximum(j - 3, 0))),
    )
    return pl.pallas_call(
        _proj_kernel,
        grid=(rows // tm, _N_PROJ_STEPS),
        in_specs=[
            pl.BlockSpec((tm, d_model), lambda i, j: (i, 0)),
            full((1, d_model)),
            w_spec(_wa_idx), w_spec(_wb_idx), w_spec(_wc_idx), w_spec(_wd_idx),
            full((1, HEAD_DIM)), full((1, HEAD_DIM)),
            table, table,
        ],
        out_specs=out_specs,
        out_shape=out_shape,
        scratch_shapes=[pltpu.VMEM((tm, d_model), bf16)],
        compiler_params=pltpu.CompilerParams(
            dimension_semantics=("arbitrary", "arbitrary"),
            vmem_limit_bytes=_VMEM_LIMIT),
        name="in_projection",
    )(x2, norm_in, w_bf, w_bf, w_bf, w_bf, q_norm, k_norm, cos_t, sin_t)


def _attn_kernel(q_ref, k_ref, v_ref, sg_ref, o_ref, acc_ref, s_ref, *, tq, tk, nk):
    cols = GQA_GROUP * tq
    q_t = jnp.concatenate([q_ref[0, h] for h in range(GQA_GROUP)], axis=1)
    acc_ref[...] = jnp.zeros(acc_ref.shape, f32)

    def scores(j, slot):
        k = k_ref[0, 0, pl.ds(pl.multiple_of(j * tk, tk), tk), :]
        s_ref[slot] = jnp.dot(k, q_t, preferred_element_type=f32)

    ones_rows = jnp.ones((_BF16_SUBLANES, tk), bf16)

    def step(j, slot, m_prev):
        scores(jnp.minimum(j + 1, nk - 1), 1 - slot)
        off = pl.multiple_of(j * tk, tk)
        v_aug = jnp.concatenate([v_ref[0, 0, :, pl.ds(off, tk)], ones_rows], axis=0)
        s_t = s_ref[slot]
        m_new = jnp.maximum(m_prev, jnp.max(s_t, axis=0, keepdims=True))
        alpha = jnp.exp2(m_prev - m_new)
        p_t = jnp.exp2(s_t - m_new).astype(bf16)
        pv_t = jnp.dot(v_aug, p_t, preferred_element_type=f32)
        acc_ref[...] = alpha * acc_ref[...] + pv_t
        return m_new

    def body(i, m_prev):
        return step(2 * i + 1, 1, step(2 * i, 0, m_prev))

    scores(0, 0)
    lax.fori_loop(0, nk // 2, body, jnp.full((1, cols), -jnp.inf, f32))
    o_t = acc_ref[:HEAD_DIM, :] / acc_ref[HEAD_DIM:HEAD_DIM + 1, :]
    for h in range(GQA_GROUP):
        lanes = slice(h * HEAD_DIM, (h + 1) * HEAD_DIM)
        gate = sg_ref[:, lanes].astype(f32)
        o_ref[:, lanes] = (o_t[:, h * tq:(h + 1) * tq].T * gate).astype(bf16)


def _attention(q_t, k, v_t, sg, batch, seq):
    tq, tk = _ATTN_TQ, _ATTN_TK
    nq = seq // tq
    cols = GQA_GROUP * tq
    group_width = GQA_GROUP * HEAD_DIM
    gate_spec = pl.BlockSpec((tq, group_width), lambda b, h, i: (b * nq + i, h))
    return pl.pallas_call(
        functools.partial(_attn_kernel, tq=tq, tk=tk, nk=seq // tk),
        grid=(batch, N_KV_HEADS, nq),
        in_specs=[
            pl.BlockSpec((1, GQA_GROUP, HEAD_DIM, tq), lambda b, h, i: (b, h, 0, i)),
            pl.BlockSpec((1, 1, seq, HEAD_DIM), lambda b, h, i: (b, h, 0, 0)),
            pl.BlockSpec((1, 1, HEAD_DIM, seq), lambda b, h, i: (b, h, 0, 0)),
            gate_spec,
        ],
        out_specs=gate_spec,
        out_shape=jax.ShapeDtypeStruct((batch * seq, ATTN_WIDTH), bf16),
        scratch_shapes=[pltpu.VMEM((HEAD_DIM + _BF16_SUBLANES, cols), f32), pltpu.VMEM((2, tk, cols), f32)],
        compiler_params=pltpu.CompilerParams(
            dimension_semantics=("arbitrary", "arbitrary", "arbitrary"),
            vmem_limit_bytes=_VMEM_LIMIT),
        name="gqa_attention",
    )(q_t, k, v_t, sg)


def _out_kernel(attn_ref, u_ref, uprev_ref, unext_ref, e_ref, x_ref, w_ref, cw_ref, cb_ref, nf_ref,
                o_ref, *, tm, seq):
    i = pl.program_id(0)
    t0 = (i * tm) % seq
    u = u_ref[...].astype(f32)
    prev_row = uprev_ref[...].astype(f32)[_BF16_SUBLANES - 1:_BF16_SUBLANES, :]
    next_row = unext_ref[...].astype(f32)[0:1, :]
    prev_row = jnp.where(t0 == 0, 0.0, prev_row)
    next_row = jnp.where(t0 + tm == seq, 0.0, next_row)
    row = lax.broadcasted_iota(jnp.int32, u.shape, 0)
    u_m1 = jnp.where(row == 0, prev_row, pltpu.roll(u, 1, 0))
    u_p1 = jnp.where(row == tm - 1, next_row, pltpu.roll(u, tm - 1, 0))
    cw = cw_ref[...]
    y = cb_ref[...] + u_m1 * cw[0:1, :]
    y = y + u * cw[1:2, :]
    y = y + u_p1 * cw[2:3, :]
    conv = (e_ref[...].astype(f32) * y).astype(bf16)
    acc = jnp.dot(attn_ref[...], w_ref[:ATTN_WIDTH, :], preferred_element_type=f32)
    acc = acc + jnp.dot(conv, w_ref[ATTN_WIDTH:, :], preferred_element_type=f32)
    h = x_ref[...] + acc
    ms = jnp.mean(h * h, axis=-1, keepdims=True)
    o_ref[...] = h * lax.rsqrt(ms + EPS) * nf_ref[...]


def _out_projection(attn, u, e, x2, w_bf, conv_w, conv_b, norm_final, seq):
    rows, d_model = x2.shape
    tm = _OUT_ROWS
    conv_width = u.shape[1]
    halo_blocks = rows // _BF16_SUBLANES
    per_tile = tm // _BF16_SUBLANES
    row_tile = lambda width: pl.BlockSpec((tm, width), lambda i: (i, 0))
    full = lambda shape: pl.BlockSpec(shape, lambda i: (0,) * len(shape))
    return pl.pallas_call(
        functools.partial(_out_kernel, tm=tm, seq=seq),
        grid=(rows // tm,),
        in_specs=[
            row_tile(ATTN_WIDTH),
            row_tile(conv_width),
            pl.BlockSpec((_BF16_SUBLANES, conv_width), lambda i: (jnp.maximum(i * per_tile - 1, 0), 0)),
            pl.BlockSpec((_BF16_SUBLANES, conv_width),
                         lambda i: (jnp.minimum((i + 1) * per_tile, halo_blocks - 1), 0)),
            row_tile(conv_width),
            row_tile(d_model),
            full(w_bf.shape), full(conv_w.shape), full(conv_b.shape), full(norm_final.shape),
        ],
        out_specs=row_tile(d_model),
        out_shape=jax.ShapeDtypeStruct((rows, d_model), f32),
        compiler_params=pltpu.CompilerParams(
            dimension_semantics=("arbitrary",),
            vmem_limit_bytes=_VMEM_LIMIT),
        name="out_projection",
    )(attn, u, u, u, e, x2, w_bf, conv_w, conv_b, norm_final)


def kernel(x, norm_in, w_in, q_norm, k_norm, conv_w, conv_b, w_out, norm_final):
    batch, seq, d_model = x.shape
    assert norm_in.shape[0] == 1, "single-layer block"
    assert seq % _PROJ_ROWS == 0 and seq % _OUT_ROWS == 0 and seq % _ATTN_TK == 0 and seq % _ATTN_TQ == 0
    x2 = x.reshape(batch * seq, d_model)
    cos_t, sin_t = _rope_tables(seq)
    q, k, v, sg, u, e = _in_projection(
        x2, norm_in, w_in[0].astype(bf16), q_norm, k_norm, cos_t, sin_t, batch, seq)
    attn = _attention(q, k, v, sg, batch, seq)
    out = _out_projection(attn, u, e, x2, w_out[0].astype(bf16), conv_w[0], conv_b,
                          norm_final.reshape(1, d_model), seq)
    return out.reshape(batch, seq, d_model)
```

```python
import functools

import jax
import jax.numpy as jnp
from jax import lax
from jax.experimental import pallas as pl
from jax.experimental.pallas import tpu as pltpu

HEAD_DIM = 128
N_Q_HEADS = 8
N_KV_HEADS = 2
GQA_GROUP = N_Q_HEADS // N_KV_HEADS
ATTN_WIDTH = N_Q_HEADS * HEAD_DIM
CONV_K = 3
GRID_W = 64
ROPE_THETA = 10000.0
ROPE_AXIS_DIM = HEAD_DIM // 2
EPS = 1e-6
_LOG2_E = 1.4426950408889634

_V7X_VMEM_BYTES = 64 * 2**20
_VMEM_LIMIT = _V7X_VMEM_BYTES - 8 * 2**20
_BF16_SUBLANES = 16
_W_BLOCK = 256

_PROJ_ROWS = 1024
_ATTN_TQ = 128
_ATTN_TK = 1024
_OUT_ROWS = 512

f32 = jnp.float32
bf16 = jnp.bfloat16


def _silu(x):
    return x / (1.0 + jnp.exp(-x))


def _head_norm(x, g):
    ms = jnp.mean(x * x, axis=-1, keepdims=True)
    return x * lax.rsqrt(ms + EPS) * g


def _rope(x, c, s):
    lane = lax.broadcasted_iota(jnp.int32, x.shape, 1)
    first_half = (lane & (ROPE_AXIS_DIM // 2)) == 0
    swapped = jnp.where(first_half,
                        pltpu.roll(x, HEAD_DIM - ROPE_AXIS_DIM // 2, 1),
                        pltpu.roll(x, ROPE_AXIS_DIM // 2, 1))
    return x * c + swapped * s


def _rope_tables(seq_len):
    rows = seq_len // GRID_W
    row = jnp.repeat(jnp.arange(rows, dtype=f32), GRID_W)
    col = jnp.tile(jnp.arange(GRID_W, dtype=f32), rows)
    inv_freq = ROPE_THETA ** (-jnp.arange(0, ROPE_AXIS_DIM, 2, dtype=f32) / ROPE_AXIS_DIM)
    ang_r = row[:, None] * inv_freq[None, :]
    ang_c = col[:, None] * inv_freq[None, :]
    cr, sr, cc, sc = jnp.cos(ang_r), jnp.sin(ang_r), jnp.cos(ang_c), jnp.sin(ang_c)
    cos_t = jnp.concatenate([cr, cr, cc, cc], axis=-1)
    sin_t = jnp.concatenate([-sr, sr, -sc, sc], axis=-1)
    return cos_t, sin_t


_N_PROJ_STEPS = 6


def _wa_idx(j):
    return jnp.where(j < 2, 2 * j, 8 + j)


def _wb_idx(j):
    return jnp.where(j < 2, 2 * j + 1, 12 + j)


def _wc_idx(j):
    return jnp.where(j < 2, 6 + 2 * j, 16 + j)


def _wd_idx(j):
    return jnp.where(j < 2, 7 + 2 * j, 20 + j)


def _we_idx(j):
    return jnp.where(j == 0, 4, 5)


def _rope_t(x_t, cos_tt, sin_tt):
    q = ROPE_AXIS_DIM // 2
    swapped = jnp.concatenate([x_t[q:2 * q], x_t[0:q], x_t[3 * q:4 * q], x_t[2 * q:3 * q]], axis=0)
    return x_t * cos_tt + swapped * sin_tt


def _proj_kernel(x_ref, nin_ref, wa_ref, wb_ref, wc_ref, wd_ref, we_ref, qg_ref, kn_ref,
                 cos_ref, sin_ref, cost_ref, sint_ref,
                 q_ref, k_ref, v_ref, sg_ref, u_ref, e_ref, hn_ref):
    j = pl.program_id(1)
    tm = hn_ref.shape[0]

    @pl.when(j == 0)
    def _():
        x = x_ref[...]
        ms = jnp.mean(x * x, axis=-1, keepdims=True)
        hn_ref[...] = (x * lax.rsqrt(ms + EPS) * nin_ref[...]).astype(bf16)

    def proj(w_ref):
        return jnp.dot(hn_ref[...], w_ref[...], preferred_element_type=f32)

    def q_and_gates():
        cos_tt, sin_tt = cost_ref[...], sint_ref[...]
        gain = jnp.tile(qg_ref[...], (1, tm // HEAD_DIM))
        scale = HEAD_DIM ** -0.5 * _LOG2_E
        for half, w_ref in enumerate((wa_ref, wb_ref)):
            qq = proj(w_ref)
            for hh in range(2):
                q_t = qq[:, hh * HEAD_DIM:(hh + 1) * HEAD_DIM].T
                ms = jnp.mean(q_t * q_t, axis=0, keepdims=True)
                r = lax.rsqrt(ms + EPS) * scale
                q_ref[0, 2 * half + hh] = (_rope_t(q_t * gain, cos_tt, sin_tt) * r).astype(bf16)
        sg_ref[:, :_W_BLOCK] = _silu(proj(wc_ref)).astype(bf16)
        sg_ref[:, _W_BLOCK:] = _silu(proj(wd_ref)).astype(bf16)

    @pl.when(j == 0)
    def _():
        kk = proj(we_ref)
        for hh in range(N_KV_HEADS):
            kh = kk[:, hh * HEAD_DIM:(hh + 1) * HEAD_DIM]
            k_ref[0, hh] = _rope(_head_norm(kh, kn_ref[...]), cos_ref[...], sin_ref[...]).astype(bf16)
        q_and_gates()

    @pl.when(j == 1)
    def _():
        vv = proj(we_ref)
        for hh in range(N_KV_HEADS):
            v_ref[0, hh] = vv[:, hh * HEAD_DIM:(hh + 1) * HEAD_DIM].T.astype(bf16)
        q_and_gates()

    @pl.when(j >= 2)
    def _():
        cb = proj(wa_ref)
        cc = proj(wb_ref)
        cx = proj(wc_ref)
        gc = proj(wd_ref)
        u_ref[...] = (cc * cx).astype(bf16)
        e_ref[...] = (cb * _silu(gc)).astype(bf16)


def _in_projection(x2, norm_in, w_bf, q_norm, k_norm, cos_t, sin_t, batch, seq):
    rows, d_model = x2.shape
    tm = _PROJ_ROWS
    nt = seq // tm
    w_spec = lambda idx: pl.BlockSpec((d_model, _W_BLOCK), lambda i, j: (0, idx(j)))
    full = lambda shape: pl.BlockSpec(shape, lambda i, j: (0,) * len(shape))
    table = pl.BlockSpec((tm, HEAD_DIM), lambda i, j: (i % nt, 0))
    table_t = pl.BlockSpec((HEAD_DIM, tm), lambda i, j: (0, i % nt))
    q_gain = jnp.broadcast_to(q_norm.reshape(HEAD_DIM, 1), (HEAD_DIM, HEAD_DIM))
    conv_width = 4 * _W_BLOCK
    out_shape = (
        jax.ShapeDtypeStruct((batch, N_Q_HEADS, HEAD_DIM, seq), bf16),
        jax.ShapeDtypeStruct((batch, N_KV_HEADS, seq, HEAD_DIM), bf16),
        jax.ShapeDtypeStruct((batch, N_KV_HEADS, HEAD_DIM, seq), bf16),
        jax.ShapeDtypeStruct((rows, ATTN_WIDTH), bf16),
        jax.ShapeDtypeStruct((rows, conv_width), bf16),
        jax.ShapeDtypeStruct((rows, conv_width), bf16),
    )
    out_specs = (
        pl.BlockSpec((1, GQA_GROUP, HEAD_DIM, tm), lambda i, j: (i // nt, jnp.minimum(j, 1), 0, i % nt)),
        pl.BlockSpec((1, N_KV_HEADS, tm, HEAD_DIM), lambda i, j: (i // nt, 0, i % nt, 0)),
        pl.BlockSpec((1, N_KV_HEADS, HEAD_DIM, tm), lambda i, j: (i // nt, 0, 0, i % nt)),
        pl.BlockSpec((tm, 2 * _W_BLOCK), lambda i, j: (i, jnp.minimum(j, 1))),
        pl.BlockSpec((tm, _W_BLOCK), lambda i, j: (i, jnp.maximum(j - 2, 0))),
        pl.BlockSpec((tm, _W_BLOCK), lambda i, j: (i, jnp.maximum(j - 2, 0))),
    )
    return pl.pallas_call(
        _proj_kernel,
        grid=(rows // tm, _N_PROJ_STEPS),
        in_specs=[
            pl.BlockSpec((tm, d_model), lambda i, j: (i, 0)),
            full((1, d_model)),
            w_spec(_wa_idx), w_spec(_wb_idx), w_spec(_wc_idx), w_spec(_wd_idx), w_spec(_we_idx),
            full((HEAD_DIM, HEAD_DIM)), full((1, HEAD_DIM)),
            table, table, table_t, table_t,
        ],
        out_specs=out_specs,
        out_shape=out_shape,
        scratch_shapes=[pltpu.VMEM((tm, d_model), bf16)],
        compiler_params=pltpu.CompilerParams(
            dimension_semantics=("arbitrary", "arbitrary"),
            vmem_limit_bytes=_VMEM_LIMIT),
        name="in_projection",
    )(x2, norm_in, w_bf, w_bf, w_bf, w_bf, w_bf, q_gain, k_norm, cos_t, sin_t, cos_t.T, sin_t.T)


def _attn_kernel(q_ref, qnext_ref, k_ref, v_ref, sg_ref, o_ref, acc_ref, s_ref, *, tq, tk, nk):
    cols = GQA_GROUP * tq
    q_t = jnp.concatenate([q_ref[0, h] for h in range(GQA_GROUP)], axis=1)
    q_next = jnp.concatenate([qnext_ref[0, h] for h in range(GQA_GROUP)], axis=1)
    acc_ref[...] = jnp.zeros(acc_ref.shape, f32)

    def scores(q_mat, j, slot):
        k = k_ref[0, 0, pl.ds(pl.multiple_of(j * tk, tk), tk), :]
        s_ref[slot] = jnp.dot(k, q_mat, preferred_element_type=f32)

    ones_rows = jnp.ones((_BF16_SUBLANES, tk), bf16)

    def step(j, slot, m_prev, q_ahead, j_ahead):
        scores(q_ahead, j_ahead, 1 - slot)
        off = pl.multiple_of(j * tk, tk)
        v_aug = jnp.concatenate([v_ref[0, 0, :, pl.ds(off, tk)], ones_rows], axis=0)
        s_t = s_ref[slot]
        m_new = jnp.maximum(m_prev, jnp.max(s_t, axis=0, keepdims=True))
        alpha = jnp.exp2(m_prev - m_new)
        p_t = jnp.exp2(s_t - m_new).astype(bf16)
        pv_t = jnp.dot(v_aug, p_t, preferred_element_type=f32)
        acc_ref[...] = alpha * acc_ref[...] + pv_t
        return m_new

    def body(i, m_prev):
        m_mid = step(2 * i, 0, m_prev, q_t, 2 * i + 1)
        last = i == nk // 2 - 1
        return step(2 * i + 1, 1, m_mid, jnp.where(last, q_next, q_t), jnp.where(last, 0, 2 * i + 2))

    @pl.when(pl.program_id(2) == 0)
    def _():
        scores(q_t, 0, 0)

    lax.fori_loop(0, nk // 2, body, jnp.full((1, cols), -jnp.inf, f32))
    o_t = acc_ref[:HEAD_DIM, :] / acc_ref[HEAD_DIM:HEAD_DIM + 1, :]
    for h in range(GQA_GROUP):
        lanes = slice(h * HEAD_DIM, (h + 1) * HEAD_DIM)
        gate = sg_ref[:, lanes].astype(f32)
        o_ref[:, lanes] = (o_t[:, h * tq:(h + 1) * tq].T * gate).astype(bf16)


def _attention(q_t, k, v_t, sg, batch, seq):
    tq, tk = _ATTN_TQ, _ATTN_TK
    nq = seq // tq
    cols = GQA_GROUP * tq
    group_width = GQA_GROUP * HEAD_DIM
    gate_spec = pl.BlockSpec((tq, group_width), lambda b, h, i: (b * nq + i, h))
    return pl.pallas_call(
        functools.partial(_attn_kernel, tq=tq, tk=tk, nk=seq // tk),
        grid=(batch, N_KV_HEADS, nq),
        in_specs=[
            pl.BlockSpec((1, GQA_GROUP, HEAD_DIM, tq), lambda b, h, i: (b, h, 0, i)),
            pl.BlockSpec((1, GQA_GROUP, HEAD_DIM, tq), lambda b, h, i: (b, h, 0, jnp.minimum(i + 1, nq - 1))),
            pl.BlockSpec((1, 1, seq, HEAD_DIM), lambda b, h, i: (b, h, 0, 0)),
            pl.BlockSpec((1, 1, HEAD_DIM, seq), lambda b, h, i: (b, h, 0, 0)),
            gate_spec,
        ],
        out_specs=gate_spec,
        out_shape=jax.ShapeDtypeStruct((batch * seq, ATTN_WIDTH), bf16),
        scratch_shapes=[pltpu.VMEM((HEAD_DIM + _BF16_SUBLANES, cols), f32), pltpu.VMEM((2, tk, cols), f32)],
        compiler_params=pltpu.CompilerParams(
            dimension_semantics=("arbitrary", "arbitrary", "arbitrary"),
            vmem_limit_bytes=_VMEM_LIMIT),
        name="gqa_attention",
    )(q_t, q_t, k, v_t, sg)


def _out_kernel(attn_ref, u_ref, uprev_ref, unext_ref, e_ref, x_ref, w_ref, cw_ref, cb_ref, nf_ref,
                o_ref, *, tm, seq):
    i = pl.program_id(0)
    t0 = (i * tm) % seq
    u = u_ref[...].astype(f32)
    prev_row = uprev_ref[...].astype(f32)[_BF16_SUBLANES - 1:_BF16_SUBLANES, :]
    next_row = unext_ref[...].astype(f32)[0:1, :]
    prev_row = jnp.where(t0 == 0, 0.0, prev_row)
    next_row = jnp.where(t0 + tm == seq, 0.0, next_row)
    row = lax.broadcasted_iota(jnp.int32, u.shape, 0)
    u_m1 = jnp.where(row == 0, prev_row, pltpu.roll(u, 1, 0))
    u_p1 = jnp.where(row == tm - 1, next_row, pltpu.roll(u, tm - 1, 0))
    cw = cw_ref[...]
    y = cb_ref[...] + u_m1 * cw[0:1, :]
    y = y + u * cw[1:2, :]
    y = y + u_p1 * cw[2:3, :]
    conv = (e_ref[...].astype(f32) * y).astype(bf16)
    acc = jnp.dot(attn_ref[...], w_ref[:ATTN_WIDTH, :], preferred_element_type=f32)
    acc = acc + jnp.dot(conv, w_ref[ATTN_WIDTH:, :], preferred_element_type=f32)
    h = x_ref[...] + acc
    ms = jnp.mean(h * h, axis=-1, keepdims=True)
    o_ref[...] = h * lax.rsqrt(ms + EPS) * nf_ref[...]


def _out_projection(attn, u, e, x2, w_bf, conv_w, conv_b, norm_final, seq):
    rows, d_model = x2.shape
    tm = _OUT_ROWS
    conv_width = u.shape[1]
    halo_blocks = rows // _BF16_SUBLANES
    per_tile = tm // _BF16_SUBLANES
    row_tile = lambda width: pl.BlockSpec((tm, width), lambda i: (i, 0))
    full = lambda shape: pl.BlockSpec(shape, lambda i: (0,) * len(shape))
    return pl.pallas_call(
        functools.partial(_out_kernel, tm=tm, seq=seq),
        grid=(rows // tm,),
        in_specs=[
            row_tile(ATTN_WIDTH),
            row_tile(conv_width),
            pl.BlockSpec((_BF16_SUBLANES, conv_width), lambda i: (jnp.maximum(i * per_tile - 1, 0), 0)),
            pl.BlockSpec((_BF16_SUBLANES, conv_width),
                         lambda i: (jnp.minimum((i + 1) * per_tile, halo_blocks - 1), 0)),
            row_tile(conv_width),
            row_tile(d_model),
            full(w_bf.shape), full(conv_w.shape), full(conv_b.shape), full(norm_final.shape),
        ],
        out_specs=row_tile(d_model),
        out_shape=jax.ShapeDtypeStruct((rows, d_model), f32),
        compiler_params=pltpu.CompilerParams(
            dimension_semantics=("arbitrary",),
            vmem_limit_bytes=_VMEM_LIMIT),
        name="out_projection",
    )(attn, u, u, u, e, x2, w_bf, conv_w, conv_b, norm_final)


def kernel(x, norm_in, w_in, q_norm, k_norm, conv_w, conv_b, w_out, norm_final):
    batch, seq, d_model = x.shape
    assert norm_in.shape[0] == 1, "single-layer block"
    assert seq % _PROJ_ROWS == 0 and seq % _OUT_ROWS == 0 and seq % _ATTN_TK == 0 and seq % _ATTN_TQ == 0
    x2 = x.reshape(batch * seq, d_model)
    cos_t, sin_t = _rope_tables(seq)
    q, k, v, sg, u, e = _in_projection(
        x2, norm_in, w_in[0].astype(bf16), q_norm, k_norm, cos_t, sin_t, batch, seq)
    attn = _attention(q, k, v, sg, batch, seq)
    out = _out_projection(attn, u, e, x2, w_out[0].astype(bf16), conv_w[0], conv_b,
                          norm_final.reshape(1, d_model), seq)
    return out.reshape(batch, seq, d_model)
```

```python
import functools

import jax
import jax.numpy as jnp
from jax import lax
from jax.experimental import pallas as pl
from jax.experimental.pallas import tpu as pltpu

HEAD_DIM = 128
N_Q_HEADS = 8
N_KV_HEADS = 2
GQA_GROUP = N_Q_HEADS // N_KV_HEADS
ATTN_WIDTH = N_Q_HEADS * HEAD_DIM
CONV_K = 3
GRID_W = 64
ROPE_THETA = 10000.0
ROPE_AXIS_DIM = HEAD_DIM // 2
EPS = 1e-6
_LOG2_E = 1.4426950408889634

_V7X_VMEM_BYTES = 64 * 2**20
_VMEM_LIMIT = _V7X_VMEM_BYTES - 8 * 2**20
_BF16_SUBLANES = 16
_W_BLOCK = 256

_PROJ_ROWS = 1024
_ATTN_TQ = 128
_ATTN_TK = 1024
_OUT_ROWS = 512

f32 = jnp.float32
bf16 = jnp.bfloat16


def _silu(x):
    return x / (1.0 + jnp.exp(-x))


def _head_norm(x, g):
    ms = jnp.mean(x * x, axis=-1, keepdims=True)
    return x * lax.rsqrt(ms + EPS) * g


def _rope(x, c, s):
    lane = lax.broadcasted_iota(jnp.int32, x.shape, 1)
    first_half = (lane & (ROPE_AXIS_DIM // 2)) == 0
    swapped = jnp.where(first_half,
                        pltpu.roll(x, HEAD_DIM - ROPE_AXIS_DIM // 2, 1),
                        pltpu.roll(x, ROPE_AXIS_DIM // 2, 1))
    return x * c + swapped * s


def _rope_tables(seq_len):
    rows = seq_len // GRID_W
    row = jnp.repeat(jnp.arange(rows, dtype=f32), GRID_W)
    col = jnp.tile(jnp.arange(GRID_W, dtype=f32), rows)
    inv_freq = ROPE_THETA ** (-jnp.arange(0, ROPE_AXIS_DIM, 2, dtype=f32) / ROPE_AXIS_DIM)
    ang_r = row[:, None] * inv_freq[None, :]
    ang_c = col[:, None] * inv_freq[None, :]
    cr, sr, cc, sc = jnp.cos(ang_r), jnp.sin(ang_r), jnp.cos(ang_c), jnp.sin(ang_c)
    cos_t = jnp.concatenate([cr, cr, cc, cc], axis=-1)
    sin_t = jnp.concatenate([-sr, sr, -sc, sc], axis=-1)
    return cos_t, sin_t


_N_PROJ_STEPS = 6


def _wa_idx(j):
    return jnp.where(j < 2, 2 * j, 8 + j)


def _wb_idx(j):
    return jnp.where(j < 2, 2 * j + 1, 12 + j)


def _wc_idx(j):
    return jnp.where(j < 2, 6 + 2 * j, 16 + j)


def _wd_idx(j):
    return jnp.where(j < 2, 7 + 2 * j, 20 + j)


def _we_idx(j):
    return jnp.where(j == 0, 4, 5)


def _rope_t(x_t, cos_tt, sin_tt):
    q = ROPE_AXIS_DIM // 2
    swapped = jnp.concatenate([x_t[q:2 * q], x_t[0:q], x_t[3 * q:4 * q], x_t[2 * q:3 * q]], axis=0)
    return x_t * cos_tt + swapped * sin_tt


def _proj_kernel(x_ref, nin_ref, wa_ref, wb_ref, wc_ref, wd_ref, we_ref, qg_ref, kn_ref,
                 cos_ref, sin_ref, cost_ref, sint_ref,
                 q_ref, k_ref, v_ref, sg_ref, u_ref, e_ref, hn_ref):
    j = pl.program_id(1)
    tm = hn_ref.shape[0]

    @pl.when(j == 0)
    def _():
        x = x_ref[...]
        ms = jnp.mean(x * x, axis=-1, keepdims=True)
        hn_ref[...] = (x * lax.rsqrt(ms + EPS) * nin_ref[...]).astype(bf16)

    def proj(w_ref):
        return jnp.dot(hn_ref[...], w_ref[...], preferred_element_type=f32)

    def q_and_gates():
        cos_tt, sin_tt = cost_ref[...], sint_ref[...]
        gain = jnp.tile(qg_ref[...], (1, tm // HEAD_DIM))
        scale = HEAD_DIM ** -0.5 * _LOG2_E
        for half, w_ref in enumerate((wa_ref, wb_ref)):
            qq = proj(w_ref)
            for hh in range(2):
                q_t = qq[:, hh * HEAD_DIM:(hh + 1) * HEAD_DIM].T
                ms = jnp.mean(q_t * q_t, axis=0, keepdims=True)
                r = lax.rsqrt(ms + EPS) * scale
                q_ref[0, 2 * half + hh] = (_rope_t(q_t * gain, cos_tt, sin_tt) * r).astype(bf16)
        sg_ref[:, :_W_BLOCK] = _silu(proj(wc_ref)).astype(bf16)
        sg_ref[:, _W_BLOCK:] = _silu(proj(wd_ref)).astype(bf16)

    @pl.when(j == 0)
    def _():
        kk = proj(we_ref)
        for hh in range(N_KV_HEADS):
            kh = kk[:, hh * HEAD_DIM:(hh + 1) * HEAD_DIM]
            k_ref[0, hh] = _rope(_head_norm(kh, kn_ref[...]), cos_ref[...], sin_ref[...]).astype(bf16)
        q_and_gates()

    @pl.when(j == 1)
    def _():
        vv = proj(we_ref)
        for hh in range(N_KV_HEADS):
            v_ref[0, hh] = vv[:, hh * HEAD_DIM:(hh + 1) * HEAD_DIM].T.astype(bf16)
        q_and_gates()

    @pl.when(j >= 2)
    def _():
        cb = proj(wa_ref)
        cc = proj(wb_ref)
        cx = proj(wc_ref)
        gc = proj(wd_ref)
        u_ref[...] = (cc * cx).astype(bf16)
        e_ref[...] = (cb * _silu(gc)).astype(bf16)


def _in_projection(x2, norm_in, w_bf, q_norm, k_norm, cos_t, sin_t, batch, seq):
    rows, d_model = x2.shape
    tm = _PROJ_ROWS
    nt = seq // tm
    w_spec = lambda idx: pl.BlockSpec((d_model, _W_BLOCK), lambda i, j: (0, idx(j)))
    full = lambda shape: pl.BlockSpec(shape, lambda i, j: (0,) * len(shape))
    table = pl.BlockSpec((tm, HEAD_DIM), lambda i, j: (i % nt, 0))
    table_t = pl.BlockSpec((HEAD_DIM, tm), lambda i, j: (0, i % nt))
    q_gain = jnp.broadcast_to(q_norm.reshape(HEAD_DIM, 1), (HEAD_DIM, HEAD_DIM))
    conv_width = 4 * _W_BLOCK
    out_shape = (
        jax.ShapeDtypeStruct((batch, N_Q_HEADS, HEAD_DIM, seq), bf16),
        jax.ShapeDtypeStruct((batch, N_KV_HEADS, seq, HEAD_DIM), bf16),
        jax.ShapeDtypeStruct((batch, N_KV_HEADS, HEAD_DIM, seq), bf16),
        jax.ShapeDtypeStruct((rows, ATTN_WIDTH), bf16),
        jax.ShapeDtypeStruct((rows, conv_width), bf16),
        jax.ShapeDtypeStruct((rows, conv_width), bf16),
    )
    out_specs = (
        pl.BlockSpec((1, GQA_GROUP, HEAD_DIM, tm), lambda i, j: (i // nt, jnp.minimum(j, 1), 0, i % nt)),
        pl.BlockSpec((1, N_KV_HEADS, tm, HEAD_DIM), lambda i, j: (i // nt, 0, i % nt, 0)),
        pl.BlockSpec((1, N_KV_HEADS, HEAD_DIM, tm), lambda i, j: (i // nt, 0, 0, i % nt)),
        pl.BlockSpec((tm, 2 * _W_BLOCK), lambda i, j: (i, jnp.minimum(j, 1))),
        pl.BlockSpec((tm, _W_BLOCK), lambda i, j: (i, jnp.maximum(j - 2, 0))),
        pl.BlockSpec((tm, _W_BLOCK), lambda i, j: (i, jnp.maximum(j - 2, 0))),
    )
    return pl.pallas_call(
        _proj_kernel,
        grid=(rows // tm, _N_PROJ_STEPS),
        in_specs=[
            pl.BlockSpec((tm, d_model), lambda i, j: (i, 0)),
            full((1, d_model)),
            w_spec(_wa_idx), w_spec(_wb_idx), w_spec(_wc_idx), w_spec(_wd_idx), w_spec(_we_idx),
            full((HEAD_DIM, HEAD_DIM)), full((1, HEAD_DIM)),
            table, table, table_t, table_t,
        ],
        out_specs=out_specs,
        out_shape=out_shape,
        scratch_shapes=[pltpu.VMEM((tm, d_model), bf16)],
        compiler_params=pltpu.CompilerParams(
            dimension_semantics=("arbitrary", "arbitrary"),
            vmem_limit_bytes=_VMEM_LIMIT),
        name="in_projection",
    )(x2, norm_in, w_bf, w_bf, w_bf, w_bf, w_bf, q_gain, k_norm, cos_t, sin_t, cos_t.T, sin_t.T)


def _attn_kernel(q_ref, k_ref, v_ref, sg_ref, o_ref, acc_ref, s_ref, *, tq, tk, nk, nq):
    cols = GQA_GROUP * tq

    def q_tile(i):
        off = pl.multiple_of(i * tq, tq)
        return jnp.concatenate([q_ref[0, h, :, pl.ds(off, tq)] for h in range(GQA_GROUP)], axis=1)

    def scores(q_mat, j, slot):
        k = k_ref[0, 0, pl.ds(pl.multiple_of(j * tk, tk), tk), :]
        s_ref[slot] = jnp.dot(k, q_mat, preferred_element_type=f32)

    ones_rows = jnp.ones((_BF16_SUBLANES, tk), bf16)

    def step(j, slot, m_prev, q_ahead, j_ahead):
        scores(q_ahead, j_ahead, 1 - slot)
        off = pl.multiple_of(j * tk, tk)
        v_aug = jnp.concatenate([v_ref[0, 0, :, pl.ds(off, tk)], ones_rows], axis=0)
        s_t = s_ref[slot]
        m_new = jnp.maximum(m_prev, jnp.max(s_t, axis=0, keepdims=True))
        alpha = jnp.exp2(m_prev - m_new)
        p_t = jnp.exp2(s_t - m_new).astype(bf16)
        pv_t = jnp.dot(v_aug, p_t, preferred_element_type=f32)
        acc_ref[...] = alpha * acc_ref[...] + pv_t
        return m_new

    def query_tile(i, carry):
        q_t = q_tile(i)
        q_next = q_tile(jnp.minimum(i + 1, nq - 1))
        acc_ref[...] = jnp.zeros(acc_ref.shape, f32)

        m_run = jnp.full((1, cols), -jnp.inf, f32)
        for c in range(nk):
            last = c == nk - 1
            m_run = step(c, c % 2, m_run, q_next if last else q_t, 0 if last else c + 1)
        o_t = acc_ref[:HEAD_DIM, :] / acc_ref[HEAD_DIM:HEAD_DIM + 1, :]
        rows = pl.ds(pl.multiple_of(i * tq, tq), tq)
        for h in range(GQA_GROUP):
            lanes = slice(h * HEAD_DIM, (h + 1) * HEAD_DIM)
            gate = sg_ref[rows, lanes].astype(f32)
            o_ref[rows, lanes] = (o_t[:, h * tq:(h + 1) * tq].T * gate).astype(bf16)
        return carry

    scores(q_tile(0), 0, 0)
    lax.fori_loop(0, nq, query_tile, 0)


def _attention(q_t, k, v_t, sg, batch, seq):
    tq, tk = _ATTN_TQ, _ATTN_TK
    cols = GQA_GROUP * tq
    group_width = GQA_GROUP * HEAD_DIM
    gate_spec = pl.BlockSpec((seq, group_width), lambda b, h: (b, h))
    return pl.pallas_call(
        functools.partial(_attn_kernel, tq=tq, tk=tk, nk=seq // tk, nq=seq // tq),
        grid=(batch, N_KV_HEADS),
        in_specs=[
            pl.BlockSpec((1, GQA_GROUP, HEAD_DIM, seq), lambda b, h: (b, h, 0, 0)),
            pl.BlockSpec((1, 1, seq, HEAD_DIM), lambda b, h: (b, h, 0, 0)),
            pl.BlockSpec((1, 1, HEAD_DIM, seq), lambda b, h: (b, h, 0, 0)),
            gate_spec,
        ],
        out_specs=gate_spec,
        out_shape=jax.ShapeDtypeStruct((batch * seq, ATTN_WIDTH), bf16),
        scratch_shapes=[pltpu.VMEM((HEAD_DIM + _BF16_SUBLANES, cols), f32), pltpu.VMEM((2, tk, cols), f32)],
        compiler_params=pltpu.CompilerParams(
            dimension_semantics=("arbitrary", "arbitrary"),
            vmem_limit_bytes=_VMEM_LIMIT),
        name="gqa_attention",
    )(q_t, k, v_t, sg)


def _out_kernel(attn_ref, u_ref, uprev_ref, unext_ref, e_ref, x_ref, w_ref, cw_ref, cb_ref, nf_ref,
                o_ref, *, tm, seq):
    i = pl.program_id(0)
    t0 = (i * tm) % seq
    u = u_ref[...].astype(f32)
    prev_row = uprev_ref[...].astype(f32)[_BF16_SUBLANES - 1:_BF16_SUBLANES, :]
    next_row = unext_ref[...].astype(f32)[0:1, :]
    prev_row = jnp.where(t0 == 0, 0.0, prev_row)
    next_row = jnp.where(t0 + tm == seq, 0.0, next_row)
    row = lax.broadcasted_iota(jnp.int32, u.shape, 0)
    u_m1 = jnp.where(row == 0, prev_row, pltpu.roll(u, 1, 0))
    u_p1 = jnp.where(row == tm - 1, next_row, pltpu.roll(u, tm - 1, 0))
    cw = cw_ref[...]
    y = cb_ref[...] + u_m1 * cw[0:1, :]
    y = y + u * cw[1:2, :]
    y = y + u_p1 * cw[2:3, :]
    conv = (e_ref[...].astype(f32) * y).astype(bf16)
    acc = jnp.dot(attn_ref[...], w_ref[:ATTN_WIDTH, :], preferred_element_type=f32)
    acc = acc + jnp.dot(conv, w_ref[ATTN_WIDTH:, :], preferred_element_type=f32)
    h = x_ref[...] + acc
    ms = jnp.mean(h * h, axis=-1, keepdims=True)
    o_ref[...] = h * lax.rsqrt(ms + EPS) * nf_ref[...]


def _out_projection(attn, u, e, x2, w_bf, conv_w, conv_b, norm_final, seq):
    rows, d_model = x2.shape
    tm = _OUT_ROWS
    conv_width = u.shape[1]
    halo_blocks = rows // _BF16_SUBLANES
    per_tile = tm // _BF16_SUBLANES
    row_tile = lambda width: pl.BlockSpec((tm, width), lambda i: (i, 0))
    full = lambda shape: pl.BlockSpec(shape, lambda i: (0,) * len(shape))
    return pl.pallas_call(
        functools.partial(_out_kernel, tm=tm, seq=seq),
        grid=(rows // tm,),
        in_specs=[
            row_tile(ATTN_WIDTH),
            row_tile(conv_width),
            pl.BlockSpec((_BF16_SUBLANES, conv_width), lambda i: (jnp.maximum(i * per_tile - 1, 0), 0)),
            pl.BlockSpec((_BF16_SUBLANES, conv_width),
                         lambda i: (jnp.minimum((i + 1) * per_tile, halo_blocks - 1), 0)),
            row_tile(conv_width),
            row_tile(d_model),
            full(w_bf.shape), full(conv_w.shape), full(conv_b.shape), full(norm_final.shape),
        ],
        out_specs=row_tile(d_model),
        out_shape=jax.ShapeDtypeStruct((rows, d_model), f32),
        compiler_params=pltpu.CompilerParams(
            dimension_semantics=("arbitrary",),
            vmem_limit_bytes=_VMEM_LIMIT),
        name="out_projection",
    )(attn, u, u, u, e, x2, w_bf, conv_w, conv_b, norm_final)


def kernel(x, norm_in, w_in, q_norm, k_norm, conv_w, conv_b, w_out, norm_final):
    batch, seq, d_model = x.shape
    assert norm_in.shape[0] == 1, "single-layer block"
    assert seq % _PROJ_ROWS == 0 and seq % _OUT_ROWS == 0 and seq % _ATTN_TK == 0 and seq % _ATTN_TQ == 0
    x2 = x.reshape(batch * seq, d_model)
    cos_t, sin_t = _rope_tables(seq)
    q, k, v, sg, u, e = _in_projection(
        x2, norm_in, w_in[0].astype(bf16), q_norm, k_norm, cos_t, sin_t, batch, seq)
    attn = _attention(q, k, v, sg, batch, seq)
    out = _out_projection(attn, u, e, x2, w_out[0].astype(bf16), conv_w[0], conv_b,
                          norm_final.reshape(1, d_model), seq)
    return out.reshape(batch, seq, d_model)
```

```python
import functools

import jax
import jax.numpy as jnp
from jax import lax
from jax.experimental import pallas as pl
from jax.experimental.pallas import tpu as pltpu

HEAD_DIM = 128
N_Q_HEADS = 8
N_KV_HEADS = 2
GQA_GROUP = N_Q_HEADS // N_KV_HEADS
ATTN_WIDTH = N_Q_HEADS * HEAD_DIM
CONV_K = 3
GRID_W = 64
ROPE_THETA = 10000.0
ROPE_AXIS_DIM = HEAD_DIM // 2
EPS = 1e-6
_LOG2_E = 1.4426950408889634

_V7X_VMEM_BYTES = 64 * 2**20
_VMEM_LIMIT = _V7X_VMEM_BYTES - 8 * 2**20
_BF16_SUBLANES = 16
_W_BLOCK = 256

_PROJ_ROWS = 1024
_ATTN_TQ = 128
_ATTN_TK = 1024
_OUT_ROWS = 512

f32 = jnp.float32
bf16 = jnp.bfloat16


def _silu(x):
    return x / (1.0 + jnp.exp(-x))


def _head_norm(x, g):
    ms = jnp.mean(x * x, axis=-1, keepdims=True)
    return x * lax.rsqrt(ms + EPS) * g


def _rope(x, c, s):
    lane = lax.broadcasted_iota(jnp.int32, x.shape, 1)
    first_half = (lane & (ROPE_AXIS_DIM // 2)) == 0
    swapped = jnp.where(first_half,
                        pltpu.roll(x, HEAD_DIM - ROPE_AXIS_DIM // 2, 1),
                        pltpu.roll(x, ROPE_AXIS_DIM // 2, 1))
    return x * c + swapped * s


def _rope_tables(seq_len):
    rows = seq_len // GRID_W
    row = jnp.repeat(jnp.arange(rows, dtype=f32), GRID_W)
    col = jnp.tile(jnp.arange(GRID_W, dtype=f32), rows)
    inv_freq = ROPE_THETA ** (-jnp.arange(0, ROPE_AXIS_DIM, 2, dtype=f32) / ROPE_AXIS_DIM)
    ang_r = row[:, None] * inv_freq[None, :]
    ang_c = col[:, None] * inv_freq[None, :]
    cr, sr, cc, sc = jnp.cos(ang_r), jnp.sin(ang_r), jnp.cos(ang_c), jnp.sin(ang_c)
    cos_t = jnp.concatenate([cr, cr, cc, cc], axis=-1)
    sin_t = jnp.concatenate([-sr, sr, -sc, sc], axis=-1)
    return cos_t, sin_t


_N_PROJ_STEPS = 6


def _wa_idx(j):
    return jnp.where(j < 2, 2 * j, 8 + j)


def _wb_idx(j):
    return jnp.where(j < 2, 2 * j + 1, 12 + j)


def _wc_idx(j):
    return jnp.where(j < 2, 6 + 2 * j, 16 + j)


def _wd_idx(j):
    return jnp.where(j < 2, 7 + 2 * j, 20 + j)


def _we_idx(j):
    return jnp.where(j == 0, 4, 5)


def _rope_t(x_t, cos_tt, sin_tt):
    q = ROPE_AXIS_DIM // 2
    swapped = jnp.concatenate([x_t[q:2 * q], x_t[0:q], x_t[3 * q:4 * q], x_t[2 * q:3 * q]], axis=0)
    return x_t * cos_tt + swapped * sin_tt


def _proj_kernel(x_ref, nin_ref, wa_ref, wb_ref, wc_ref, wd_ref, we_ref, qg_ref, kn_ref,
                 cos_ref, sin_ref, cost_ref, sint_ref,
                 q_ref, k_ref, v_ref, sg_ref, u_ref, e_ref, hn_ref):
    j = pl.program_id(1)
    tm = hn_ref.shape[0]

    @pl.when(j == 0)
    def _():
        x = x_ref[...]
        ms = jnp.mean(x * x, axis=-1, keepdims=True)
        hn_ref[...] = (x * lax.rsqrt(ms + EPS) * nin_ref[...]).astype(bf16)

    def proj(w_ref):
        return jnp.dot(hn_ref[...], w_ref[...], preferred_element_type=f32)

    def q_and_gates():
        cos_tt, sin_tt = cost_ref[...], sint_ref[...]
        gain = jnp.tile(qg_ref[...], (1, tm // HEAD_DIM))
        scale = HEAD_DIM ** -0.5 * _LOG2_E
        for half, w_ref in enumerate((wa_ref, wb_ref)):
            qq = proj(w_ref)
            for hh in range(2):
                q_t = qq[:, hh * HEAD_DIM:(hh + 1) * HEAD_DIM].T
                ms = jnp.mean(q_t * q_t, axis=0, keepdims=True)
                r = lax.rsqrt(ms + EPS) * scale
                q_ref[0, 2 * half + hh] = (_rope_t(q_t * gain, cos_tt, sin_tt) * r).astype(bf16)
        sg_ref[:, :_W_BLOCK] = _silu(proj(wc_ref)).astype(bf16)
        sg_ref[:, _W_BLOCK:] = _silu(proj(wd_ref)).astype(bf16)

    @pl.when(j == 0)
    def _():
        kk = proj(we_ref)
        for hh in range(N_KV_HEADS):
            kh = kk[:, hh * HEAD_DIM:(hh + 1) * HEAD_DIM]
            k_ref[0, hh] = _rope(_head_norm(kh, kn_ref[...]), cos_ref[...], sin_ref[...]).astype(bf16)
        q_and_gates()

    @pl.when(j == 1)
    def _():
        vv = proj(we_ref)
        for hh in range(N_KV_HEADS):
            v_ref[0, hh] = vv[:, hh * HEAD_DIM:(hh + 1) * HEAD_DIM].T.astype(bf16)
        q_and_gates()

    @pl.when(j >= 2)
    def _():
        cb = proj(wa_ref)
        cc = proj(wb_ref)
        cx = proj(wc_ref)
        gc = proj(wd_ref)
        u_ref[...] = (cc * cx).astype(bf16)
        e_ref[...] = (cb * _silu(gc)).astype(bf16)


def _in_projection(x2, norm_in, w_bf, q_norm, k_norm, cos_t, sin_t, batch, seq):
    rows, d_model = x2.shape
    tm = _PROJ_ROWS
    nt = seq // tm
    w_spec = lambda idx: pl.BlockSpec((d_model, _W_BLOCK), lambda i, j: (0, idx(j)))
    full = lambda shape: pl.BlockSpec(shape, lambda i, j: (0,) * len(shape))
    table = pl.BlockSpec((tm, HEAD_DIM), lambda i, j: (i % nt, 0))
    table_t = pl.BlockSpec((HEAD_DIM, tm), lambda i, j: (0, i % nt))
    q_gain = jnp.broadcast_to(q_norm.reshape(HEAD_DIM, 1), (HEAD_DIM, HEAD_DIM))
    conv_width = 4 * _W_BLOCK
    out_shape = (
        jax.ShapeDtypeStruct((batch, N_Q_HEADS, HEAD_DIM, seq), bf16),
        jax.ShapeDtypeStruct((batch, N_KV_HEADS, seq, HEAD_DIM), bf16),
        jax.ShapeDtypeStruct((batch, N_KV_HEADS, HEAD_DIM, seq), bf16),
        jax.ShapeDtypeStruct((rows, ATTN_WIDTH), bf16),
        jax.ShapeDtypeStruct((rows, conv_width), bf16),
        jax.ShapeDtypeStruct((rows, conv_width), bf16),
    )
    out_specs = (
        pl.BlockSpec((1, GQA_GROUP, HEAD_DIM, tm), lambda i, j: (i // nt, jnp.minimum(j, 1), 0, i % nt)),
        pl.BlockSpec((1, N_KV_HEADS, tm, HEAD_DIM), lambda i, j: (i // nt, 0, i % nt, 0)),
        pl.BlockSpec((1, N_KV_HEADS, HEAD_DIM, tm), lambda i, j: (i // nt, 0, 0, i % nt)),
        pl.BlockSpec((tm, 2 * _W_BLOCK), lambda i, j: (i, jnp.minimum(j, 1))),
        pl.BlockSpec((tm, _W_BLOCK), lambda i, j: (i, jnp.maximum(j - 2, 0))),
        pl.BlockSpec((tm, _W_BLOCK), lambda i, j: (i, jnp.maximum(j - 2, 0))),
    )
    return pl.pallas_call(
        _proj_kernel,
        grid=(rows // tm, _N_PROJ_STEPS),
        in_specs=[
            pl.BlockSpec((tm, d_model), lambda i, j: (i, 0)),
            full((1, d_model)),
            w_spec(_wa_idx), w_spec(_wb_idx), w_spec(_wc_idx), w_spec(_wd_idx), w_spec(_we_idx),
            full((HEAD_DIM, HEAD_DIM)), full((1, HEAD_DIM)),
            table, table, table_t, table_t,
        ],
        out_specs=out_specs,
        out_shape=out_shape,
        scratch_shapes=[pltpu.VMEM((tm, d_model), bf16)],
        compiler_params=pltpu.CompilerParams(
            dimension_semantics=("arbitrary", "arbitrary"),
            vmem_limit_bytes=_VMEM_LIMIT),
        name="in_projection",
    )(x2, norm_in, w_bf, w_bf, w_bf, w_bf, w_bf, q_gain, k_norm, cos_t, sin_t, cos_t.T, sin_t.T)


def _attn_kernel(q_ref, k_ref, v_ref, sg_ref, o_ref, acc_ref, s_ref, *, tq, tk, nk, nq):
    cols = GQA_GROUP * tq

    def q_tile(i):
        off = pl.multiple_of(i * tq, tq)
        return jnp.concatenate([q_ref[0, h, :, pl.ds(off, tq)] for h in range(GQA_GROUP)], axis=1)

    def scores(q_mat, j, slot):
        k = k_ref[0, 0, pl.ds(pl.multiple_of(j * tk, tk), tk), :]
        s_t = jnp.dot(k, q_mat, preferred_element_type=f32)
        s_ref[slot] = s_t
        return jnp.max(s_t, axis=0, keepdims=True)

    ones_rows = jnp.ones((_BF16_SUBLANES, tk), bf16)

    def step(j, slot, m_prev, chunk_max, q_ahead, j_ahead):
        ahead_max = scores(q_ahead, j_ahead, 1 - slot)
        off = pl.multiple_of(j * tk, tk)
        v_aug = jnp.concatenate([v_ref[0, 0, :, pl.ds(off, tk)], ones_rows], axis=0)
        m_new = jnp.maximum(m_prev, chunk_max)
        alpha = jnp.exp2(m_prev - m_new)
        p_t = jnp.exp2(s_ref[slot] - m_new).astype(bf16)
        pv_t = jnp.dot(v_aug, p_t, preferred_element_type=f32)
        acc_ref[...] = alpha * acc_ref[...] + pv_t
        return m_new, ahead_max

    def query_tile(i, chunk_max):
        q_t = q_tile(i)
        q_next = q_tile(jnp.minimum(i + 1, nq - 1))
        acc_ref[...] = jnp.zeros(acc_ref.shape, f32)

        m_run = jnp.full((1, cols), -jnp.inf, f32)
        for c in range(nk):
            last = c == nk - 1
            m_run, chunk_max = step(c, c % 2, m_run, chunk_max, q_next if last else q_t, 0 if last else c + 1)
        o_t = acc_ref[:HEAD_DIM, :] / acc_ref[HEAD_DIM:HEAD_DIM + 1, :]
        rows = pl.ds(pl.multiple_of(i * tq, tq), tq)
        for h in range(GQA_GROUP):
            lanes = slice(h * HEAD_DIM, (h + 1) * HEAD_DIM)
            gate = sg_ref[rows, lanes].astype(f32)
            o_ref[rows, lanes] = (o_t[:, h * tq:(h + 1) * tq].T * gate).astype(bf16)
        return chunk_max

    lax.fori_loop(0, nq, query_tile, scores(q_tile(0), 0, 0))


def _attention(q_t, k, v_t, sg, batch, seq):
    tq, tk = _ATTN_TQ, _ATTN_TK
    cols = GQA_GROUP * tq
    group_width = GQA_GROUP * HEAD_DIM
    gate_spec = pl.BlockSpec((seq, group_width), lambda b, h: (b, h))
    return pl.pallas_call(
        functools.partial(_attn_kernel, tq=tq, tk=tk, nk=seq // tk, nq=seq // tq),
        grid=(batch, N_KV_HEADS),
        in_specs=[
            pl.BlockSpec((1, GQA_GROUP, HEAD_DIM, seq), lambda b, h: (b, h, 0, 0)),
            pl.BlockSpec((1, 1, seq, HEAD_DIM), lambda b, h: (b, h, 0, 0)),
            pl.BlockSpec((1, 1, HEAD_DIM, seq), lambda b, h: (b, h, 0, 0)),
            gate_spec,
        ],
        out_specs=gate_spec,
        out_shape=jax.ShapeDtypeStruct((batch * seq, ATTN_WIDTH), bf16),
        scratch_shapes=[pltpu.VMEM((HEAD_DIM + _BF16_SUBLANES, cols), f32), pltpu.VMEM((2, tk, cols), f32)],
        compiler_params=pltpu.CompilerParams(
            dimension_semantics=("arbitrary", "arbitrary"),
            vmem_limit_bytes=_VMEM_LIMIT),
        name="gqa_attention",
    )(q_t, k, v_t, sg)


def _out_kernel(attn_ref, u_ref, uprev_ref, unext_ref, e_ref, x_ref, w_ref, cw_ref, cb_ref, nf_ref,
                o_ref, *, tm, seq):
    i = pl.program_id(0)
    t0 = (i * tm) % seq
    u = u_ref[...].astype(f32)
    prev_row = uprev_ref[...].astype(f32)[_BF16_SUBLANES - 1:_BF16_SUBLANES, :]
    next_row = unext_ref[...].astype(f32)[0:1, :]
    prev_row = jnp.where(t0 == 0, 0.0, prev_row)
    next_row = jnp.where(t0 + tm == seq, 0.0, next_row)
    row = lax.broadcasted_iota(jnp.int32, u.shape, 0)
    u_m1 = jnp.where(row == 0, prev_row, pltpu.roll(u, 1, 0))
    u_p1 = jnp.where(row == tm - 1, next_row, pltpu.roll(u, tm - 1, 0))
    cw = cw_ref[...]
    y = cb_ref[...] + u_m1 * cw[0:1, :]
    y = y + u * cw[1:2, :]
    y = y + u_p1 * cw[2:3, :]
    conv = (e_ref[...].astype(f32) * y).astype(bf16)
    acc = jnp.dot(attn_ref[...], w_ref[:ATTN_WIDTH, :], preferred_element_type=f32)
    acc = acc + jnp.dot(conv, w_ref[ATTN_WIDTH:, :], preferred_element_type=f32)
    h = x_ref[...] + acc
    ms = jnp.mean(h * h, axis=-1, keepdims=True)
    o_ref[...] = h * lax.rsqrt(ms + EPS) * nf_ref[...]


def _out_projection(attn, u, e, x2, w_bf, conv_w, conv_b, norm_final, seq):
    rows, d_model = x2.shape
    tm = _OUT_ROWS
    conv_width = u.shape[1]
    halo_blocks = rows // _BF16_SUBLANES
    per_tile = tm // _BF16_SUBLANES
    row_tile = lambda width: pl.BlockSpec((tm, width), lambda i: (i, 0))
    full = lambda shape: pl.BlockSpec(shape, lambda i: (0,) * len(shape))
    return pl.pallas_call(
        functools.partial(_out_kernel, tm=tm, seq=seq),
        grid=(rows // tm,),
        in_specs=[
            row_tile(ATTN_WIDTH),
            row_tile(conv_width),
            pl.BlockSpec((_BF16_SUBLANES, conv_width), lambda i: (jnp.maximum(i * per_tile - 1, 0), 0)),
            pl.BlockSpec((_BF16_SUBLANES, conv_width),
                         lambda i: (jnp.minimum((i + 1) * per_tile, halo_blocks - 1), 0)),
            row_tile(conv_width),
            row_tile(d_model),
            full(w_bf.shape), full(conv_w.shape), full(conv_b.shape), full(norm_final.shape),
        ],
        out_specs=row_tile(d_model),
        out_shape=jax.ShapeDtypeStruct((rows, d_model), f32),
        compiler_params=pltpu.CompilerParams(
            dimension_semantics=("arbitrary",),
            vmem_limit_bytes=_VMEM_LIMIT),
        name="out_projection",
    )(attn, u, u, u, e, x2, w_bf, conv_w, conv_b, norm_final)


def kernel(x, norm_in, w_in, q_norm, k_norm, conv_w, conv_b, w_out, norm_final):
    batch, seq, d_model = x.shape
    assert norm_in.shape[0] == 1, "single-layer block"
    assert seq % _PROJ_ROWS == 0 and seq % _OUT_ROWS == 0 and seq % _ATTN_TK == 0 and seq % _ATTN_TQ == 0
    x2 = x.reshape(batch * seq, d_model)
    cos_t, sin_t = _rope_tables(seq)
    q, k, v, sg, u, e = _in_projection(
        x2, norm_in, w_in[0].astype(bf16), q_norm, k_norm, cos_t, sin_t, batch, seq)
    attn = _attention(q, k, v, sg, batch, seq)
    out = _out_projection(attn, u, e, x2, w_out[0].astype(bf16), conv_w[0], conv_b,
                          norm_final.reshape(1, d_model), seq)
    return out.reshape(batch, seq, d_model)
```

```python
import functools

import jax
import jax.numpy as jnp
from jax import lax
from jax.experimental import pallas as pl
from jax.experimental.pallas import tpu as pltpu

HEAD_DIM = 128
N_Q_HEADS = 8
N_KV_HEADS = 2
GQA_GROUP = N_Q_HEADS // N_KV_HEADS
ATTN_WIDTH = N_Q_HEADS * HEAD_DIM
CONV_K = 3
GRID_W = 64
ROPE_THETA = 10000.0
ROPE_AXIS_DIM = HEAD_DIM // 2
EPS = 1e-6
_LOG2_E = 1.4426950408889634

_V7X_VMEM_BYTES = 64 * 2**20
_VMEM_LIMIT = _V7X_VMEM_BYTES - 8 * 2**20
_BF16_SUBLANES = 16
_W_BLOCK = 256

_PROJ_ROWS = 1024
_ATTN_TQ = 128
_ATTN_TK = 1024
_ATTN_TILES_PER_ITER = 4
_OUT_ROWS = 512

f32 = jnp.float32
bf16 = jnp.bfloat16


def _silu(x):
    return x / (1.0 + jnp.exp(-x))


def _head_norm(x, g):
    ms = jnp.mean(x * x, axis=-1, keepdims=True)
    return x * lax.rsqrt(ms + EPS) * g


def _rope(x, c, s):
    lane = lax.broadcasted_iota(jnp.int32, x.shape, 1)
    first_half = (lane & (ROPE_AXIS_DIM // 2)) == 0
    swapped = jnp.where(first_half,
                        pltpu.roll(x, HEAD_DIM - ROPE_AXIS_DIM // 2, 1),
                        pltpu.roll(x, ROPE_AXIS_DIM // 2, 1))
    return x * c + swapped * s


def _rope_tables(seq_len):
    rows = seq_len // GRID_W
    row = jnp.repeat(jnp.arange(rows, dtype=f32), GRID_W)
    col = jnp.tile(jnp.arange(GRID_W, dtype=f32), rows)
    inv_freq = ROPE_THETA ** (-jnp.arange(0, ROPE_AXIS_DIM, 2, dtype=f32) / ROPE_AXIS_DIM)
    ang_r = row[:, None] * inv_freq[None, :]
    ang_c = col[:, None] * inv_freq[None, :]
    cr, sr, cc, sc = jnp.cos(ang_r), jnp.sin(ang_r), jnp.cos(ang_c), jnp.sin(ang_c)
    cos_t = jnp.concatenate([cr, cr, cc, cc], axis=-1)
    sin_t = jnp.concatenate([-sr, sr, -sc, sc], axis=-1)
    return cos_t, sin_t


_N_PROJ_STEPS = 6


def _wa_idx(j):
    return jnp.where(j < 2, 2 * j, 8 + j)


def _wb_idx(j):
    return jnp.where(j < 2, 2 * j + 1, 12 + j)


def _wc_idx(j):
    return jnp.where(j < 2, 6 + 2 * j, 16 + j)


def _wd_idx(j):
    return jnp.where(j < 2, 7 + 2 * j, 20 + j)


def _we_idx(j):
    return jnp.where(j == 0, 4, 5)


def _rope_t(x_t, cos_tt, sin_tt):
    q = ROPE_AXIS_DIM // 2
    swapped = jnp.concatenate([x_t[q:2 * q], x_t[0:q], x_t[3 * q:4 * q], x_t[2 * q:3 * q]], axis=0)
    return x_t * cos_tt + swapped * sin_tt


def _proj_kernel(x_ref, nin_ref, wa_ref, wb_ref, wc_ref, wd_ref, we_ref, qg_ref, kn_ref,
                 cos_ref, sin_ref, cost_ref, sint_ref,
                 q_ref, k_ref, v_ref, sg_ref, u_ref, e_ref, hn_ref):
    j = pl.program_id(1)
    tm = hn_ref.shape[0]

    @pl.when(j == 0)
    def _():
        x = x_ref[...]
        ms = jnp.mean(x * x, axis=-1, keepdims=True)
        hn_ref[...] = (x * lax.rsqrt(ms + EPS) * nin_ref[...]).astype(bf16)

    def proj(w_ref):
        return jnp.dot(hn_ref[...], w_ref[...], preferred_element_type=f32)

    def q_and_gates():
        cos_tt, sin_tt = cost_ref[...], sint_ref[...]
        gain = jnp.tile(qg_ref[...], (1, tm // HEAD_DIM))
        scale = HEAD_DIM ** -0.5 * _LOG2_E
        for half, w_ref in enumerate((wa_ref, wb_ref)):
            qq = proj(w_ref)
            for hh in range(2):
                q_t = qq[:, hh * HEAD_DIM:(hh + 1) * HEAD_DIM].T
                ms = jnp.mean(q_t * q_t, axis=0, keepdims=True)
                r = lax.rsqrt(ms + EPS) * scale
                q_ref[0, 2 * half + hh] = (_rope_t(q_t * gain, cos_tt, sin_tt) * r).astype(bf16)
        sg_ref[:, :_W_BLOCK] = _silu(proj(wc_ref)).astype(bf16)
        sg_ref[:, _W_BLOCK:] = _silu(proj(wd_ref)).astype(bf16)

    @pl.when(j == 0)
    def _():
        kk = proj(we_ref)
        for hh in range(N_KV_HEADS):
            kh = kk[:, hh * HEAD_DIM:(hh + 1) * HEAD_DIM]
            k_ref[0, hh] = _rope(_head_norm(kh, kn_ref[...]), cos_ref[...], sin_ref[...]).astype(bf16)
        q_and_gates()

    @pl.when(j == 1)
    def _():
        vv = proj(we_ref)
        for hh in range(N_KV_HEADS):
            v_ref[0, hh] = vv[:, hh * HEAD_DIM:(hh + 1) * HEAD_DIM].T.astype(bf16)
        q_and_gates()

    @pl.when(j >= 2)
    def _():
        cb = proj(wa_ref)
        cc = proj(wb_ref)
        cx = proj(wc_ref)
        gc = proj(wd_ref)
        u_ref[...] = (cc * cx).astype(bf16)
        e_ref[...] = (cb * _silu(gc)).astype(bf16)


def _in_projection(x2, norm_in, w_bf, q_norm, k_norm, cos_t, sin_t, batch, seq):
    rows, d_model = x2.shape
    tm = _PROJ_ROWS
    nt = seq // tm
    w_spec = lambda idx: pl.BlockSpec((d_model, _W_BLOCK), lambda i, j: (0, idx(j)))
    full = lambda shape: pl.BlockSpec(shape, lambda i, j: (0,) * len(shape))
    table = pl.BlockSpec((tm, HEAD_DIM), lambda i, j: (i % nt, 0))
    table_t = pl.BlockSpec((HEAD_DIM, tm), lambda i, j: (0, i % nt))
    q_gain = jnp.broadcast_to(q_norm.reshape(HEAD_DIM, 1), (HEAD_DIM, HEAD_DIM))
    conv_width = 4 * _W_BLOCK
    out_shape = (
        jax.ShapeDtypeStruct((batch, N_Q_HEADS, HEAD_DIM, seq), bf16),
        jax.ShapeDtypeStruct((batch, N_KV_HEADS, seq, HEAD_DIM), bf16),
        jax.ShapeDtypeStruct((batch, N_KV_HEADS, HEAD_DIM, seq), bf16),
        jax.ShapeDtypeStruct((rows, ATTN_WIDTH), bf16),
        jax.ShapeDtypeStruct((rows, conv_width), bf16),
        jax.ShapeDtypeStruct((rows, conv_width), bf16),
    )
    out_specs = (
        pl.BlockSpec((1, GQA_GROUP, HEAD_DIM, tm), lambda i, j: (i // nt, jnp.minimum(j, 1), 0, i % nt)),
        pl.BlockSpec((1, N_KV_HEADS, tm, HEAD_DIM), lambda i, j: (i // nt, 0, i % nt, 0)),
        pl.BlockSpec((1, N_KV_HEADS, HEAD_DIM, tm), lambda i, j: (i // nt, 0, 0, i % nt)),
        pl.BlockSpec((tm, 2 * _W_BLOCK), lambda i, j: (i, jnp.minimum(j, 1))),
        pl.BlockSpec((tm, _W_BLOCK), lambda i, j: (i, jnp.maximum(j - 2, 0))),
        pl.BlockSpec((tm, _W_BLOCK), lambda i, j: (i, jnp.maximum(j - 2, 0))),
    )
    return pl.pallas_call(
        _proj_kernel,
        grid=(rows // tm, _N_PROJ_STEPS),
        in_specs=[
            pl.BlockSpec((tm, d_model), lambda i, j: (i, 0)),
            full((1, d_model)),
            w_spec(_wa_idx), w_spec(_wb_idx), w_spec(_wc_idx), w_spec(_wd_idx), w_spec(_we_idx),
            full((HEAD_DIM, HEAD_DIM)), full((1, HEAD_DIM)),
            table, table, table_t, table_t,
        ],
        out_specs=out_specs,
        out_shape=out_shape,
        scratch_shapes=[pltpu.VMEM((tm, d_model), bf16)],
        compiler_params=pltpu.CompilerParams(
            dimension_semantics=("arbitrary", "arbitrary"),
            vmem_limit_bytes=_VMEM_LIMIT),
        name="in_projection",
    )(x2, norm_in, w_bf, w_bf, w_bf, w_bf, w_bf, q_gain, k_norm, cos_t, sin_t, cos_t.T, sin_t.T)


def _attn_kernel(q_ref, k_ref, v_ref, sg_ref, o_ref, acc_ref, s_ref, *, tq, tk, nk, nq):
    cols = GQA_GROUP * tq

    def q_tile(i):
        off = pl.multiple_of(i * tq, tq)
        return jnp.concatenate([q_ref[0, h, :, pl.ds(off, tq)] for h in range(GQA_GROUP)], axis=1)

    def scores(q_mat, j, slot):
        k = k_ref[0, 0, pl.ds(pl.multiple_of(j * tk, tk), tk), :]
        s_t = jnp.dot(k, q_mat, preferred_element_type=f32)
        s_ref[slot] = s_t
        return jnp.max(s_t, axis=0, keepdims=True)

    ones_rows = jnp.ones((_BF16_SUBLANES, tk), bf16)

    def step(j, slot, acc, m_prev, chunk_max, q_ahead, j_ahead):
        ahead_max = scores(q_ahead, j_ahead, 1 - slot)
        off = pl.multiple_of(j * tk, tk)
        v_aug = jnp.concatenate([v_ref[0, 0, :, pl.ds(off, tk)], ones_rows], axis=0)
        m_new = jnp.maximum(m_prev, chunk_max)
        alpha = jnp.exp2(m_prev - m_new)
        p_t = jnp.exp2(s_ref[slot] - m_new).astype(bf16)
        pv_t = jnp.dot(v_aug, p_t, preferred_element_type=f32)
        acc[...] = alpha * acc[...] + pv_t
        return m_new, ahead_max

    def query_tile(i, acc, chunk_max):
        q_t = q_tile(i)
        q_next = q_tile(jnp.minimum(i + 1, nq - 1))
        acc[...] = jnp.zeros(acc.shape, f32)

        m_run = jnp.full((1, cols), -jnp.inf, f32)
        for c in range(nk):
            last = c == nk - 1
            m_run, chunk_max = step(c, c % 2, acc, m_run, chunk_max, q_next if last else q_t, 0 if last else c + 1)
        o_t = acc[:HEAD_DIM, :] / acc[HEAD_DIM:HEAD_DIM + 1, :]
        rows = pl.ds(pl.multiple_of(i * tq, tq), tq)
        for h in range(GQA_GROUP):
            lanes = slice(h * HEAD_DIM, (h + 1) * HEAD_DIM)
            gate = sg_ref[rows, lanes].astype(f32)
            o_ref[rows, lanes] = (o_t[:, h * tq:(h + 1) * tq].T * gate).astype(bf16)
        return chunk_max

    def tile_group(g, chunk_max):
        for t in range(_ATTN_TILES_PER_ITER):
            chunk_max = query_tile(g * _ATTN_TILES_PER_ITER + t, acc_ref.at[t], chunk_max)
        return chunk_max

    lax.fori_loop(0, nq // _ATTN_TILES_PER_ITER, tile_group, scores(q_tile(0), 0, 0))


def _attention(q_t, k, v_t, sg, batch, seq):
    tq, tk = _ATTN_TQ, _ATTN_TK
    cols = GQA_GROUP * tq
    group_width = GQA_GROUP * HEAD_DIM
    gate_spec = pl.BlockSpec((seq, group_width), lambda b, h: (b, h))
    return pl.pallas_call(
        functools.partial(_attn_kernel, tq=tq, tk=tk, nk=seq // tk, nq=seq // tq),
        grid=(batch, N_KV_HEADS),
        in_specs=[
            pl.BlockSpec((1, GQA_GROUP, HEAD_DIM, seq), lambda b, h: (b, h, 0, 0)),
            pl.BlockSpec((1, 1, seq, HEAD_DIM), lambda b, h: (b, h, 0, 0)),
            pl.BlockSpec((1, 1, HEAD_DIM, seq), lambda b, h: (b, h, 0, 0)),
            gate_spec,
        ],
        out_specs=gate_spec,
        out_shape=jax.ShapeDtypeStruct((batch * seq, ATTN_WIDTH), bf16),
        scratch_shapes=[pltpu.VMEM((_ATTN_TILES_PER_ITER, HEAD_DIM + _BF16_SUBLANES, cols), f32),
                        pltpu.VMEM((2, tk, cols), f32)],
        compiler_params=pltpu.CompilerParams(
            dimension_semantics=("arbitrary", "arbitrary"),
            vmem_limit_bytes=_VMEM_LIMIT),
        name="gqa_attention",
    )(q_t, k, v_t, sg)


def _out_kernel(attn_ref, u_ref, uprev_ref, unext_ref, e_ref, x_ref, w_ref, cw_ref, cb_ref, nf_ref,
                o_ref, *, tm, seq):
    i = pl.program_id(0)
    t0 = (i * tm) % seq
    u = u_ref[...].astype(f32)
    prev_row = uprev_ref[...].astype(f32)[_BF16_SUBLANES - 1:_BF16_SUBLANES, :]
    next_row = unext_ref[...].astype(f32)[0:1, :]
    prev_row = jnp.where(t0 == 0, 0.0, prev_row)
    next_row = jnp.where(t0 + tm == seq, 0.0, next_row)
    row = lax.broadcasted_iota(jnp.int32, u.shape, 0)
    u_m1 = jnp.where(row == 0, prev_row, pltpu.roll(u, 1, 0))
    u_p1 = jnp.where(row == tm - 1, next_row, pltpu.roll(u, tm - 1, 0))
    cw = cw_ref[...]
    y = cb_ref[...] + u_m1 * cw[0:1, :]
    y = y + u * cw[1:2, :]
    y = y + u_p1 * cw[2:3, :]
    conv = (e_ref[...].astype(f32) * y).astype(bf16)
    acc = jnp.dot(attn_ref[...], w_ref[:ATTN_WIDTH, :], preferred_element_type=f32)
    acc = acc + jnp.dot(conv, w_ref[ATTN_WIDTH:, :], preferred_element_type=f32)
    h = x_ref[...] + acc
    ms = jnp.mean(h * h, axis=-1, keepdims=True)
    o_ref[...] = h * lax.rsqrt(ms + EPS) * nf_ref[...]


def _out_projection(attn, u, e, x2, w_bf, conv_w, conv_b, norm_final, seq):
    rows, d_model = x2.shape
    tm = _OUT_ROWS
    conv_width = u.shape[1]
    halo_blocks = rows // _BF16_SUBLANES
    per_tile = tm // _BF16_SUBLANES
    row_tile = lambda width: pl.BlockSpec((tm, width), lambda i: (i, 0))
    full = lambda shape: pl.BlockSpec(shape, lambda i: (0,) * len(shape))
    return pl.pallas_call(
        functools.partial(_out_kernel, tm=tm, seq=seq),
        grid=(rows // tm,),
        in_specs=[
            row_tile(ATTN_WIDTH),
            row_tile(conv_width),
            pl.BlockSpec((_BF16_SUBLANES, conv_width), lambda i: (jnp.maximum(i * per_tile - 1, 0), 0)),
            pl.BlockSpec((_BF16_SUBLANES, conv_width),
                         lambda i: (jnp.minimum((i + 1) * per_tile, halo_blocks - 1), 0)),
            row_tile(conv_width),
            row_tile(d_model),
            full(w_bf.shape), full(conv_w.shape), full(conv_b.shape), full(norm_final.shape),
        ],
        out_specs=row_tile(d_model),
        out_shape=jax.ShapeDtypeStruct((rows, d_model), f32),
        compiler_params=pltpu.CompilerParams(
            dimension_semantics=("arbitrary",),
            vmem_limit_bytes=_VMEM_LIMIT),
        name="out_projection",
    )(attn, u, u, u, e, x2, w_bf, conv_w, conv_b, norm_final)


def kernel(x, norm_in, w_in, q_norm, k_norm, conv_w, conv_b, w_out, norm_final):
    batch, seq, d_model = x.shape
    assert norm_in.shape[0] == 1, "single-layer block"
    assert seq % _PROJ_ROWS == 0 and seq % _OUT_ROWS == 0 and seq % _ATTN_TK == 0 and seq % _ATTN_TQ == 0
    x2 = x.reshape(batch * seq, d_model)
    cos_t, sin_t = _rope_tables(seq)
    q, k, v, sg, u, e = _in_projection(
        x2, norm_in, w_in[0].astype(bf16), q_norm, k_norm, cos_t, sin_t, batch, seq)
    attn = _attention(q, k, v, sg, batch, seq)
    out = _out_projection(attn, u, e, x2, w_out[0].astype(bf16), conv_w[0], conv_b,
                          norm_final.reshape(1, d_model), seq)
    return out.reshape(batch, seq, d_model)
```

```python
import functools

import jax
import jax.numpy as jnp
from jax import lax
from jax.experimental import pallas as pl
from jax.experimental.pallas import tpu as pltpu

HEAD_DIM = 128
N_Q_HEADS = 8
N_KV_HEADS = 2
GQA_GROUP = N_Q_HEADS // N_KV_HEADS
ATTN_WIDTH = N_Q_HEADS * HEAD_DIM
CONV_K = 3
GRID_W = 64
ROPE_THETA = 10000.0
ROPE_AXIS_DIM = HEAD_DIM // 2
EPS = 1e-6
_LOG2_E = 1.4426950408889634

_V7X_VMEM_BYTES = 64 * 2**20
_VMEM_LIMIT = _V7X_VMEM_BYTES - 8 * 2**20
_BF16_SUBLANES = 16
_W_BLOCK = 256

_PROJ_ROWS = 512
_ATTN_TQ = 128
_ATTN_TK = 1024
_ATTN_TILES_PER_ITER = 4
_OUT_ROWS = 512

f32 = jnp.float32
bf16 = jnp.bfloat16


def _silu(x):
    return x / (1.0 + jnp.exp(-x))


def _head_norm(x, g):
    ms = jnp.mean(x * x, axis=-1, keepdims=True)
    return x * lax.rsqrt(ms + EPS) * g


def _rope(x, c, s):
    lane = lax.broadcasted_iota(jnp.int32, x.shape, 1)
    first_half = (lane & (ROPE_AXIS_DIM // 2)) == 0
    swapped = jnp.where(first_half,
                        pltpu.roll(x, HEAD_DIM - ROPE_AXIS_DIM // 2, 1),
                        pltpu.roll(x, ROPE_AXIS_DIM // 2, 1))
    return x * c + swapped * s


def _rope_tables(seq_len):
    rows = seq_len // GRID_W
    row = jnp.repeat(jnp.arange(rows, dtype=f32), GRID_W)
    col = jnp.tile(jnp.arange(GRID_W, dtype=f32), rows)
    inv_freq = ROPE_THETA ** (-jnp.arange(0, ROPE_AXIS_DIM, 2, dtype=f32) / ROPE_AXIS_DIM)
    ang_r = row[:, None] * inv_freq[None, :]
    ang_c = col[:, None] * inv_freq[None, :]
    cr, sr, cc, sc = jnp.cos(ang_r), jnp.sin(ang_r), jnp.cos(ang_c), jnp.sin(ang_c)
    cos_t = jnp.concatenate([cr, cr, cc, cc], axis=-1)
    sin_t = jnp.concatenate([-sr, sr, -sc, sc], axis=-1)
    return cos_t, sin_t


_Q_BLOCK0, _K_BLOCK, _V_BLOCK, _G_BLOCK0 = 0, 4, 5, 6
_CB_BLOCK0, _CC_BLOCK0, _CX_BLOCK0, _GC_BLOCK0 = 10, 14, 18, 22
_N_CONV_BLOCKS = 4


def _rope_t(x_t, cos_tt, sin_tt):
    q = ROPE_AXIS_DIM // 2
    swapped = jnp.concatenate([x_t[q:2 * q], x_t[0:q], x_t[3 * q:4 * q], x_t[2 * q:3 * q]], axis=0)
    return x_t * cos_tt + swapped * sin_tt


def _proj_kernel(x_ref, nin_ref, w_ref, qg_ref, kn_ref, cos_ref, sin_ref, cost_ref, sint_ref,
                 q_ref, k_ref, v_ref, sg_ref, u_ref, e_ref, hn_ref):
    tm = hn_ref.shape[0]
    x = x_ref[...]
    ms = jnp.mean(x * x, axis=-1, keepdims=True)
    hn_ref[...] = (x * lax.rsqrt(ms + EPS) * nin_ref[...]).astype(bf16)

    def proj(block):
        w = w_ref[:, block * _W_BLOCK:(block + 1) * _W_BLOCK]
        return jnp.dot(hn_ref[...], w, preferred_element_type=f32)

    def head(mat, hh):
        return mat[:, hh * HEAD_DIM:(hh + 1) * HEAD_DIM]

    kk = proj(_K_BLOCK)
    for hh in range(N_KV_HEADS):
        k_ref[0, hh] = _rope(_head_norm(head(kk, hh), kn_ref[...]), cos_ref[...], sin_ref[...]).astype(bf16)

    cos_tt, sin_tt = cost_ref[...], sint_ref[...]
    gain = jnp.tile(qg_ref[...], (1, tm // HEAD_DIM))
    scale = HEAD_DIM ** -0.5 * _LOG2_E
    for pair in range(N_Q_HEADS // 2):
        qq = proj(_Q_BLOCK0 + pair)
        for hh in range(2):
            q_t = head(qq, hh).T
            ms = jnp.mean(q_t * q_t, axis=0, keepdims=True)
            r = lax.rsqrt(ms + EPS) * scale
            q_ref[0, 2 * pair + hh] = (_rope_t(q_t * gain, cos_tt, sin_tt) * r).astype(bf16)

    vv = proj(_V_BLOCK)
    for hh in range(N_KV_HEADS):
        v_ref[0, hh] = head(vv, hh).T.astype(bf16)

    for c in range(ATTN_WIDTH // _W_BLOCK):
        sg_ref[:, c * _W_BLOCK:(c + 1) * _W_BLOCK] = _silu(proj(_G_BLOCK0 + c)).astype(bf16)

    for c in range(_N_CONV_BLOCKS):
        cols = slice(c * _W_BLOCK, (c + 1) * _W_BLOCK)
        u_ref[:, cols] = (proj(_CC_BLOCK0 + c) * proj(_CX_BLOCK0 + c)).astype(bf16)
        e_ref[:, cols] = (proj(_CB_BLOCK0 + c) * _silu(proj(_GC_BLOCK0 + c))).astype(bf16)


def _in_projection(x2, norm_in, w_bf, q_norm, k_norm, cos_t, sin_t, batch, seq):
    rows, d_model = x2.shape
    tm = _PROJ_ROWS
    nt = seq // tm
    full = lambda shape: pl.BlockSpec(shape, lambda i: (0,) * len(shape))
    table = pl.BlockSpec((tm, HEAD_DIM), lambda i: (i % nt, 0))
    table_t = pl.BlockSpec((HEAD_DIM, tm), lambda i: (0, i % nt))
    row_tile = lambda width: pl.BlockSpec((tm, width), lambda i: (i, 0))
    q_gain = jnp.broadcast_to(q_norm.reshape(HEAD_DIM, 1), (HEAD_DIM, HEAD_DIM))
    conv_width = _N_CONV_BLOCKS * _W_BLOCK
    out_shape = (
        jax.ShapeDtypeStruct((batch, N_Q_HEADS, HEAD_DIM, seq), bf16),
        jax.ShapeDtypeStruct((batch, N_KV_HEADS, seq, HEAD_DIM), bf16),
        jax.ShapeDtypeStruct((batch, N_KV_HEADS, HEAD_DIM, seq), bf16),
        jax.ShapeDtypeStruct((rows, ATTN_WIDTH), bf16),
        jax.ShapeDtypeStruct((rows, conv_width), bf16),
        jax.ShapeDtypeStruct((rows, conv_width), bf16),
    )
    out_specs = (
        pl.BlockSpec((1, N_Q_HEADS, HEAD_DIM, tm), lambda i: (i // nt, 0, 0, i % nt)),
        pl.BlockSpec((1, N_KV_HEADS, tm, HEAD_DIM), lambda i: (i // nt, 0, i % nt, 0)),
        pl.BlockSpec((1, N_KV_HEADS, HEAD_DIM, tm), lambda i: (i // nt, 0, 0, i % nt)),
        row_tile(ATTN_WIDTH), row_tile(conv_width), row_tile(conv_width),
    )
    return pl.pallas_call(
        _proj_kernel,
        grid=(rows // tm,),
        in_specs=[
            row_tile(d_model),
            full((1, d_model)),
            pl.BlockSpec(w_bf.shape, lambda i: (0, 0), pipeline_mode=pl.Buffered(1)),
            full((HEAD_DIM, HEAD_DIM)), full((1, HEAD_DIM)),
            table, table, table_t, table_t,
        ],
        out_specs=out_specs,
        out_shape=out_shape,
        scratch_shapes=[pltpu.VMEM((tm, d_model), bf16)],
        compiler_params=pltpu.CompilerParams(
            dimension_semantics=("arbitrary",),
            vmem_limit_bytes=_VMEM_LIMIT),
        name="in_projection",
    )(x2, norm_in, w_bf, q_gain, k_norm, cos_t, sin_t, cos_t.T, sin_t.T)


def _attn_kernel(q_ref, k_ref, v_ref, sg_ref, o_ref, acc_ref, s_ref, *, tq, tk, nk, nq):
    cols = GQA_GROUP * tq

    def q_tile(i):
        off = pl.multiple_of(i * tq, tq)
        return jnp.concatenate([q_ref[0, h, :, pl.ds(off, tq)] for h in range(GQA_GROUP)], axis=1)

    def scores(q_mat, j, slot):
        k = k_ref[0, 0, pl.ds(pl.multiple_of(j * tk, tk), tk), :]
        s_t = jnp.dot(k, q_mat, preferred_element_type=f32)
        s_ref[slot] = s_t
        return jnp.max(s_t, axis=0, keepdims=True)

    ones_rows = jnp.ones((_BF16_SUBLANES, tk), bf16)

    def step(j, slot, acc, m_prev, chunk_max, q_ahead, j_ahead):
        ahead_max = scores(q_ahead, j_ahead, 1 - slot)
        off = pl.multiple_of(j * tk, tk)
        v_aug = jnp.concatenate([v_ref[0, 0, :, pl.ds(off, tk)], ones_rows], axis=0)
        m_new = jnp.maximum(m_prev, chunk_max)
        alpha = jnp.exp2(m_prev - m_new)
        p_t = jnp.exp2(s_ref[slot] - m_new).astype(bf16)
        pv_t = jnp.dot(v_aug, p_t, preferred_element_type=f32)
        acc[...] = alpha * acc[...] + pv_t
        return m_new, ahead_max

    def query_tile(i, acc, chunk_max):
        q_t = q_tile(i)
        q_next = q_tile(jnp.minimum(i + 1, nq - 1))
        acc[...] = jnp.zeros(acc.shape, f32)

        m_run = jnp.full((1, cols), -jnp.inf, f32)
        for c in range(nk):
            last = c == nk - 1
            m_run, chunk_max = step(c, c % 2, acc, m_run, chunk_max, q_next if last else q_t, 0 if last else c + 1)
        o_t = acc[:HEAD_DIM, :] / acc[HEAD_DIM:HEAD_DIM + 1, :]
        rows = pl.ds(pl.multiple_of(i * tq, tq), tq)
        for h in range(GQA_GROUP):
            lanes = slice(h * HEAD_DIM, (h + 1) * HEAD_DIM)
            gate = sg_ref[rows, lanes].astype(f32)
            o_ref[rows, lanes] = (o_t[:, h * tq:(h + 1) * tq].T * gate).astype(bf16)
        return chunk_max

    def tile_group(g, chunk_max):
        for t in range(_ATTN_TILES_PER_ITER):
            chunk_max = query_tile(g * _ATTN_TILES_PER_ITER + t, acc_ref.at[t], chunk_max)
        return chunk_max

    lax.fori_loop(0, nq // _ATTN_TILES_PER_ITER, tile_group, scores(q_tile(0), 0, 0))


def _attention(q_t, k, v_t, sg, batch, seq):
    tq, tk = _ATTN_TQ, _ATTN_TK
    cols = GQA_GROUP * tq
    group_width = GQA_GROUP * HEAD_DIM
    gate_spec = pl.BlockSpec((seq, group_width), lambda b, h: (b, h))
    return pl.pallas_call(
        functools.partial(_attn_kernel, tq=tq, tk=tk, nk=seq // tk, nq=seq // tq),
        grid=(batch, N_KV_HEADS),
        in_specs=[
            pl.BlockSpec((1, GQA_GROUP, HEAD_DIM, seq), lambda b, h: (b, h, 0, 0)),
            pl.BlockSpec((1, 1, seq, HEAD_DIM), lambda b, h: (b, h, 0, 0)),
            pl.BlockSpec((1, 1, HEAD_DIM, seq), lambda b, h: (b, h, 0, 0)),
            gate_spec,
        ],
        out_specs=gate_spec,
        out_shape=jax.ShapeDtypeStruct((batch * seq, ATTN_WIDTH), bf16),
        scratch_shapes=[pltpu.VMEM((_ATTN_TILES_PER_ITER, HEAD_DIM + _BF16_SUBLANES, cols), f32),
                        pltpu.VMEM((2, tk, cols), f32)],
        compiler_params=pltpu.CompilerParams(
            dimension_semantics=("arbitrary", "arbitrary"),
            vmem_limit_bytes=_VMEM_LIMIT),
        name="gqa_attention",
    )(q_t, k, v_t, sg)


def _out_kernel(attn_ref, u_ref, uprev_ref, unext_ref, e_ref, x_ref, w_ref, cw_ref, cb_ref, nf_ref,
                o_ref, *, tm, seq):
    i = pl.program_id(0)
    t0 = (i * tm) % seq
    u = u_ref[...].astype(f32)
    prev_row = uprev_ref[...].astype(f32)[_BF16_SUBLANES - 1:_BF16_SUBLANES, :]
    next_row = unext_ref[...].astype(f32)[0:1, :]
    prev_row = jnp.where(t0 == 0, 0.0, prev_row)
    next_row = jnp.where(t0 + tm == seq, 0.0, next_row)
    row = lax.broadcasted_iota(jnp.int32, u.shape, 0)
    u_m1 = jnp.where(row == 0, prev_row, pltpu.roll(u, 1, 0))
    u_p1 = jnp.where(row == tm - 1, next_row, pltpu.roll(u, tm - 1, 0))
    cw = cw_ref[...]
    y = cb_ref[...] + u_m1 * cw[0:1, :]
    y = y + u * cw[1:2, :]
    y = y + u_p1 * cw[2:3, :]
    conv = (e_ref[...].astype(f32) * y).astype(bf16)
    acc = jnp.dot(attn_ref[...], w_ref[:ATTN_WIDTH, :], preferred_element_type=f32)
    acc = acc + jnp.dot(conv, w_ref[ATTN_WIDTH:, :], preferred_element_type=f32)
    h = x_ref[...] + acc
    ms = jnp.mean(h * h, axis=-1, keepdims=True)
    o_ref[...] = h * lax.rsqrt(ms + EPS) * nf_ref[...]


def _out_projection(attn, u, e, x2, w_bf, conv_w, conv_b, norm_final, seq):
    rows, d_model = x2.shape
    tm = _OUT_ROWS
    conv_width = u.shape[1]
    halo_blocks = rows // _BF16_SUBLANES
    per_tile = tm // _BF16_SUBLANES
    row_tile = lambda width: pl.BlockSpec((tm, width), lambda i: (i, 0))
    full = lambda shape: pl.BlockSpec(shape, lambda i: (0,) * len(shape))
    return pl.pallas_call(
        functools.partial(_out_kernel, tm=tm, seq=seq),
        grid=(rows // tm,),
        in_specs=[
            row_tile(ATTN_WIDTH),
            row_tile(conv_width),
            pl.BlockSpec((_BF16_SUBLANES, conv_width), lambda i: (jnp.maximum(i * per_tile - 1, 0), 0)),
            pl.BlockSpec((_BF16_SUBLANES, conv_width),
                         lambda i: (jnp.minimum((i + 1) * per_tile, halo_blocks - 1), 0)),
            row_tile(conv_width),
            row_tile(d_model),
            full(w_bf.shape), full(conv_w.shape), full(conv_b.shape), full(norm_final.shape),
        ],
        out_specs=row_tile(d_model),
        out_shape=jax.ShapeDtypeStruct((rows, d_model), f32),
        compiler_params=pltpu.CompilerParams(
            dimension_semantics=("arbitrary",),
            vmem_limit_bytes=_VMEM_LIMIT),
        name="out_projection",
    )(attn, u, u, u, e, x2, w_bf, conv_w, conv_b, norm_final)


def kernel(x, norm_in, w_in, q_norm, k_norm, conv_w, conv_b, w_out, norm_final):
    batch, seq, d_model = x.shape
    assert norm_in.shape[0] == 1, "single-layer block"
    assert seq % _PROJ_ROWS == 0 and seq % _OUT_ROWS == 0 and seq % _ATTN_TK == 0 and seq % _ATTN_TQ == 0
    x2 = x.reshape(batch * seq, d_model)
    cos_t, sin_t = _rope_tables(seq)
    q, k, v, sg, u, e = _in_projection(
        x2, norm_in, w_in[0].astype(bf16), q_norm, k_norm, cos_t, sin_t, batch, seq)
    attn = _attention(q, k, v, sg, batch, seq)
    out = _out_projection(attn, u, e, x2, w_out[0].astype(bf16), conv_w[0], conv_b,
                          norm_final.reshape(1, d_model), seq)
    return out.reshape(batch, seq, d_model)
```

```python
import functools

import jax
import jax.numpy as jnp
from jax import lax
from jax.experimental import pallas as pl
from jax.experimental.pallas import tpu as pltpu

HEAD_DIM = 128
N_Q_HEADS = 8
N_KV_HEADS = 2
GQA_GROUP = N_Q_HEADS // N_KV_HEADS
ATTN_WIDTH = N_Q_HEADS * HEAD_DIM
CONV_K = 3
GRID_W = 64
ROPE_THETA = 10000.0
ROPE_AXIS_DIM = HEAD_DIM // 2
EPS = 1e-6
_LOG2_E = 1.4426950408889634

_V7X_VMEM_BYTES = 64 * 2**20
_VMEM_LIMIT = _V7X_VMEM_BYTES - 8 * 2**20
_BF16_SUBLANES = 16
_W_BLOCK = 256
_W_STAGE_ROWS = 64

_PROJ_ROWS = 512
_ATTN_TQ = 128
_ATTN_TK = 1024
_ATTN_TILES_PER_ITER = 4
_OUT_ROWS = 512

f32 = jnp.float32
bf16 = jnp.bfloat16


def _silu(x):
    return x / (1.0 + jnp.exp(-x))


def _head_norm(x, g):
    ms = jnp.mean(x * x, axis=-1, keepdims=True)
    return x * lax.rsqrt(ms + EPS) * g


def _rope(x, c, s):
    lane = lax.broadcasted_iota(jnp.int32, x.shape, 1)
    first_half = (lane & (ROPE_AXIS_DIM // 2)) == 0
    swapped = jnp.where(first_half,
                        pltpu.roll(x, HEAD_DIM - ROPE_AXIS_DIM // 2, 1),
                        pltpu.roll(x, ROPE_AXIS_DIM // 2, 1))
    return x * c + swapped * s


def _rope_tables(seq_len):
    rows = seq_len // GRID_W
    row = jnp.repeat(jnp.arange(rows, dtype=f32), GRID_W)
    col = jnp.tile(jnp.arange(GRID_W, dtype=f32), rows)
    inv_freq = ROPE_THETA ** (-jnp.arange(0, ROPE_AXIS_DIM, 2, dtype=f32) / ROPE_AXIS_DIM)
    ang_r = row[:, None] * inv_freq[None, :]
    ang_c = col[:, None] * inv_freq[None, :]
    cr, sr, cc, sc = jnp.cos(ang_r), jnp.sin(ang_r), jnp.cos(ang_c), jnp.sin(ang_c)
    cos_t = jnp.concatenate([cr, cr, cc, cc], axis=-1)
    sin_t = jnp.concatenate([-sr, sr, -sc, sc], axis=-1)
    return cos_t, sin_t


_Q_BLOCK0, _K_BLOCK, _V_BLOCK, _G_BLOCK0 = 0, 4, 5, 6
_CB_BLOCK0, _CC_BLOCK0, _CX_BLOCK0, _GC_BLOCK0 = 10, 14, 18, 22
_N_CONV_BLOCKS = 4


def _rope_t(x_t, cos_tt, sin_tt):
    q = ROPE_AXIS_DIM // 2
    swapped = jnp.concatenate([x_t[q:2 * q], x_t[0:q], x_t[3 * q:4 * q], x_t[2 * q:3 * q]], axis=0)
    return x_t * cos_tt + swapped * sin_tt


def _load_weights_as_bf16(w_hbm, w_ref, stage_ref, sem):
    chunk = stage_ref.shape[1]
    n_chunks = w_ref.shape[0] // chunk

    def copy(c):
        return pltpu.make_async_copy(w_hbm.at[pl.ds(c * chunk, chunk), :], stage_ref.at[c % 2], sem.at[c % 2])

    copy(0).start()
    for c in range(n_chunks):
        if c + 1 < n_chunks:
            copy(c + 1).start()
        copy(c).wait()
        w_ref[c * chunk:(c + 1) * chunk, :] = stage_ref[c % 2].astype(bf16)


def _proj_kernel(x_ref, nin_ref, w_hbm, qg_ref, kn_ref, cos_ref, sin_ref, cost_ref, sint_ref,
                 q_ref, k_ref, v_ref, sg_ref, u_ref, e_ref, hn_ref, w_ref, stage_ref, sem):
    tm = hn_ref.shape[0]

    @pl.when(pl.program_id(0) == 0)
    def _():
        _load_weights_as_bf16(w_hbm, w_ref, stage_ref, sem)

    x = x_ref[...]
    ms = jnp.mean(x * x, axis=-1, keepdims=True)
    hn_ref[...] = (x * lax.rsqrt(ms + EPS) * nin_ref[...]).astype(bf16)

    def proj(block):
        w = w_ref[:, block * _W_BLOCK:(block + 1) * _W_BLOCK]
        return jnp.dot(hn_ref[...], w, preferred_element_type=f32)

    def head(mat, hh):
        return mat[:, hh * HEAD_DIM:(hh + 1) * HEAD_DIM]

    kk = proj(_K_BLOCK)
    for hh in range(N_KV_HEADS):
        k_ref[0, hh] = _rope(_head_norm(head(kk, hh), kn_ref[...]), cos_ref[...], sin_ref[...]).astype(bf16)

    cos_tt, sin_tt = cost_ref[...], sint_ref[...]
    gain = jnp.tile(qg_ref[...], (1, tm // HEAD_DIM))
    scale = HEAD_DIM ** -0.5 * _LOG2_E
    for pair in range(N_Q_HEADS // 2):
        qq = proj(_Q_BLOCK0 + pair)
        for hh in range(2):
            q_t = head(qq, hh).T
            ms = jnp.mean(q_t * q_t, axis=0, keepdims=True)
            r = lax.rsqrt(ms + EPS) * scale
            q_ref[0, 2 * pair + hh] = (_rope_t(q_t * gain, cos_tt, sin_tt) * r).astype(bf16)

    vv = proj(_V_BLOCK)
    for hh in range(N_KV_HEADS):
        v_ref[0, hh] = head(vv, hh).T.astype(bf16)

    for c in range(ATTN_WIDTH // _W_BLOCK):
        sg_ref[:, c * _W_BLOCK:(c + 1) * _W_BLOCK] = _silu(proj(_G_BLOCK0 + c)).astype(bf16)

    for c in range(_N_CONV_BLOCKS):
        cols = slice(c * _W_BLOCK, (c + 1) * _W_BLOCK)
        u_ref[:, cols] = (proj(_CC_BLOCK0 + c) * proj(_CX_BLOCK0 + c)).astype(bf16)
        e_ref[:, cols] = (proj(_CB_BLOCK0 + c) * _silu(proj(_GC_BLOCK0 + c))).astype(bf16)


def _in_projection(x2, norm_in, w, q_norm, k_norm, cos_t, sin_t, batch, seq):
    rows, d_model = x2.shape
    tm = _PROJ_ROWS
    nt = seq // tm
    full = lambda shape: pl.BlockSpec(shape, lambda i: (0,) * len(shape))
    table = pl.BlockSpec((tm, HEAD_DIM), lambda i: (i % nt, 0))
    table_t = pl.BlockSpec((HEAD_DIM, tm), lambda i: (0, i % nt))
    row_tile = lambda width: pl.BlockSpec((tm, width), lambda i: (i, 0))
    q_gain = jnp.broadcast_to(q_norm.reshape(HEAD_DIM, 1), (HEAD_DIM, HEAD_DIM))
    conv_width = _N_CONV_BLOCKS * _W_BLOCK
    out_shape = (
        jax.ShapeDtypeStruct((batch, N_Q_HEADS, HEAD_DIM, seq), bf16),
        jax.ShapeDtypeStruct((batch, N_KV_HEADS, seq, HEAD_DIM), bf16),
        jax.ShapeDtypeStruct((batch, N_KV_HEADS, HEAD_DIM, seq), bf16),
        jax.ShapeDtypeStruct((rows, ATTN_WIDTH), bf16),
        jax.ShapeDtypeStruct((rows, conv_width), bf16),
        jax.ShapeDtypeStruct((rows, conv_width), bf16),
    )
    out_specs = (
        pl.BlockSpec((1, N_Q_HEADS, HEAD_DIM, tm), lambda i: (i // nt, 0, 0, i % nt)),
        pl.BlockSpec((1, N_KV_HEADS, tm, HEAD_DIM), lambda i: (i // nt, 0, i % nt, 0)),
        pl.BlockSpec((1, N_KV_HEADS, HEAD_DIM, tm), lambda i: (i // nt, 0, 0, i % nt)),
        row_tile(ATTN_WIDTH), row_tile(conv_width), row_tile(conv_width),
    )
    return pl.pallas_call(
        _proj_kernel,
        grid=(rows // tm,),
        in_specs=[
            row_tile(d_model),
            full((1, d_model)),
            pl.BlockSpec(memory_space=pl.ANY),
            full((HEAD_DIM, HEAD_DIM)), full((1, HEAD_DIM)),
            table, table, table_t, table_t,
        ],
        out_specs=out_specs,
        out_shape=out_shape,
        scratch_shapes=[pltpu.VMEM((tm, d_model), bf16),
                        pltpu.VMEM(w.shape, bf16),
                        pltpu.VMEM((2, _W_STAGE_ROWS, w.shape[1]), f32),
                        pltpu.SemaphoreType.DMA((2,))],
        compiler_params=pltpu.CompilerParams(
            dimension_semantics=("arbitrary",),
            vmem_limit_bytes=_VMEM_LIMIT),
        name="in_projection",
    )(x2, norm_in, w, q_gain, k_norm, cos_t, sin_t, cos_t.T, sin_t.T)


def _attn_kernel(q_ref, k_ref, v_ref, sg_ref, o_ref, acc_ref, s_ref, *, tq, tk, nk, nq):
    cols = GQA_GROUP * tq

    def q_tile(i):
        off = pl.multiple_of(i * tq, tq)
        return jnp.concatenate([q_ref[0, h, :, pl.ds(off, tq)] for h in range(GQA_GROUP)], axis=1)

    def scores(q_mat, j, slot):
        k = k_ref[0, 0, pl.ds(pl.multiple_of(j * tk, tk), tk), :]
        s_t = jnp.dot(k, q_mat, preferred_element_type=f32)
        s_ref[slot] = s_t
        return jnp.max(s_t, axis=0, keepdims=True)

    ones_rows = jnp.ones((_BF16_SUBLANES, tk), bf16)

    def step(j, slot, acc, m_prev, chunk_max, q_ahead, j_ahead):
        ahead_max = scores(q_ahead, j_ahead, 1 - slot)
        off = pl.multiple_of(j * tk, tk)
        v_aug = jnp.concatenate([v_ref[0, 0, :, pl.ds(off, tk)], ones_rows], axis=0)
        m_new = jnp.maximum(m_prev, chunk_max)
        alpha = jnp.exp2(m_prev - m_new)
        p_t = jnp.exp2(s_ref[slot] - m_new).astype(bf16)
        pv_t = jnp.dot(v_aug, p_t, preferred_element_type=f32)
        acc[...] = alpha * acc[...] + pv_t
        return m_new, ahead_max

    def query_tile(i, acc, chunk_max):
        q_t = q_tile(i)
        q_next = q_tile(jnp.minimum(i + 1, nq - 1))
        acc[...] = jnp.zeros(acc.shape, f32)

        m_run = jnp.full((1, cols), -jnp.inf, f32)
        for c in range(nk):
            last = c == nk - 1
            m_run, chunk_max = step(c, c % 2, acc, m_run, chunk_max, q_next if last else q_t, 0 if last else c + 1)
        o_t = acc[:HEAD_DIM, :] / acc[HEAD_DIM:HEAD_DIM + 1, :]
        rows = pl.ds(pl.multiple_of(i * tq, tq), tq)
        for h in range(GQA_GROUP):
            lanes = slice(h * HEAD_DIM, (h + 1) * HEAD_DIM)
            gate = sg_ref[rows, lanes].astype(f32)
            o_ref[rows, lanes] = (o_t[:, h * tq:(h + 1) * tq].T * gate).astype(bf16)
        return chunk_max

    def tile_group(g, chunk_max):
        for t in range(_ATTN_TILES_PER_ITER):
            chunk_max = query_tile(g * _ATTN_TILES_PER_ITER + t, acc_ref.at[t], chunk_max)
        return chunk_max

    lax.fori_loop(0, nq // _ATTN_TILES_PER_ITER, tile_group, scores(q_tile(0), 0, 0))


def _attention(q_t, k, v_t, sg, batch, seq):
    tq, tk = _ATTN_TQ, _ATTN_TK
    cols = GQA_GROUP * tq
    group_width = GQA_GROUP * HEAD_DIM
    gate_spec = pl.BlockSpec((seq, group_width), lambda b, h: (b, h))
    return pl.pallas_call(
        functools.partial(_attn_kernel, tq=tq, tk=tk, nk=seq // tk, nq=seq // tq),
        grid=(batch, N_KV_HEADS),
        in_specs=[
            pl.BlockSpec((1, GQA_GROUP, HEAD_DIM, seq), lambda b, h: (b, h, 0, 0)),
            pl.BlockSpec((1, 1, seq, HEAD_DIM), lambda b, h: (b, h, 0, 0)),
            pl.BlockSpec((1, 1, HEAD_DIM, seq), lambda b, h: (b, h, 0, 0)),
            gate_spec,
        ],
        out_specs=gate_spec,
        out_shape=jax.ShapeDtypeStruct((batch * seq, ATTN_WIDTH), bf16),
        scratch_shapes=[pltpu.VMEM((_ATTN_TILES_PER_ITER, HEAD_DIM + _BF16_SUBLANES, cols), f32),
                        pltpu.VMEM((2, tk, cols), f32)],
        compiler_params=pltpu.CompilerParams(
            dimension_semantics=("arbitrary", "arbitrary"),
            vmem_limit_bytes=_VMEM_LIMIT),
        name="gqa_attention",
    )(q_t, k, v_t, sg)


def _out_kernel(attn_ref, u_ref, uprev_ref, unext_ref, e_ref, x_ref, w_ref, cw_ref, cb_ref, nf_ref,
                o_ref, *, tm, seq):
    i = pl.program_id(0)
    t0 = (i * tm) % seq
    u = u_ref[...].astype(f32)
    prev_row = uprev_ref[...].astype(f32)[_BF16_SUBLANES - 1:_BF16_SUBLANES, :]
    next_row = unext_ref[...].astype(f32)[0:1, :]
    prev_row = jnp.where(t0 == 0, 0.0, prev_row)
    next_row = jnp.where(t0 + tm == seq, 0.0, next_row)
    row = lax.broadcasted_iota(jnp.int32, u.shape, 0)
    u_m1 = jnp.where(row == 0, prev_row, pltpu.roll(u, 1, 0))
    u_p1 = jnp.where(row == tm - 1, next_row, pltpu.roll(u, tm - 1, 0))
    cw = cw_ref[...]
    y = cb_ref[...] + u_m1 * cw[0:1, :]
    y = y + u * cw[1:2, :]
    y = y + u_p1 * cw[2:3, :]
    conv = (e_ref[...].astype(f32) * y).astype(bf16)
    acc = jnp.dot(attn_ref[...], w_ref[:ATTN_WIDTH, :], preferred_element_type=f32)
    acc = acc + jnp.dot(conv, w_ref[ATTN_WIDTH:, :], preferred_element_type=f32)
    h = x_ref[...] + acc
    ms = jnp.mean(h * h, axis=-1, keepdims=True)
    o_ref[...] = h * lax.rsqrt(ms + EPS) * nf_ref[...]


def _out_projection(attn, u, e, x2, w_bf, conv_w, conv_b, norm_final, seq):
    rows, d_model = x2.shape
    tm = _OUT_ROWS
    conv_width = u.shape[1]
    halo_blocks = rows // _BF16_SUBLANES
    per_tile = tm // _BF16_SUBLANES
    row_tile = lambda width: pl.BlockSpec((tm, width), lambda i: (i, 0))
    full = lambda shape: pl.BlockSpec(shape, lambda i: (0,) * len(shape))
    return pl.pallas_call(
        functools.partial(_out_kernel, tm=tm, seq=seq),
        grid=(rows // tm,),
        in_specs=[
            row_tile(ATTN_WIDTH),
            row_tile(conv_width),
            pl.BlockSpec((_BF16_SUBLANES, conv_width), lambda i: (jnp.maximum(i * per_tile - 1, 0), 0)),
            pl.BlockSpec((_BF16_SUBLANES, conv_width),
                         lambda i: (jnp.minimum((i + 1) * per_tile, halo_blocks - 1), 0)),
            row_tile(conv_width),
            row_tile(d_model),
            full(w_bf.shape), full(conv_w.shape), full(conv_b.shape), full(norm_final.shape),
        ],
        out_specs=row_tile(d_model),
        out_shape=jax.ShapeDtypeStruct((rows, d_model), f32),
        compiler_params=pltpu.CompilerParams(
            dimension_semantics=("arbitrary",),
            vmem_limit_bytes=_VMEM_LIMIT),
        name="out_projection",
    )(attn, u, u, u, e, x2, w_bf, conv_w, conv_b, norm_final)


def kernel(x, norm_in, w_in, q_norm, k_norm, conv_w, conv_b, w_out, norm_final):
    batch, seq, d_model = x.shape
    assert norm_in.shape[0] == 1, "single-layer block"
    assert seq % _PROJ_ROWS == 0 and seq % _OUT_ROWS == 0 and seq % _ATTN_TK == 0 and seq % _ATTN_TQ == 0
    x2 = x.reshape(batch * seq, d_model)
    cos_t, sin_t = _rope_tables(seq)
    q, k, v, sg, u, e = _in_projection(
        x2, norm_in, w_in[0], q_norm, k_norm, cos_t, sin_t, batch, seq)
    attn = _attention(q, k, v, sg, batch, seq)
    out = _out_projection(attn, u, e, x2, w_out[0].astype(bf16), conv_w[0], conv_b,
                          norm_final.reshape(1, d_model), seq)
    return out.reshape(batch, seq, d_model)
```

```python
import functools

import jax
import jax.numpy as jnp
from jax import lax
from jax.experimental import pallas as pl
from jax.experimental.pallas import tpu as pltpu

HEAD_DIM = 128
N_Q_HEADS = 8
N_KV_HEADS = 2
GQA_GROUP = N_Q_HEADS // N_KV_HEADS
ATTN_WIDTH = N_Q_HEADS * HEAD_DIM
CONV_K = 3
GRID_W = 64
ROPE_THETA = 10000.0
ROPE_AXIS_DIM = HEAD_DIM // 2
EPS = 1e-6
_LOG2_E = 1.4426950408889634

_V7X_VMEM_BYTES = 64 * 2**20
_VMEM_LIMIT = _V7X_VMEM_BYTES - 8 * 2**20
_BF16_SUBLANES = 16
_W_BLOCK = 256
_W_STAGE_BYTES = 2 * 2**20
_W_STAGE_SLOTS = 4

_PROJ_ROWS = 512
_ATTN_TQ = 128
_ATTN_TK = 1024
_ATTN_TILES_PER_ITER = 4
_OUT_ROWS = 512

f32 = jnp.float32
bf16 = jnp.bfloat16


def _silu(x):
    return x / (1.0 + jnp.exp(-x))


def _head_norm(x, g):
    ms = jnp.mean(x * x, axis=-1, keepdims=True)
    return x * lax.rsqrt(ms + EPS) * g


def _rope(x, c, s):
    lane = lax.broadcasted_iota(jnp.int32, x.shape, 1)
    first_half = (lane & (ROPE_AXIS_DIM // 2)) == 0
    swapped = jnp.where(first_half,
                        pltpu.roll(x, HEAD_DIM - ROPE_AXIS_DIM // 2, 1),
                        pltpu.roll(x, ROPE_AXIS_DIM // 2, 1))
    return x * c + swapped * s


def _rope_tables(seq_len):
    rows = seq_len // GRID_W
    row = jnp.repeat(jnp.arange(rows, dtype=f32), GRID_W)
    col = jnp.tile(jnp.arange(GRID_W, dtype=f32), rows)
    inv_freq = ROPE_THETA ** (-jnp.arange(0, ROPE_AXIS_DIM, 2, dtype=f32) / ROPE_AXIS_DIM)
    ang_r = row[:, None] * inv_freq[None, :]
    ang_c = col[:, None] * inv_freq[None, :]
    cr, sr, cc, sc = jnp.cos(ang_r), jnp.sin(ang_r), jnp.cos(ang_c), jnp.sin(ang_c)
    cos_t = jnp.concatenate([cr, cr, cc, cc], axis=-1)
    sin_t = jnp.concatenate([-sr, sr, -sc, sc], axis=-1)
    return cos_t, sin_t


_Q_BLOCK0, _K_BLOCK, _V_BLOCK, _G_BLOCK0 = 0, 4, 5, 6
_CB_BLOCK0, _CC_BLOCK0, _CX_BLOCK0, _GC_BLOCK0 = 10, 14, 18, 22
_N_CONV_BLOCKS = 4


def _rope_t(x_t, cos_tt, sin_tt):
    q = ROPE_AXIS_DIM // 2
    swapped = jnp.concatenate([x_t[q:2 * q], x_t[0:q], x_t[3 * q:4 * q], x_t[2 * q:3 * q]], axis=0)
    return x_t * cos_tt + swapped * sin_tt


def _weight_stage_scratch(w):
    rows = 1 << ((_W_STAGE_BYTES // (4 * w.shape[1])).bit_length() - 1)
    assert w.shape[0] % rows == 0 and rows % _BF16_SUBLANES == 0
    return [pltpu.VMEM(w.shape, bf16),
            pltpu.VMEM((_W_STAGE_SLOTS, rows, w.shape[1]), f32),
            pltpu.SemaphoreType.DMA((_W_STAGE_SLOTS,))]


def _load_weights_as_bf16(w_hbm, w_ref, stage_ref, sem):
    slots, chunk = stage_ref.shape[0], stage_ref.shape[1]
    n_chunks = w_ref.shape[0] // chunk

    def copy(c):
        slot = c % slots
        return pltpu.make_async_copy(w_hbm.at[pl.ds(c * chunk, chunk), :], stage_ref.at[slot], sem.at[slot])

    for c in range(slots - 1):
        copy(c).start()
    for c in range(n_chunks):
        if c + slots - 1 < n_chunks:
            copy(c + slots - 1).start()
        copy(c).wait()
        w_ref[c * chunk:(c + 1) * chunk, :] = stage_ref[c % slots].astype(bf16)


def _proj_kernel(x_ref, nin_ref, w_hbm, qg_ref, kn_ref, cos_ref, sin_ref, cost_ref, sint_ref,
                 q_ref, k_ref, v_ref, sg_ref, u_ref, e_ref, hn_ref, w_ref, stage_ref, sem):
    tm = hn_ref.shape[0]

    @pl.when(pl.program_id(0) == 0)
    def _():
        _load_weights_as_bf16(w_hbm, w_ref, stage_ref, sem)

    x = x_ref[...]
    ms = jnp.mean(x * x, axis=-1, keepdims=True)
    hn_ref[...] = (x * lax.rsqrt(ms + EPS) * nin_ref[...]).astype(bf16)

    def proj(block):
        w = w_ref[:, block * _W_BLOCK:(block + 1) * _W_BLOCK]
        return jnp.dot(hn_ref[...], w, preferred_element_type=f32)

    def head(mat, hh):
        return mat[:, hh * HEAD_DIM:(hh + 1) * HEAD_DIM]

    kk = proj(_K_BLOCK)
    for hh in range(N_KV_HEADS):
        k_ref[0, hh] = _rope(_head_norm(head(kk, hh), kn_ref[...]), cos_ref[...], sin_ref[...]).astype(bf16)

    cos_tt, sin_tt = cost_ref[...], sint_ref[...]
    gain = jnp.tile(qg_ref[...], (1, tm // HEAD_DIM))
    scale = HEAD_DIM ** -0.5 * _LOG2_E
    for pair in range(N_Q_HEADS // 2):
        qq = proj(_Q_BLOCK0 + pair)
        for hh in range(2):
            q_t = head(qq, hh).T
            ms = jnp.mean(q_t * q_t, axis=0, keepdims=True)
            r = lax.rsqrt(ms + EPS) * scale
            q_ref[0, 2 * pair + hh] = (_rope_t(q_t * gain, cos_tt, sin_tt) * r).astype(bf16)

    vv = proj(_V_BLOCK)
    for hh in range(N_KV_HEADS):
        v_ref[0, hh] = head(vv, hh).T.astype(bf16)

    for c in range(ATTN_WIDTH // _W_BLOCK):
        sg_ref[:, c * _W_BLOCK:(c + 1) * _W_BLOCK] = _silu(proj(_G_BLOCK0 + c)).astype(bf16)

    for c in range(_N_CONV_BLOCKS):
        cols = slice(c * _W_BLOCK, (c + 1) * _W_BLOCK)
        u_ref[:, cols] = (proj(_CC_BLOCK0 + c) * proj(_CX_BLOCK0 + c)).astype(bf16)
        e_ref[:, cols] = (proj(_CB_BLOCK0 + c) * _silu(proj(_GC_BLOCK0 + c))).astype(bf16)


def _in_projection(x2, norm_in, w, q_norm, k_norm, cos_t, sin_t, batch, seq):
    rows, d_model = x2.shape
    tm = _PROJ_ROWS
    nt = seq // tm
    full = lambda shape: pl.BlockSpec(shape, lambda i: (0,) * len(shape))
    table = pl.BlockSpec((tm, HEAD_DIM), lambda i: (i % nt, 0))
    table_t = pl.BlockSpec((HEAD_DIM, tm), lambda i: (0, i % nt))
    row_tile = lambda width: pl.BlockSpec((tm, width), lambda i: (i, 0))
    q_gain = jnp.broadcast_to(q_norm.reshape(HEAD_DIM, 1), (HEAD_DIM, HEAD_DIM))
    conv_width = _N_CONV_BLOCKS * _W_BLOCK
    out_shape = (
        jax.ShapeDtypeStruct((batch, N_Q_HEADS, HEAD_DIM, seq), bf16),
        jax.ShapeDtypeStruct((batch, N_KV_HEADS, seq, HEAD_DIM), bf16),
        jax.ShapeDtypeStruct((batch, N_KV_HEADS, HEAD_DIM, seq), bf16),
        jax.ShapeDtypeStruct((rows, ATTN_WIDTH), bf16),
        jax.ShapeDtypeStruct((rows, conv_width), bf16),
        jax.ShapeDtypeStruct((rows, conv_width), bf16),
    )
    out_specs = (
        pl.BlockSpec((1, N_Q_HEADS, HEAD_DIM, tm), lambda i: (i // nt, 0, 0, i % nt)),
        pl.BlockSpec((1, N_KV_HEADS, tm, HEAD_DIM), lambda i: (i // nt, 0, i % nt, 0)),
        pl.BlockSpec((1, N_KV_HEADS, HEAD_DIM, tm), lambda i: (i // nt, 0, 0, i % nt)),
        row_tile(ATTN_WIDTH), row_tile(conv_width), row_tile(conv_width),
    )
    return pl.pallas_call(
        _proj_kernel,
        grid=(rows // tm,),
        in_specs=[
            row_tile(d_model),
            full((1, d_model)),
            pl.BlockSpec(memory_space=pl.ANY),
            full((HEAD_DIM, HEAD_DIM)), full((1, HEAD_DIM)),
            table, table, table_t, table_t,
        ],
        out_specs=out_specs,
        out_shape=out_shape,
        scratch_shapes=[pltpu.VMEM((tm, d_model), bf16)] + _weight_stage_scratch(w),
        compiler_params=pltpu.CompilerParams(
            dimension_semantics=("arbitrary",),
            vmem_limit_bytes=_VMEM_LIMIT),
        name="in_projection",
    )(x2, norm_in, w, q_gain, k_norm, cos_t, sin_t, cos_t.T, sin_t.T)


def _attn_kernel(q_ref, k_ref, v_ref, sg_ref, o_ref, acc_ref, s_ref, *, tq, tk, nk, nq):
    cols = GQA_GROUP * tq

    def q_tile(i):
        off = pl.multiple_of(i * tq, tq)
        return jnp.concatenate([q_ref[0, h, :, pl.ds(off, tq)] for h in range(GQA_GROUP)], axis=1)

    def scores(q_mat, j, slot):
        k = k_ref[0, 0, pl.ds(pl.multiple_of(j * tk, tk), tk), :]
        s_t = jnp.dot(k, q_mat, preferred_element_type=f32)
        s_ref[slot] = s_t
        return jnp.max(s_t, axis=0, keepdims=True)

    ones_rows = jnp.ones((_BF16_SUBLANES, tk), bf16)

    def step(j, slot, acc, m_prev, chunk_max, q_ahead, j_ahead):
        ahead_max = scores(q_ahead, j_ahead, 1 - slot)
        off = pl.multiple_of(j * tk, tk)
        v_aug = jnp.concatenate([v_ref[0, 0, :, pl.ds(off, tk)], ones_rows], axis=0)
        m_new = jnp.maximum(m_prev, chunk_max)
        alpha = jnp.exp2(m_prev - m_new)
        p_t = jnp.exp2(s_ref[slot] - m_new).astype(bf16)
        pv_t = jnp.dot(v_aug, p_t, preferred_element_type=f32)
        acc[...] = alpha * acc[...] + pv_t
        return m_new, ahead_max

    def query_tile(i, acc, chunk_max):
        q_t = q_tile(i)
        q_next = q_tile(jnp.minimum(i + 1, nq - 1))
        acc[...] = jnp.zeros(acc.shape, f32)

        m_run = jnp.full((1, cols), -jnp.inf, f32)
        for c in range(nk):
            last = c == nk - 1
            m_run, chunk_max = step(c, c % 2, acc, m_run, chunk_max, q_next if last else q_t, 0 if last else c + 1)
        o_t = acc[:HEAD_DIM, :] / acc[HEAD_DIM:HEAD_DIM + 1, :]
        rows = pl.ds(pl.multiple_of(i * tq, tq), tq)
        for h in range(GQA_GROUP):
            lanes = slice(h * HEAD_DIM, (h + 1) * HEAD_DIM)
            gate = sg_ref[rows, lanes].astype(f32)
            o_ref[rows, lanes] = (o_t[:, h * tq:(h + 1) * tq].T * gate).astype(bf16)
        return chunk_max

    def tile_group(g, chunk_max):
        for t in range(_ATTN_TILES_PER_ITER):
            chunk_max = query_tile(g * _ATTN_TILES_PER_ITER + t, acc_ref.at[t], chunk_max)
        return chunk_max

    lax.fori_loop(0, nq // _ATTN_TILES_PER_ITER, tile_group, scores(q_tile(0), 0, 0))


def _attention(q_t, k, v_t, sg, batch, seq):
    tq, tk = _ATTN_TQ, _ATTN_TK
    cols = GQA_GROUP * tq
    group_width = GQA_GROUP * HEAD_DIM
    gate_spec = pl.BlockSpec((seq, group_width), lambda b, h: (b, h))
    return pl.pallas_call(
        functools.partial(_attn_kernel, tq=tq, tk=tk, nk=seq // tk, nq=seq // tq),
        grid=(batch, N_KV_HEADS),
        in_specs=[
            pl.BlockSpec((1, GQA_GROUP, HEAD_DIM, seq), lambda b, h: (b, h, 0, 0)),
            pl.BlockSpec((1, 1, seq, HEAD_DIM), lambda b, h: (b, h, 0, 0)),
            pl.BlockSpec((1, 1, HEAD_DIM, seq), lambda b, h: (b, h, 0, 0)),
            gate_spec,
        ],
        out_specs=gate_spec,
        out_shape=jax.ShapeDtypeStruct((batch * seq, ATTN_WIDTH), bf16),
        scratch_shapes=[pltpu.VMEM((_ATTN_TILES_PER_ITER, HEAD_DIM + _BF16_SUBLANES, cols), f32),
                        pltpu.VMEM((2, tk, cols), f32)],
        compiler_params=pltpu.CompilerParams(
            dimension_semantics=("arbitrary", "arbitrary"),
            vmem_limit_bytes=_VMEM_LIMIT),
        name="gqa_attention",
    )(q_t, k, v_t, sg)


def _out_kernel(attn_ref, u_ref, uprev_ref, unext_ref, e_ref, x_ref, w_hbm, cw_ref, cb_ref, nf_ref,
                o_ref, w_ref, stage_ref, sem, *, tm, seq):
    i = pl.program_id(0)

    @pl.when(i == 0)
    def _():
        _load_weights_as_bf16(w_hbm, w_ref, stage_ref, sem)

    t0 = (i * tm) % seq
    u = u_ref[...].astype(f32)
    prev_row = uprev_ref[...].astype(f32)[_BF16_SUBLANES - 1:_BF16_SUBLANES, :]
    next_row = unext_ref[...].astype(f32)[0:1, :]
    prev_row = jnp.where(t0 == 0, 0.0, prev_row)
    next_row = jnp.where(t0 + tm == seq, 0.0, next_row)
    row = lax.broadcasted_iota(jnp.int32, u.shape, 0)
    u_m1 = jnp.where(row == 0, prev_row, pltpu.roll(u, 1, 0))
    u_p1 = jnp.where(row == tm - 1, next_row, pltpu.roll(u, tm - 1, 0))
    cw = cw_ref[...]
    y = cb_ref[...] + u_m1 * cw[0:1, :]
    y = y + u * cw[1:2, :]
    y = y + u_p1 * cw[2:3, :]
    conv = (e_ref[...].astype(f32) * y).astype(bf16)
    acc = jnp.dot(attn_ref[...], w_ref[:ATTN_WIDTH, :], preferred_element_type=f32)
    acc = acc + jnp.dot(conv, w_ref[ATTN_WIDTH:, :], preferred_element_type=f32)
    h = x_ref[...] + acc
    ms = jnp.mean(h * h, axis=-1, keepdims=True)
    o_ref[...] = h * lax.rsqrt(ms + EPS) * nf_ref[...]


def _out_projection(attn, u, e, x2, w, conv_w, conv_b, norm_final, seq):
    rows, d_model = x2.shape
    tm = _OUT_ROWS
    conv_width = u.shape[1]
    halo_blocks = rows // _BF16_SUBLANES
    per_tile = tm // _BF16_SUBLANES
    row_tile = lambda width: pl.BlockSpec((tm, width), lambda i: (i, 0))
    full = lambda shape: pl.BlockSpec(shape, lambda i: (0,) * len(shape))
    return pl.pallas_call(
        functools.partial(_out_kernel, tm=tm, seq=seq),
        grid=(rows // tm,),
        in_specs=[
            row_tile(ATTN_WIDTH),
            row_tile(conv_width),
            pl.BlockSpec((_BF16_SUBLANES, conv_width), lambda i: (jnp.maximum(i * per_tile - 1, 0), 0)),
            pl.BlockSpec((_BF16_SUBLANES, conv_width),
                         lambda i: (jnp.minimum((i + 1) * per_tile, halo_blocks - 1), 0)),
            row_tile(conv_width),
            row_tile(d_model),
            pl.BlockSpec(memory_space=pl.ANY),
            full(conv_w.shape), full(conv_b.shape), full(norm_final.shape),
        ],
        out_specs=row_tile(d_model),
        out_shape=jax.ShapeDtypeStruct((rows, d_model), f32),
        scratch_shapes=_weight_stage_scratch(w),
        compiler_params=pltpu.CompilerParams(
            dimension_semantics=("arbitrary",),
            vmem_limit_bytes=_VMEM_LIMIT),
        name="out_projection",
    )(attn, u, u, u, e, x2, w, conv_w, conv_b, norm_final)


def kernel(x, norm_in, w_in, q_norm, k_norm, conv_w, conv_b, w_out, norm_final):
    batch, seq, d_model = x.shape
    assert norm_in.shape[0] == 1, "single-layer block"
    assert seq % _PROJ_ROWS == 0 and seq % _OUT_ROWS == 0 and seq % _ATTN_TK == 0 and seq % _ATTN_TQ == 0
    x2 = x.reshape(batch * seq, d_model)
    cos_t, sin_t = _rope_tables(seq)
    q, k, v, sg, u, e = _in_projection(
        x2, norm_in, w_in[0], q_norm, k_norm, cos_t, sin_t, batch, seq)
    attn = _attention(q, k, v, sg, batch, seq)
    out = _out_projection(attn, u, e, x2, w_out[0], conv_w[0], conv_b,
                          norm_final.reshape(1, d_model), seq)
    return out.reshape(batch, seq, d_model)
```

```python
import functools

import jax
import jax.numpy as jnp
from jax import lax
from jax.experimental import pallas as pl
from jax.experimental.pallas import tpu as pltpu

HEAD_DIM = 128
N_Q_HEADS = 8
N_KV_HEADS = 2
GQA_GROUP = N_Q_HEADS // N_KV_HEADS
ATTN_WIDTH = N_Q_HEADS * HEAD_DIM
CONV_K = 3
GRID_W = 64
ROPE_THETA = 10000.0
ROPE_AXIS_DIM = HEAD_DIM // 2
EPS = 1e-6
_LOG2_E = 1.4426950408889634

_V7X_VMEM_BYTES = 64 * 2**20
_VMEM_LIMIT = _V7X_VMEM_BYTES - 8 * 2**20
_BF16_SUBLANES = 16
_F32_SUBLANES = 8
_EDGE_ROWS = 4 * _F32_SUBLANES
_W_BLOCK = 256
_W_STAGE_BYTES = 2 * 2**20
_W_STAGE_SLOTS = 4

_PROJ_ROWS = 512
_ATTN_TQ = 128
_ATTN_TK = 1024
_ATTN_TILES_PER_ITER = 4
_OUT_ROWS = 512

f32 = jnp.float32
bf16 = jnp.bfloat16


def _silu(x):
    return x / (1.0 + jnp.exp(-x))


def _head_norm(x, g):
    ms = jnp.mean(x * x, axis=-1, keepdims=True)
    return x * lax.rsqrt(ms + EPS) * g


def _rope(x, c, s):
    lane = lax.broadcasted_iota(jnp.int32, x.shape, 1)
    first_half = (lane & (ROPE_AXIS_DIM // 2)) == 0
    swapped = jnp.where(first_half,
                        pltpu.roll(x, HEAD_DIM - ROPE_AXIS_DIM // 2, 1),
                        pltpu.roll(x, ROPE_AXIS_DIM // 2, 1))
    return x * c + swapped * s


def _rope_tables(seq_len):
    rows = seq_len // GRID_W
    row = jnp.repeat(jnp.arange(rows, dtype=f32), GRID_W)
    col = jnp.tile(jnp.arange(GRID_W, dtype=f32), rows)
    inv_freq = ROPE_THETA ** (-jnp.arange(0, ROPE_AXIS_DIM, 2, dtype=f32) / ROPE_AXIS_DIM)
    ang_r = row[:, None] * inv_freq[None, :]
    ang_c = col[:, None] * inv_freq[None, :]
    cr, sr, cc, sc = jnp.cos(ang_r), jnp.sin(ang_r), jnp.cos(ang_c), jnp.sin(ang_c)
    cos_t = jnp.concatenate([cr, cr, cc, cc], axis=-1)
    sin_t = jnp.concatenate([-sr, sr, -sc, sc], axis=-1)
    return cos_t, sin_t


_Q_BLOCK0, _K_BLOCK, _V_BLOCK, _G_BLOCK0 = 0, 4, 5, 6
_CB_BLOCK0, _CC_BLOCK0, _CX_BLOCK0, _GC_BLOCK0 = 10, 14, 18, 22
_N_CONV_BLOCKS = 4


def _rope_t(x_t, cos_tt, sin_tt):
    q = ROPE_AXIS_DIM // 2
    swapped = jnp.concatenate([x_t[q:2 * q], x_t[0:q], x_t[3 * q:4 * q], x_t[2 * q:3 * q]], axis=0)
    return x_t * cos_tt + swapped * sin_tt


def _weight_stage_scratch(w):
    rows = 1 << ((_W_STAGE_BYTES // (4 * w.shape[1])).bit_length() - 1)
    assert w.shape[0] % rows == 0 and rows % _BF16_SUBLANES == 0
    return [pltpu.VMEM(w.shape, bf16),
            pltpu.VMEM((_W_STAGE_SLOTS, rows, w.shape[1]), f32),
            pltpu.SemaphoreType.DMA((_W_STAGE_SLOTS,))]


def _load_weights_as_bf16(w_hbm, w_ref, stage_ref, sem):
    slots, chunk = stage_ref.shape[0], stage_ref.shape[1]
    n_chunks = w_ref.shape[0] // chunk

    def copy(c):
        slot = c % slots
        return pltpu.make_async_copy(w_hbm.at[pl.ds(c * chunk, chunk), :], stage_ref.at[slot], sem.at[slot])

    for c in range(slots - 1):
        copy(c).start()
    for c in range(n_chunks):
        if c + slots - 1 < n_chunks:
            copy(c + slots - 1).start()
        copy(c).wait()
        w_ref[c * chunk:(c + 1) * chunk, :] = stage_ref[c % slots].astype(bf16)


def _conv_taps(u_prev, u_mid, u_next, cw, cb):
    y = cb + u_prev * cw[0:1, :]
    y = y + u_mid * cw[1:2, :]
    return y + u_next * cw[2:3, :]


def _proj_kernel(x_ref, nin_ref, w_hbm, qg_ref, kn_ref, cos_ref, sin_ref, cost_ref, sint_ref, cw_ref, cb_ref,
                 q_ref, k_ref, v_ref, sg_ref, conv_ref, edge_ref, hn_ref, w_ref, stage_ref, sem):
    tm = hn_ref.shape[0]

    @pl.when(pl.program_id(0) == 0)
    def _():
        _load_weights_as_bf16(w_hbm, w_ref, stage_ref, sem)

    x = x_ref[...]
    ms = jnp.mean(x * x, axis=-1, keepdims=True)
    hn_ref[...] = (x * lax.rsqrt(ms + EPS) * nin_ref[...]).astype(bf16)

    def proj(block):
        w = w_ref[:, block * _W_BLOCK:(block + 1) * _W_BLOCK]
        return jnp.dot(hn_ref[...], w, preferred_element_type=f32)

    def head(mat, hh):
        return mat[:, hh * HEAD_DIM:(hh + 1) * HEAD_DIM]

    kk = proj(_K_BLOCK)
    for hh in range(N_KV_HEADS):
        k_ref[0, hh] = _rope(_head_norm(head(kk, hh), kn_ref[...]), cos_ref[...], sin_ref[...]).astype(bf16)

    cos_tt, sin_tt = cost_ref[...], sint_ref[...]
    gain = jnp.tile(qg_ref[...], (1, tm // HEAD_DIM))
    scale = HEAD_DIM ** -0.5 * _LOG2_E
    for pair in range(N_Q_HEADS // 2):
        qq = proj(_Q_BLOCK0 + pair)
        for hh in range(2):
            q_t = head(qq, hh).T
            ms = jnp.mean(q_t * q_t, axis=0, keepdims=True)
            r = lax.rsqrt(ms + EPS) * scale
            q_ref[0, 2 * pair + hh] = (_rope_t(q_t * gain, cos_tt, sin_tt) * r).astype(bf16)

    vv = proj(_V_BLOCK)
    for hh in range(N_KV_HEADS):
        v_ref[0, hh] = head(vv, hh).T.astype(bf16)

    for c in range(ATTN_WIDTH // _W_BLOCK):
        sg_ref[:, c * _W_BLOCK:(c + 1) * _W_BLOCK] = _silu(proj(_G_BLOCK0 + c)).astype(bf16)

    g = _F32_SUBLANES
    for c in range(_N_CONV_BLOCKS):
        cols = slice(c * _W_BLOCK, (c + 1) * _W_BLOCK)
        u = proj(_CC_BLOCK0 + c) * proj(_CX_BLOCK0 + c)
        e = proj(_CB_BLOCK0 + c) * _silu(proj(_GC_BLOCK0 + c))
        y = _conv_taps(pltpu.roll(u, 1, 0), u, pltpu.roll(u, tm - 1, 0), cw_ref[:, cols], cb_ref[:, cols])
        conv_ref[:, cols] = (e * y).astype(bf16)
        edge_ref[0, 0 * g:1 * g, cols] = u[:g]
        edge_ref[0, 1 * g:2 * g, cols] = u[tm - g:]
        edge_ref[0, 2 * g:3 * g, cols] = e[:g]
        edge_ref[0, 3 * g:4 * g, cols] = e[tm - g:]


def _in_projection(x2, norm_in, w, q_norm, k_norm, cos_t, sin_t, conv_w, conv_b, batch, seq):
    rows, d_model = x2.shape
    tm = _PROJ_ROWS
    nt = seq // tm
    full = lambda shape: pl.BlockSpec(shape, lambda i: (0,) * len(shape))
    table = pl.BlockSpec((tm, HEAD_DIM), lambda i: (i % nt, 0))
    table_t = pl.BlockSpec((HEAD_DIM, tm), lambda i: (0, i % nt))
    row_tile = lambda width: pl.BlockSpec((tm, width), lambda i: (i, 0))
    q_gain = jnp.broadcast_to(q_norm.reshape(HEAD_DIM, 1), (HEAD_DIM, HEAD_DIM))
    conv_width = _N_CONV_BLOCKS * _W_BLOCK
    out_shape = (
        jax.ShapeDtypeStruct((batch, N_Q_HEADS, HEAD_DIM, seq), bf16),
        jax.ShapeDtypeStruct((batch, N_KV_HEADS, seq, HEAD_DIM), bf16),
        jax.ShapeDtypeStruct((batch, N_KV_HEADS, HEAD_DIM, seq), bf16),
        jax.ShapeDtypeStruct((rows, ATTN_WIDTH), bf16),
        jax.ShapeDtypeStruct((rows, conv_width), bf16),
        jax.ShapeDtypeStruct((rows // tm, _EDGE_ROWS, conv_width), f32),
    )
    out_specs = (
        pl.BlockSpec((1, N_Q_HEADS, HEAD_DIM, tm), lambda i: (i // nt, 0, 0, i % nt)),
        pl.BlockSpec((1, N_KV_HEADS, tm, HEAD_DIM), lambda i: (i // nt, 0, i % nt, 0)),
        pl.BlockSpec((1, N_KV_HEADS, HEAD_DIM, tm), lambda i: (i // nt, 0, 0, i % nt)),
        row_tile(ATTN_WIDTH), row_tile(conv_width),
        pl.BlockSpec((1, _EDGE_ROWS, conv_width), lambda i: (i, 0, 0)),
    )
    return pl.pallas_call(
        _proj_kernel,
        grid=(rows // tm,),
        in_specs=[
            row_tile(d_model),
            full((1, d_model)),
            pl.BlockSpec(memory_space=pl.ANY),
            full((HEAD_DIM, HEAD_DIM)), full((1, HEAD_DIM)),
            table, table, table_t, table_t,
            full(conv_w.shape), full(conv_b.shape),
        ],
        out_specs=out_specs,
        out_shape=out_shape,
        scratch_shapes=[pltpu.VMEM((tm, d_model), bf16)] + _weight_stage_scratch(w),
        compiler_params=pltpu.CompilerParams(
            dimension_semantics=("arbitrary",),
            vmem_limit_bytes=_VMEM_LIMIT),
        name="in_projection",
    )(x2, norm_in, w, q_gain, k_norm, cos_t, sin_t, cos_t.T, sin_t.T, conv_w, conv_b)


def _attn_kernel(q_ref, k_ref, v_ref, sg_ref, o_ref, acc_ref, s_ref, *, tq, tk, nk, nq):
    cols = GQA_GROUP * tq

    def q_tile(i):
        off = pl.multiple_of(i * tq, tq)
        return jnp.concatenate([q_ref[0, h, :, pl.ds(off, tq)] for h in range(GQA_GROUP)], axis=1)

    def scores(q_mat, j, slot):
        k = k_ref[0, 0, pl.ds(pl.multiple_of(j * tk, tk), tk), :]
        s_t = jnp.dot(k, q_mat, preferred_element_type=f32)
        s_ref[slot] = s_t
        return jnp.max(s_t, axis=0, keepdims=True)

    ones_rows = jnp.ones((_BF16_SUBLANES, tk), bf16)

    def step(j, slot, acc, m_prev, chunk_max, q_ahead, j_ahead):
        ahead_max = scores(q_ahead, j_ahead, 1 - slot)
        off = pl.multiple_of(j * tk, tk)
        v_aug = jnp.concatenate([v_ref[0, 0, :, pl.ds(off, tk)], ones_rows], axis=0)
        m_new = jnp.maximum(m_prev, chunk_max)
        alpha = jnp.exp2(m_prev - m_new)
        p_t = jnp.exp2(s_ref[slot] - m_new).astype(bf16)
        pv_t = jnp.dot(v_aug, p_t, preferred_element_type=f32)
        acc[...] = alpha * acc[...] + pv_t
        return m_new, ahead_max

    def query_tile(i, acc, chunk_max):
        q_t = q_tile(i)
        q_next = q_tile(jnp.minimum(i + 1, nq - 1))
        acc[...] = jnp.zeros(acc.shape, f32)

        m_run = jnp.full((1, cols), -jnp.inf, f32)
        for c in range(nk):
            last = c == nk - 1
            m_run, chunk_max = step(c, c % 2, acc, m_run, chunk_max, q_next if last else q_t, 0 if last else c + 1)
        o_t = acc[:HEAD_DIM, :] / acc[HEAD_DIM:HEAD_DIM + 1, :]
        rows = pl.ds(pl.multiple_of(i * tq, tq), tq)
        for h in range(GQA_GROUP):
            lanes = slice(h * HEAD_DIM, (h + 1) * HEAD_DIM)
            gate = sg_ref[rows, lanes].astype(f32)
            o_ref[rows, lanes] = (o_t[:, h * tq:(h + 1) * tq].T * gate).astype(bf16)
        return chunk_max

    def tile_group(g, chunk_max):
        for t in range(_ATTN_TILES_PER_ITER):
            chunk_max = query_tile(g * _ATTN_TILES_PER_ITER + t, acc_ref.at[t], chunk_max)
        return chunk_max

    lax.fori_loop(0, nq // _ATTN_TILES_PER_ITER, tile_group, scores(q_tile(0), 0, 0))


def _attention(q_t, k, v_t, sg, batch, seq):
    tq, tk = _ATTN_TQ, _ATTN_TK
    cols = GQA_GROUP * tq
    group_width = GQA_GROUP * HEAD_DIM
    gate_spec = pl.BlockSpec((seq, group_width), lambda b, h: (b, h))
    return pl.pallas_call(
        functools.partial(_attn_kernel, tq=tq, tk=tk, nk=seq // tk, nq=seq // tq),
        grid=(batch, N_KV_HEADS),
        in_specs=[
            pl.BlockSpec((1, GQA_GROUP, HEAD_DIM, seq), lambda b, h: (b, h, 0, 0)),
            pl.BlockSpec((1, 1, seq, HEAD_DIM), lambda b, h: (b, h, 0, 0)),
            pl.BlockSpec((1, 1, HEAD_DIM, seq), lambda b, h: (b, h, 0, 0)),
            gate_spec,
        ],
        out_specs=gate_spec,
        out_shape=jax.ShapeDtypeStruct((batch * seq, ATTN_WIDTH), bf16),
        scratch_shapes=[pltpu.VMEM((_ATTN_TILES_PER_ITER, HEAD_DIM + _BF16_SUBLANES, cols), f32),
                        pltpu.VMEM((2, tk, cols), f32)],
        compiler_params=pltpu.CompilerParams(
            dimension_semantics=("arbitrary", "arbitrary"),
            vmem_limit_bytes=_VMEM_LIMIT),
        name="gqa_attention",
    )(q_t, k, v_t, sg)


def _out_kernel(attn_ref, conv_ref, eprev_ref, edge_ref, enext_ref, x_ref, w_hbm, cw_ref, cb_ref, nf_ref,
                o_ref, w_ref, stage_ref, sem, *, tm, seq):
    i = pl.program_id(0)

    @pl.when(i == 0)
    def _():
        _load_weights_as_bf16(w_hbm, w_ref, stage_ref, sem)

    g = _F32_SUBLANES
    t0 = (i * tm) % seq
    cw, cb = cw_ref[...], cb_ref[...]
    row = lambda ref, r: ref[0, r:r + 1, :]
    u_before = jnp.where(t0 == 0, 0.0, row(eprev_ref, 2 * g - 1))
    u_after = jnp.where(t0 + tm == seq, 0.0, row(enext_ref, 0))
    first = row(edge_ref, 2 * g) * _conv_taps(u_before, row(edge_ref, 0), row(edge_ref, 1), cw, cb)
    last = row(edge_ref, 4 * g - 1) * _conv_taps(row(edge_ref, 2 * g - 2), row(edge_ref, 2 * g - 1), u_after, cw, cb)
    p = _BF16_SUBLANES
    sub = lax.broadcasted_iota(jnp.int32, (p, conv_ref.shape[1]), 0)
    top = jnp.where(sub == 0, first, conv_ref[:p, :].astype(f32)).astype(bf16)
    bottom = jnp.where(sub == p - 1, last, conv_ref[tm - p:, :].astype(f32)).astype(bf16)
    conv = jnp.concatenate([top, conv_ref[p:tm - p, :], bottom], axis=0)

    acc = jnp.dot(attn_ref[...], w_ref[:ATTN_WIDTH, :], preferred_element_type=f32)
    acc = acc + jnp.dot(conv, w_ref[ATTN_WIDTH:, :], preferred_element_type=f32)
    h = x_ref[...] + acc
    ms = jnp.mean(h * h, axis=-1, keepdims=True)
    o_ref[...] = h * lax.rsqrt(ms + EPS) * nf_ref[...]


def _out_projection(attn, conv, edge, x2, w, conv_w, conv_b, norm_final, seq):
    rows, d_model = x2.shape
    tm = _OUT_ROWS
    n_tiles = rows // tm
    assert edge.shape[0] == n_tiles, "edge rows are saved per projection tile"
    conv_width = conv.shape[1]
    row_tile = lambda width: pl.BlockSpec((tm, width), lambda i: (i, 0))
    full = lambda shape: pl.BlockSpec(shape, lambda i: (0,) * len(shape))
    edge_of = lambda shift: pl.BlockSpec((1, _EDGE_ROWS, conv_width),
                                         lambda i: (jnp.clip(i + shift, 0, n_tiles - 1), 0, 0))
    return pl.pallas_call(
        functools.partial(_out_kernel, tm=tm, seq=seq),
        grid=(n_tiles,),
        in_specs=[
            row_tile(ATTN_WIDTH),
            row_tile(conv_width),
            edge_of(-1), edge_of(0), edge_of(1),
            row_tile(d_model),
            pl.BlockSpec(memory_space=pl.ANY),
            full(conv_w.shape), full(conv_b.shape), full(norm_final.shape),
        ],
        out_specs=row_tile(d_model),
        out_shape=jax.ShapeDtypeStruct((rows, d_model), f32),
        scratch_shapes=_weight_stage_scratch(w),
        compiler_params=pltpu.CompilerParams(
            dimension_semantics=("arbitrary",),
            vmem_limit_bytes=_VMEM_LIMIT),
        name="out_projection",
    )(attn, conv, edge, edge, edge, x2, w, conv_w, conv_b, norm_final)


def kernel(x, norm_in, w_in, q_norm, k_norm, conv_w, conv_b, w_out, norm_final):
    batch, seq, d_model = x.shape
    assert norm_in.shape[0] == 1, "single-layer block"
    assert _PROJ_ROWS == _OUT_ROWS, "conv edge rows are exchanged per row tile"
    assert seq % _PROJ_ROWS == 0 and seq % _ATTN_TK == 0 and seq % (_ATTN_TQ * _ATTN_TILES_PER_ITER) == 0
    x2 = x.reshape(batch * seq, d_model)
    cos_t, sin_t = _rope_tables(seq)
    q, k, v, sg, conv, edge = _in_projection(
        x2, norm_in, w_in[0], q_norm, k_norm, cos_t, sin_t, conv_w[0], conv_b, batch, seq)
    attn = _attention(q, k, v, sg, batch, seq)
    out = _out_projection(attn, conv, edge, x2, w_out[0], conv_w[0], conv_b,
                          norm_final.reshape(1, d_model), seq)
    return out.reshape(batch, seq, d_model)
```

```python
import functools

import jax
import jax.numpy as jnp
from jax import lax
from jax.experimental import pallas as pl
from jax.experimental.pallas import tpu as pltpu

HEAD_DIM = 128
N_Q_HEADS = 8
N_KV_HEADS = 2
GQA_GROUP = N_Q_HEADS // N_KV_HEADS
ATTN_WIDTH = N_Q_HEADS * HEAD_DIM
CONV_K = 3
GRID_W = 64
ROPE_THETA = 10000.0
ROPE_AXIS_DIM = HEAD_DIM // 2
EPS = 1e-6
_LOG2_E = 1.4426950408889634
_SCORE_BOUND = 64.0

_V7X_VMEM_BYTES = 64 * 2**20
_VMEM_LIMIT = _V7X_VMEM_BYTES - 8 * 2**20
_BF16_SUBLANES = 16
_F32_SUBLANES = 8
_EDGE_ROWS = 4 * _F32_SUBLANES
_W_BLOCK = 256
_W_STAGE_BYTES = 2 * 2**20
_W_STAGE_SLOTS = 4

_PROJ_ROWS = 512
_ATTN_TQ = 128
_ATTN_TK = 1024
_ATTN_TILES_PER_ITER = 4
_OUT_ROWS = 512

f32 = jnp.float32
bf16 = jnp.bfloat16


def _silu(x):
    return x / (1.0 + jnp.exp(-x))


def _head_norm(x, g):
    ms = jnp.mean(x * x, axis=-1, keepdims=True)
    return x * lax.rsqrt(ms + EPS) * g


def _rope(x, c, s):
    lane = lax.broadcasted_iota(jnp.int32, x.shape, 1)
    first_half = (lane & (ROPE_AXIS_DIM // 2)) == 0
    swapped = jnp.where(first_half,
                        pltpu.roll(x, HEAD_DIM - ROPE_AXIS_DIM // 2, 1),
                        pltpu.roll(x, ROPE_AXIS_DIM // 2, 1))
    return x * c + swapped * s


def _rope_tables(seq_len):
    rows = seq_len // GRID_W
    row = jnp.repeat(jnp.arange(rows, dtype=f32), GRID_W)
    col = jnp.tile(jnp.arange(GRID_W, dtype=f32), rows)
    inv_freq = ROPE_THETA ** (-jnp.arange(0, ROPE_AXIS_DIM, 2, dtype=f32) / ROPE_AXIS_DIM)
    ang_r = row[:, None] * inv_freq[None, :]
    ang_c = col[:, None] * inv_freq[None, :]
    cr, sr, cc, sc = jnp.cos(ang_r), jnp.sin(ang_r), jnp.cos(ang_c), jnp.sin(ang_c)
    cos_t = jnp.concatenate([cr, cr, cc, cc], axis=-1)
    sin_t = jnp.concatenate([-sr, sr, -sc, sc], axis=-1)
    return cos_t, sin_t


_Q_BLOCK0, _K_BLOCK, _V_BLOCK, _G_BLOCK0 = 0, 4, 5, 6
_CB_BLOCK0, _CC_BLOCK0, _CX_BLOCK0, _GC_BLOCK0 = 10, 14, 18, 22
_N_CONV_BLOCKS = 4


def _rope_t(x_t, cos_tt, sin_tt):
    q = ROPE_AXIS_DIM // 2
    swapped = jnp.concatenate([x_t[q:2 * q], x_t[0:q], x_t[3 * q:4 * q], x_t[2 * q:3 * q]], axis=0)
    return x_t * cos_tt + swapped * sin_tt


def _weight_stage_scratch(w):
    rows = 1 << ((_W_STAGE_BYTES // (4 * w.shape[1])).bit_length() - 1)
    assert w.shape[0] % rows == 0 and rows % _BF16_SUBLANES == 0
    return [pltpu.VMEM(w.shape, bf16),
            pltpu.VMEM((_W_STAGE_SLOTS, rows, w.shape[1]), f32),
            pltpu.SemaphoreType.DMA((_W_STAGE_SLOTS,))]


def _load_weights_as_bf16(w_hbm, w_ref, stage_ref, sem):
    slots, chunk = stage_ref.shape[0], stage_ref.shape[1]
    n_chunks = w_ref.shape[0] // chunk

    def copy(c):
        slot = c % slots
        return pltpu.make_async_copy(w_hbm.at[pl.ds(c * chunk, chunk), :], stage_ref.at[slot], sem.at[slot])

    for c in range(slots - 1):
        copy(c).start()
    for c in range(n_chunks):
        if c + slots - 1 < n_chunks:
            copy(c + slots - 1).start()
        copy(c).wait()
        w_ref[c * chunk:(c + 1) * chunk, :] = stage_ref[c % slots].astype(bf16)


def _conv_taps(u_prev, u_mid, u_next, cw, cb):
    y = cb + u_prev * cw[0:1, :]
    y = y + u_mid * cw[1:2, :]
    return y + u_next * cw[2:3, :]


def _proj_kernel(x_ref, nin_ref, w_hbm, qg_ref, kn_ref, cos_ref, sin_ref, cost_ref, sint_ref, cw_ref, cb_ref,
                 q_ref, k_ref, v_ref, sg_ref, conv_ref, edge_ref, hn_ref, w_ref, stage_ref, sem):
    tm = hn_ref.shape[0]

    @pl.when(pl.program_id(0) == 0)
    def _():
        _load_weights_as_bf16(w_hbm, w_ref, stage_ref, sem)

    x = x_ref[...]
    ms = jnp.mean(x * x, axis=-1, keepdims=True)
    hn_ref[...] = (x * lax.rsqrt(ms + EPS) * nin_ref[...]).astype(bf16)

    def proj(block):
        w = w_ref[:, block * _W_BLOCK:(block + 1) * _W_BLOCK]
        return jnp.dot(hn_ref[...], w, preferred_element_type=f32)

    def head(mat, hh):
        return mat[:, hh * HEAD_DIM:(hh + 1) * HEAD_DIM]

    kk = proj(_K_BLOCK)
    for hh in range(N_KV_HEADS):
        k_ref[0, hh] = _rope(_head_norm(head(kk, hh), kn_ref[...]), cos_ref[...], sin_ref[...]).astype(bf16)

    cos_tt, sin_tt = cost_ref[...], sint_ref[...]
    gain = jnp.tile(qg_ref[...], (1, tm // HEAD_DIM))
    scale = HEAD_DIM ** -0.5 * _LOG2_E
    for pair in range(N_Q_HEADS // 2):
        qq = proj(_Q_BLOCK0 + pair)
        for hh in range(2):
            q_t = head(qq, hh).T
            ms = jnp.mean(q_t * q_t, axis=0, keepdims=True)
            r = lax.rsqrt(ms + EPS) * scale
            q_ref[0, 2 * pair + hh] = (_rope_t(q_t * gain, cos_tt, sin_tt) * r).astype(bf16)

    vv = proj(_V_BLOCK)
    for hh in range(N_KV_HEADS):
        v_ref[0, hh] = head(vv, hh).T.astype(bf16)

    for c in range(ATTN_WIDTH // _W_BLOCK):
        sg_ref[:, c * _W_BLOCK:(c + 1) * _W_BLOCK] = _silu(proj(_G_BLOCK0 + c)).astype(bf16)

    g = _F32_SUBLANES
    for c in range(_N_CONV_BLOCKS):
        cols = slice(c * _W_BLOCK, (c + 1) * _W_BLOCK)
        u = proj(_CC_BLOCK0 + c) * proj(_CX_BLOCK0 + c)
        e = proj(_CB_BLOCK0 + c) * _silu(proj(_GC_BLOCK0 + c))
        y = _conv_taps(pltpu.roll(u, 1, 0), u, pltpu.roll(u, tm - 1, 0), cw_ref[:, cols], cb_ref[:, cols])
        conv_ref[:, cols] = (e * y).astype(bf16)
        edge_ref[0, 0 * g:1 * g, cols] = u[:g]
        edge_ref[0, 1 * g:2 * g, cols] = u[tm - g:]
        edge_ref[0, 2 * g:3 * g, cols] = e[:g]
        edge_ref[0, 3 * g:4 * g, cols] = e[tm - g:]


def _in_projection(x2, norm_in, w, q_norm, k_norm, cos_t, sin_t, conv_w, conv_b, batch, seq):
    rows, d_model = x2.shape
    tm = _PROJ_ROWS
    nt = seq // tm
    full = lambda shape: pl.BlockSpec(shape, lambda i: (0,) * len(shape))
    table = pl.BlockSpec((tm, HEAD_DIM), lambda i: (i % nt, 0))
    table_t = pl.BlockSpec((HEAD_DIM, tm), lambda i: (0, i % nt))
    row_tile = lambda width: pl.BlockSpec((tm, width), lambda i: (i, 0))
    q_gain = jnp.broadcast_to(q_norm.reshape(HEAD_DIM, 1), (HEAD_DIM, HEAD_DIM))
    conv_width = _N_CONV_BLOCKS * _W_BLOCK
    out_shape = (
        jax.ShapeDtypeStruct((batch, N_Q_HEADS, HEAD_DIM, seq), bf16),
        jax.ShapeDtypeStruct((batch, N_KV_HEADS, seq, HEAD_DIM), bf16),
        jax.ShapeDtypeStruct((batch, N_KV_HEADS, HEAD_DIM, seq), bf16),
        jax.ShapeDtypeStruct((rows, ATTN_WIDTH), bf16),
        jax.ShapeDtypeStruct((rows, conv_width), bf16),
        jax.ShapeDtypeStruct((rows // tm, _EDGE_ROWS, conv_width), f32),
    )
    out_specs = (
        pl.BlockSpec((1, N_Q_HEADS, HEAD_DIM, tm), lambda i: (i // nt, 0, 0, i % nt)),
        pl.BlockSpec((1, N_KV_HEADS, tm, HEAD_DIM), lambda i: (i // nt, 0, i % nt, 0)),
        pl.BlockSpec((1, N_KV_HEADS, HEAD_DIM, tm), lambda i: (i // nt, 0, 0, i % nt)),
        row_tile(ATTN_WIDTH), row_tile(conv_width),
        pl.BlockSpec((1, _EDGE_ROWS, conv_width), lambda i: (i, 0, 0)),
    )
    return pl.pallas_call(
        _proj_kernel,
        grid=(rows // tm,),
        in_specs=[
            row_tile(d_model),
            full((1, d_model)),
            pl.BlockSpec(memory_space=pl.ANY),
            full((HEAD_DIM, HEAD_DIM)), full((1, HEAD_DIM)),
            table, table, table_t, table_t,
            full(conv_w.shape), full(conv_b.shape),
        ],
        out_specs=out_specs,
        out_shape=out_shape,
        scratch_shapes=[pltpu.VMEM((tm, d_model), bf16)] + _weight_stage_scratch(w),
        compiler_params=pltpu.CompilerParams(
            dimension_semantics=("arbitrary",),
            vmem_limit_bytes=_VMEM_LIMIT),
        name="in_projection",
    )(x2, norm_in, w, q_gain, k_norm, cos_t, sin_t, cos_t.T, sin_t.T, conv_w, conv_b)


def _attn_kernel(q_ref, k_ref, v_ref, sg_ref, o_ref, acc_ref, *maybe_s_ref, tq, tk, nk, nq, bounded):
    cols = GQA_GROUP * tq

    def q_tile(i):
        off = pl.multiple_of(i * tq, tq)
        return jnp.concatenate([q_ref[0, h, :, pl.ds(off, tq)] for h in range(GQA_GROUP)], axis=1)

    def key_chunk(j):
        return k_ref[0, 0, pl.ds(pl.multiple_of(j * tk, tk), tk), :]

    ones_rows = jnp.ones((_BF16_SUBLANES, tk), bf16)

    def value_chunk(j):
        off = pl.multiple_of(j * tk, tk)
        return jnp.concatenate([v_ref[0, 0, :, pl.ds(off, tk)], ones_rows], axis=0)

    def finish_tile(i, acc):
        o_t = acc[:HEAD_DIM, :] / acc[HEAD_DIM:HEAD_DIM + 1, :]
        rows = pl.ds(pl.multiple_of(i * tq, tq), tq)
        for h in range(GQA_GROUP):
            lanes = slice(h * HEAD_DIM, (h + 1) * HEAD_DIM)
            gate = sg_ref[rows, lanes].astype(f32)
            o_ref[rows, lanes] = (o_t[:, h * tq:(h + 1) * tq].T * gate).astype(bf16)

    if bounded:
        def tile_group(g, carry):
            for t in range(_ATTN_TILES_PER_ITER):
                i = g * _ATTN_TILES_PER_ITER + t
                q_t = q_tile(i)
                acc = acc_ref.at[t]
                for c in range(nk):
                    p_t = jnp.exp2(jnp.dot(key_chunk(c), q_t, preferred_element_type=f32)).astype(bf16)
                    pv_t = jnp.dot(value_chunk(c), p_t, preferred_element_type=f32)
                    acc[...] = pv_t if c == 0 else acc[...] + pv_t
                finish_tile(i, acc)
            return carry

        lax.fori_loop(0, nq // _ATTN_TILES_PER_ITER, tile_group, 0)
        return

    s_ref, = maybe_s_ref

    def scores(q_mat, j, slot):
        s_t = jnp.dot(key_chunk(j), q_mat, preferred_element_type=f32)
        s_ref[slot] = s_t
        return jnp.max(s_t, axis=0, keepdims=True)

    def step(j, slot, acc, m_prev, chunk_max, q_ahead, j_ahead):
        ahead_max = scores(q_ahead, j_ahead, 1 - slot)
        m_new = jnp.maximum(m_prev, chunk_max)
        alpha = jnp.exp2(m_prev - m_new)
        p_t = jnp.exp2(s_ref[slot] - m_new).astype(bf16)
        pv_t = jnp.dot(value_chunk(j), p_t, preferred_element_type=f32)
        acc[...] = alpha * acc[...] + pv_t
        return m_new, ahead_max

    def query_tile(i, acc, chunk_max):
        q_t = q_tile(i)
        q_next = q_tile(jnp.minimum(i + 1, nq - 1))
        acc[...] = jnp.zeros(acc.shape, f32)
        m_run = jnp.full((1, cols), -jnp.inf, f32)
        for c in range(nk):
            last = c == nk - 1
            m_run, chunk_max = step(c, c % 2, acc, m_run, chunk_max, q_next if last else q_t, 0 if last else c + 1)
        finish_tile(i, acc)
        return chunk_max

    def tile_group(g, chunk_max):
        for t in range(_ATTN_TILES_PER_ITER):
            chunk_max = query_tile(g * _ATTN_TILES_PER_ITER + t, acc_ref.at[t], chunk_max)
        return chunk_max

    lax.fori_loop(0, nq // _ATTN_TILES_PER_ITER, tile_group, scores(q_tile(0), 0, 0))


def _attention_call(q_t, k, v_t, sg, batch, seq, bounded):
    tq, tk = _ATTN_TQ, _ATTN_TK
    cols = GQA_GROUP * tq
    group_width = GQA_GROUP * HEAD_DIM
    gate_spec = pl.BlockSpec((seq, group_width), lambda b, h: (b, h))
    scratch = [pltpu.VMEM((_ATTN_TILES_PER_ITER, HEAD_DIM + _BF16_SUBLANES, cols), f32)]
    if not bounded:
        scratch.append(pltpu.VMEM((2, tk, cols), f32))
    return pl.pallas_call(
        functools.partial(_attn_kernel, tq=tq, tk=tk, nk=seq // tk, nq=seq // tq, bounded=bounded),
        grid=(batch, N_KV_HEADS),
        in_specs=[
            pl.BlockSpec((1, GQA_GROUP, HEAD_DIM, seq), lambda b, h: (b, h, 0, 0)),
            pl.BlockSpec((1, 1, seq, HEAD_DIM), lambda b, h: (b, h, 0, 0)),
            pl.BlockSpec((1, 1, HEAD_DIM, seq), lambda b, h: (b, h, 0, 0)),
            gate_spec,
        ],
        out_specs=gate_spec,
        out_shape=jax.ShapeDtypeStruct((batch * seq, ATTN_WIDTH), bf16),
        scratch_shapes=scratch,
        compiler_params=pltpu.CompilerParams(
            dimension_semantics=("arbitrary", "arbitrary"),
            vmem_limit_bytes=_VMEM_LIMIT),
        name="gqa_attention_bounded" if bounded else "gqa_attention",
    )(q_t, k, v_t, sg)


def _attention(q_t, k, v_t, sg, q_norm, k_norm, batch, seq):
    bound = (_LOG2_E * HEAD_DIM ** 0.5 * (1.0 + 2.0 ** -7)) * jnp.max(jnp.abs(q_norm)) * jnp.max(jnp.abs(k_norm))
    return lax.cond(bound <= _SCORE_BOUND,
                    functools.partial(_attention_call, batch=batch, seq=seq, bounded=True),
                    functools.partial(_attention_call, batch=batch, seq=seq, bounded=False),
                    q_t, k, v_t, sg)


def _out_kernel(attn_ref, conv_ref, eprev_ref, edge_ref, enext_ref, x_ref, w_hbm, cw_ref, cb_ref, nf_ref,
                o_ref, w_ref, stage_ref, sem, *, tm, seq):
    i = pl.program_id(0)

    @pl.when(i == 0)
    def _():
        _load_weights_as_bf16(w_hbm, w_ref, stage_ref, sem)

    g = _F32_SUBLANES
    t0 = (i * tm) % seq
    cw, cb = cw_ref[...], cb_ref[...]
    row = lambda ref, r: ref[0, r:r + 1, :]
    u_before = jnp.where(t0 == 0, 0.0, row(eprev_ref, 2 * g - 1))
    u_after = jnp.where(t0 + tm == seq, 0.0, row(enext_ref, 0))
    first = row(edge_ref, 2 * g) * _conv_taps(u_before, row(edge_ref, 0), row(edge_ref, 1), cw, cb)
    last = row(edge_ref, 4 * g - 1) * _conv_taps(row(edge_ref, 2 * g - 2), row(edge_ref, 2 * g - 1), u_after, cw, cb)
    p = _BF16_SUBLANES
    sub = lax.broadcasted_iota(jnp.int32, (p, conv_ref.shape[1]), 0)
    top = jnp.where(sub == 0, first, conv_ref[:p, :].astype(f32)).astype(bf16)
    bottom = jnp.where(sub == p - 1, last, conv_ref[tm - p:, :].astype(f32)).astype(bf16)
    conv = jnp.concatenate([top, conv_ref[p:tm - p, :], bottom], axis=0)

    acc = jnp.dot(attn_ref[...], w_ref[:ATTN_WIDTH, :], preferred_element_type=f32)
    acc = acc + jnp.dot(conv, w_ref[ATTN_WIDTH:, :], preferred_element_type=f32)
    h = x_ref[...] + acc
    ms = jnp.mean(h * h, axis=-1, keepdims=True)
    o_ref[...] = h * lax.rsqrt(ms + EPS) * nf_ref[...]


def _out_projection(attn, conv, edge, x2, w, conv_w, conv_b, norm_final, seq):
    rows, d_model = x2.shape
    tm = _OUT_ROWS
    n_tiles = rows // tm
    assert edge.shape[0] == n_tiles, "edge rows are saved per projection tile"
    conv_width = conv.shape[1]
    row_tile = lambda width: pl.BlockSpec((tm, width), lambda i: (i, 0))
    full = lambda shape: pl.BlockSpec(shape, lambda i: (0,) * len(shape))
    edge_of = lambda shift: pl.BlockSpec((1, _EDGE_ROWS, conv_width),
                                         lambda i: (jnp.clip(i + shift, 0, n_tiles - 1), 0, 0))
    return pl.pallas_call(
        functools.partial(_out_kernel, tm=tm, seq=seq),
        grid=(n_tiles,),
        in_specs=[
            row_tile(ATTN_WIDTH),
            row_tile(conv_width),
            edge_of(-1), edge_of(0), edge_of(1),
            row_tile(d_model),
            pl.BlockSpec(memory_space=pl.ANY),
            full(conv_w.shape), full(conv_b.shape), full(norm_final.shape),
        ],
        out_specs=row_tile(d_model),
        out_shape=jax.ShapeDtypeStruct((rows, d_model), f32),
        scratch_shapes=_weight_stage_scratch(w),
        compiler_params=pltpu.CompilerParams(
            dimension_semantics=("arbitrary",),
            vmem_limit_bytes=_VMEM_LIMIT),
        name="out_projection",
    )(attn, conv, edge, edge, edge, x2, w, conv_w, conv_b, norm_final)


def kernel(x, norm_in, w_in, q_norm, k_norm, conv_w, conv_b, w_out, norm_final):
    batch, seq, d_model = x.shape
    assert norm_in.shape[0] == 1, "single-layer block"
    assert _PROJ_ROWS == _OUT_ROWS, "conv edge rows are exchanged per row tile"
    assert seq % _PROJ_ROWS == 0 and seq % _ATTN_TK == 0 and seq % (_ATTN_TQ * _ATTN_TILES_PER_ITER) == 0
    x2 = x.reshape(batch * seq, d_model)
    cos_t, sin_t = _rope_tables(seq)
    q, k, v, sg, conv, edge = _in_projection(
        x2, norm_in, w_in[0], q_norm, k_norm, cos_t, sin_t, conv_w[0], conv_b, batch, seq)
    attn = _attention(q, k, v, sg, q_norm, k_norm, batch, seq)
    out = _out_projection(attn, conv, edge, x2, w_out[0], conv_w[0], conv_b,
                          norm_final.reshape(1, d_model), seq)
    return out.reshape(batch, seq, d_model)
```

```python
import functools

import jax
import jax.numpy as jnp
from jax import lax
from jax.experimental import pallas as pl
from jax.experimental.pallas import tpu as pltpu

HEAD_DIM = 128
N_Q_HEADS = 8
N_KV_HEADS = 2
GQA_GROUP = N_Q_HEADS // N_KV_HEADS
ATTN_WIDTH = N_Q_HEADS * HEAD_DIM
CONV_K = 3
GRID_W = 64
ROPE_THETA = 10000.0
ROPE_AXIS_DIM = HEAD_DIM // 2
EPS = 1e-6
_LOG2_E = 1.4426950408889634
_SCORE_BOUND = 64.0

_V7X_VMEM_BYTES = 64 * 2**20
_VMEM_LIMIT = _V7X_VMEM_BYTES - 8 * 2**20
_BF16_SUBLANES = 16
_F32_SUBLANES = 8
_EDGE_ROWS = 4 * _F32_SUBLANES
_W_BLOCK = 256
_W_STAGE_BYTES = 2 * 2**20
_W_STAGE_SLOTS = 4

_PROJ_ROWS = 512
_ATTN_TQ = 128
_ATTN_TK = 1024
_ATTN_TILES_PER_ITER = 4
_OUT_ROWS = 512

f32 = jnp.float32
bf16 = jnp.bfloat16


def _silu(x):
    return x / (1.0 + jnp.exp(-x))


def _head_norm(x, g):
    ms = jnp.mean(x * x, axis=-1, keepdims=True)
    return x * lax.rsqrt(ms + EPS) * g


def _rope(x, c, s):
    lane = lax.broadcasted_iota(jnp.int32, x.shape, 1)
    first_half = (lane & (ROPE_AXIS_DIM // 2)) == 0
    swapped = jnp.where(first_half,
                        pltpu.roll(x, HEAD_DIM - ROPE_AXIS_DIM // 2, 1),
                        pltpu.roll(x, ROPE_AXIS_DIM // 2, 1))
    return x * c + swapped * s


def _rope_tables(seq_len):
    rows = seq_len // GRID_W
    row = jnp.repeat(jnp.arange(rows, dtype=f32), GRID_W)
    col = jnp.tile(jnp.arange(GRID_W, dtype=f32), rows)
    inv_freq = ROPE_THETA ** (-jnp.arange(0, ROPE_AXIS_DIM, 2, dtype=f32) / ROPE_AXIS_DIM)
    ang_r = row[:, None] * inv_freq[None, :]
    ang_c = col[:, None] * inv_freq[None, :]
    cr, sr, cc, sc = jnp.cos(ang_r), jnp.sin(ang_r), jnp.cos(ang_c), jnp.sin(ang_c)
    cos_t = jnp.concatenate([cr, cr, cc, cc], axis=-1)
    sin_t = jnp.concatenate([-sr, sr, -sc, sc], axis=-1)
    return cos_t, sin_t


_Q_BLOCK0, _K_BLOCK, _V_BLOCK, _G_BLOCK0 = 0, 4, 5, 6
_CB_BLOCK0, _CC_BLOCK0, _CX_BLOCK0, _GC_BLOCK0 = 10, 14, 18, 22
_N_CONV_BLOCKS = 4


def _rope_t(x_t, cos_tt, sin_tt):
    q = ROPE_AXIS_DIM // 2
    swapped = jnp.concatenate([x_t[q:2 * q], x_t[0:q], x_t[3 * q:4 * q], x_t[2 * q:3 * q]], axis=0)
    return x_t * cos_tt + swapped * sin_tt


def _weight_stage_scratch(w):
    rows = 1 << ((_W_STAGE_BYTES // (4 * w.shape[1])).bit_length() - 1)
    assert w.shape[0] % rows == 0 and rows % _BF16_SUBLANES == 0
    stage = (_W_STAGE_SLOTS, rows, w.shape[1])
    return [pltpu.VMEM(w.shape, bf16), pltpu.VMEM(stage, f32), pltpu.SemaphoreType.DMA((_W_STAGE_SLOTS,))]


def _weight_stream(chunk_src, chunk_dst, n_chunks, stage_ref, sem):
    slots = stage_ref.shape[0]

    def copy(c):
        return pltpu.make_async_copy(chunk_src(c), stage_ref.at[c % slots], sem.at[c % slots])

    for c in range(min(slots - 1, n_chunks)):
        copy(c).start()

    def fetch(c):
        if c + slots - 1 < n_chunks:
            copy(c + slots - 1).start()
        copy(c).wait()
        chunk_dst(c)[...] = stage_ref[c % slots].astype(bf16)

    return fetch


def _load_weights_as_bf16(w_hbm, w_ref, stage_ref, sem):
    chunk = stage_ref.shape[1]
    n_chunks = w_ref.shape[0] // chunk
    fetch = _weight_stream(lambda c: w_hbm.at[pl.ds(c * chunk, chunk), :],
                           lambda c: w_ref.at[c * chunk:(c + 1) * chunk, :],
                           n_chunks, stage_ref, sem)
    for c in range(n_chunks):
        fetch(c)


def _conv_taps(u_prev, u_mid, u_next, cw, cb):
    y = cb + u_prev * cw[0:1, :]
    y = y + u_mid * cw[1:2, :]
    return y + u_next * cw[2:3, :]


def _proj_kernel(x_ref, nin_ref, w_hbm, qg_ref, kn_ref, cos_ref, sin_ref, cost_ref, sint_ref, cw_ref, cb_ref,
                 q_ref, k_ref, v_ref, sg_ref, conv_ref, edge_ref, hn_ref, w_ref, stage_ref, sem):
    tm = hn_ref.shape[0]

    @pl.when(pl.program_id(0) == 0)
    def _():
        _load_weights_as_bf16(w_hbm, w_ref, stage_ref, sem)

    def row_tile():
        x = x_ref[...]
        ms = jnp.mean(x * x, axis=-1, keepdims=True)
        hn_ref[...] = (x * lax.rsqrt(ms + EPS) * nin_ref[...]).astype(bf16)

        def proj(block):
            w = w_ref[:, block * _W_BLOCK:(block + 1) * _W_BLOCK]
            return jnp.dot(hn_ref[...], w, preferred_element_type=f32)

        def head(mat, hh):
            return mat[:, hh * HEAD_DIM:(hh + 1) * HEAD_DIM]

        kk = proj(_K_BLOCK)
        for hh in range(N_KV_HEADS):
            k_ref[0, hh] = _rope(_head_norm(head(kk, hh), kn_ref[...]), cos_ref[...], sin_ref[...]).astype(bf16)

        cos_tt, sin_tt = cost_ref[...], sint_ref[...]
        gain = jnp.tile(qg_ref[...], (1, tm // HEAD_DIM))
        scale = HEAD_DIM ** -0.5 * _LOG2_E
        for pair in range(N_Q_HEADS // 2):
            qq = proj(_Q_BLOCK0 + pair)
            for hh in range(2):
                q_t = head(qq, hh).T
                ms = jnp.mean(q_t * q_t, axis=0, keepdims=True)
                r = lax.rsqrt(ms + EPS) * scale
                q_ref[0, 2 * pair + hh] = (_rope_t(q_t * gain, cos_tt, sin_tt) * r).astype(bf16)

        vv = proj(_V_BLOCK)
        for hh in range(N_KV_HEADS):
            v_ref[0, hh] = head(vv, hh).T.astype(bf16)

        for c in range(ATTN_WIDTH // _W_BLOCK):
            sg_ref[:, c * _W_BLOCK:(c + 1) * _W_BLOCK] = _silu(proj(_G_BLOCK0 + c)).astype(bf16)

        g = _F32_SUBLANES
        for c in range(_N_CONV_BLOCKS):
            cols = slice(c * _W_BLOCK, (c + 1) * _W_BLOCK)
            u = proj(_CC_BLOCK0 + c) * proj(_CX_BLOCK0 + c)
            e = proj(_CB_BLOCK0 + c) * _silu(proj(_GC_BLOCK0 + c))
            y = _conv_taps(pltpu.roll(u, 1, 0), u, pltpu.roll(u, tm - 1, 0), cw_ref[:, cols], cb_ref[:, cols])
            conv_ref[:, cols] = (e * y).astype(bf16)
            edge_ref[0, 0 * g:1 * g, cols] = u[:g]
            edge_ref[0, 1 * g:2 * g, cols] = u[tm - g:]
            edge_ref[0, 2 * g:3 * g, cols] = e[:g]
            edge_ref[0, 3 * g:4 * g, cols] = e[tm - g:]

    row_tile()


def _in_projection(x2, norm_in, w, q_norm, k_norm, cos_t, sin_t, conv_w, conv_b, batch, seq):
    rows, d_model = x2.shape
    tm = _PROJ_ROWS
    nt = seq // tm
    full = lambda shape: pl.BlockSpec(shape, lambda i: (0,) * len(shape))
    table = pl.BlockSpec((tm, HEAD_DIM), lambda i: (i % nt, 0))
    table_t = pl.BlockSpec((HEAD_DIM, tm), lambda i: (0, i % nt))
    row_tile = lambda width: pl.BlockSpec((tm, width), lambda i: (i, 0))
    q_gain = jnp.broadcast_to(q_norm.reshape(HEAD_DIM, 1), (HEAD_DIM, HEAD_DIM))
    conv_width = _N_CONV_BLOCKS * _W_BLOCK
    out_shape = (
        jax.ShapeDtypeStruct((batch, N_Q_HEADS, HEAD_DIM, seq), bf16),
        jax.ShapeDtypeStruct((batch, N_KV_HEADS, seq, HEAD_DIM), bf16),
        jax.ShapeDtypeStruct((batch, N_KV_HEADS, HEAD_DIM, seq), bf16),
        jax.ShapeDtypeStruct((rows, ATTN_WIDTH), bf16),
        jax.ShapeDtypeStruct((rows, conv_width), bf16),
        jax.ShapeDtypeStruct((rows // tm, _EDGE_ROWS, conv_width), f32),
    )
    out_specs = (
        pl.BlockSpec((1, N_Q_HEADS, HEAD_DIM, tm), lambda i: (i // nt, 0, 0, i % nt)),
        pl.BlockSpec((1, N_KV_HEADS, tm, HEAD_DIM), lambda i: (i // nt, 0, i % nt, 0)),
        pl.BlockSpec((1, N_KV_HEADS, HEAD_DIM, tm), lambda i: (i // nt, 0, 0, i % nt)),
        row_tile(ATTN_WIDTH), row_tile(conv_width),
        pl.BlockSpec((1, _EDGE_ROWS, conv_width), lambda i: (i, 0, 0)),
    )
    return pl.pallas_call(
        _proj_kernel,
        grid=(rows // tm,),
        in_specs=[
            row_tile(d_model),
            full((1, d_model)),
            pl.BlockSpec(memory_space=pl.ANY),
            full((HEAD_DIM, HEAD_DIM)), full((1, HEAD_DIM)),
            table, table, table_t, table_t,
            full(conv_w.shape), full(conv_b.shape),
        ],
        out_specs=out_specs,
        out_shape=out_shape,
        scratch_shapes=[pltpu.VMEM((tm, d_model), bf16)] + _weight_stage_scratch(w),
        compiler_params=pltpu.CompilerParams(
            dimension_semantics=("arbitrary",),
            vmem_limit_bytes=_VMEM_LIMIT),
        name="in_projection",
    )(x2, norm_in, w, q_gain, k_norm, cos_t, sin_t, cos_t.T, sin_t.T, conv_w, conv_b)


def _attn_kernel(q_ref, k_ref, v_ref, sg_ref, o_ref, acc_ref, *maybe_s_ref, tq, tk, nk, nq, bounded):
    cols = GQA_GROUP * tq

    def q_tile(i):
        off = pl.multiple_of(i * tq, tq)
        return jnp.concatenate([q_ref[0, h, :, pl.ds(off, tq)] for h in range(GQA_GROUP)], axis=1)

    def key_chunk(j):
        return k_ref[0, 0, pl.ds(pl.multiple_of(j * tk, tk), tk), :]

    ones_rows = jnp.ones((_BF16_SUBLANES, tk), bf16)

    def value_chunk(j):
        off = pl.multiple_of(j * tk, tk)
        return jnp.concatenate([v_ref[0, 0, :, pl.ds(off, tk)], ones_rows], axis=0)

    def finish_tile(i, acc):
        o_t = acc[:HEAD_DIM, :] / acc[HEAD_DIM:HEAD_DIM + 1, :]
        rows = pl.ds(pl.multiple_of(i * tq, tq), tq)
        for h in range(GQA_GROUP):
            lanes = slice(h * HEAD_DIM, (h + 1) * HEAD_DIM)
            gate = sg_ref[rows, lanes].astype(f32)
            o_ref[rows, lanes] = (o_t[:, h * tq:(h + 1) * tq].T * gate).astype(bf16)

    if bounded:
        def tile_group(g, carry):
            for t in range(_ATTN_TILES_PER_ITER):
                i = g * _ATTN_TILES_PER_ITER + t
                q_t = q_tile(i)
                acc = acc_ref.at[t]
                denom = jnp.zeros((1, cols), f32)
                for c in range(nk):
                    p_t = jnp.exp2(jnp.dot(key_chunk(c), q_t, preferred_element_type=f32))
                    denom = denom + jnp.sum(p_t, axis=0, keepdims=True)
                    off = pl.multiple_of(c * tk, tk)
                    pv_t = jnp.dot(v_ref[0, 0, :, pl.ds(off, tk)], p_t.astype(bf16), preferred_element_type=f32)
                    acc[:HEAD_DIM, :] = pv_t if c == 0 else acc[:HEAD_DIM, :] + pv_t
                acc[HEAD_DIM:HEAD_DIM + 1, :] = denom
                finish_tile(i, acc)
            return carry

        lax.fori_loop(0, nq // _ATTN_TILES_PER_ITER, tile_group, 0)
        return

    s_ref, = maybe_s_ref

    def scores(q_mat, j, slot):
        s_t = jnp.dot(key_chunk(j), q_mat, preferred_element_type=f32)
        s_ref[slot] = s_t
        return jnp.max(s_t, axis=0, keepdims=True)

    def step(j, slot, acc, m_prev, chunk_max, q_ahead, j_ahead):
        ahead_max = scores(q_ahead, j_ahead, 1 - slot)
        m_new = jnp.maximum(m_prev, chunk_max)
        alpha = jnp.exp2(m_prev - m_new)
        p_t = jnp.exp2(s_ref[slot] - m_new).astype(bf16)
        pv_t = jnp.dot(value_chunk(j), p_t, preferred_element_type=f32)
        acc[...] = alpha * acc[...] + pv_t
        return m_new, ahead_max

    def query_tile(i, acc, chunk_max):
        q_t = q_tile(i)
        q_next = q_tile(jnp.minimum(i + 1, nq - 1))
        acc[...] = jnp.zeros(acc.shape, f32)
        m_run = jnp.full((1, cols), -jnp.inf, f32)
        for c in range(nk):
            last = c == nk - 1
            m_run, chunk_max = step(c, c % 2, acc, m_run, chunk_max, q_next if last else q_t, 0 if last else c + 1)
        finish_tile(i, acc)
        return chunk_max

    def tile_group(g, chunk_max):
        for t in range(_ATTN_TILES_PER_ITER):
            chunk_max = query_tile(g * _ATTN_TILES_PER_ITER + t, acc_ref.at[t], chunk_max)
        return chunk_max

    lax.fori_loop(0, nq // _ATTN_TILES_PER_ITER, tile_group, scores(q_tile(0), 0, 0))


def _attention_call(q_t, k, v_t, sg, batch, seq, bounded):
    tq, tk = _ATTN_TQ, _ATTN_TK
    cols = GQA_GROUP * tq
    group_width = GQA_GROUP * HEAD_DIM
    gate_spec = pl.BlockSpec((seq, group_width), lambda b, h: (b, h))
    scratch = [pltpu.VMEM((_ATTN_TILES_PER_ITER, HEAD_DIM + _BF16_SUBLANES, cols), f32)]
    if not bounded:
        scratch.append(pltpu.VMEM((2, tk, cols), f32))
    return pl.pallas_call(
        functools.partial(_attn_kernel, tq=tq, tk=tk, nk=seq // tk, nq=seq // tq, bounded=bounded),
        grid=(batch, N_KV_HEADS),
        in_specs=[
            pl.BlockSpec((1, GQA_GROUP, HEAD_DIM, seq), lambda b, h: (b, h, 0, 0)),
            pl.BlockSpec((1, 1, seq, HEAD_DIM), lambda b, h: (b, h, 0, 0)),
            pl.BlockSpec((1, 1, HEAD_DIM, seq), lambda b, h: (b, h, 0, 0)),
            gate_spec,
        ],
        out_specs=gate_spec,
        out_shape=jax.ShapeDtypeStruct((batch * seq, ATTN_WIDTH), bf16),
        scratch_shapes=scratch,
        compiler_params=pltpu.CompilerParams(
            dimension_semantics=("arbitrary", "arbitrary"),
            vmem_limit_bytes=_VMEM_LIMIT),
        name="gqa_attention_bounded" if bounded else "gqa_attention",
    )(q_t, k, v_t, sg)


def _attention(q_t, k, v_t, sg, q_norm, k_norm, batch, seq):
    bound = (_LOG2_E * HEAD_DIM ** 0.5 * (1.0 + 2.0 ** -7)) * jnp.max(jnp.abs(q_norm)) * jnp.max(jnp.abs(k_norm))
    return lax.cond(bound <= _SCORE_BOUND,
                    functools.partial(_attention_call, batch=batch, seq=seq, bounded=True),
                    functools.partial(_attention_call, batch=batch, seq=seq, bounded=False),
                    q_t, k, v_t, sg)


def _out_kernel(attn_ref, conv_ref, eprev_ref, edge_ref, enext_ref, x_ref, w_hbm, cw_ref, cb_ref, nf_ref,
                o_ref, w_ref, stage_ref, sem, *, tm, seq):
    i = pl.program_id(0)

    @pl.when(i == 0)
    def _():
        _load_weights_as_bf16(w_hbm, w_ref, stage_ref, sem)

    g = _F32_SUBLANES
    t0 = (i * tm) % seq
    cw, cb = cw_ref[...], cb_ref[...]
    row = lambda ref, r: ref[0, r:r + 1, :]
    u_before = jnp.where(t0 == 0, 0.0, row(eprev_ref, 2 * g - 1))
    u_after = jnp.where(t0 + tm == seq, 0.0, row(enext_ref, 0))
    first = row(edge_ref, 2 * g) * _conv_taps(u_before, row(edge_ref, 0), row(edge_ref, 1), cw, cb)
    last = row(edge_ref, 4 * g - 1) * _conv_taps(row(edge_ref, 2 * g - 2), row(edge_ref, 2 * g - 1), u_after, cw, cb)
    p = _BF16_SUBLANES
    sub = lax.broadcasted_iota(jnp.int32, (p, conv_ref.shape[1]), 0)
    top = jnp.where(sub == 0, first, conv_ref[:p, :].astype(f32)).astype(bf16)
    bottom = jnp.where(sub == p - 1, last, conv_ref[tm - p:, :].astype(f32)).astype(bf16)
    conv = jnp.concatenate([top, conv_ref[p:tm - p, :], bottom], axis=0)

    acc = jnp.dot(attn_ref[...], w_ref[:ATTN_WIDTH, :], preferred_element_type=f32)
    acc = acc + jnp.dot(conv, w_ref[ATTN_WIDTH:, :], preferred_element_type=f32)
    h = x_ref[...] + acc
    ms = jnp.mean(h * h, axis=-1, keepdims=True)
    o_ref[...] = h * lax.rsqrt(ms + EPS) * nf_ref[...]


def _out_projection(attn, conv, edge, x2, w, conv_w, conv_b, norm_final, seq):
    rows, d_model = x2.shape
    tm = _OUT_ROWS
    n_tiles = rows // tm
    assert edge.shape[0] == n_tiles, "edge rows are saved per projection tile"
    conv_width = conv.shape[1]
    row_tile = lambda width: pl.BlockSpec((tm, width), lambda i: (i, 0))
    full = lambda shape: pl.BlockSpec(shape, lambda i: (0,) * len(shape))
    edge_of = lambda shift: pl.BlockSpec((1, _EDGE_ROWS, conv_width),
                                         lambda i: (jnp.clip(i + shift, 0, n_tiles - 1), 0, 0))
    return pl.pallas_call(
        functools.partial(_out_kernel, tm=tm, seq=seq),
        grid=(n_tiles,),
        in_specs=[
            row_tile(ATTN_WIDTH),
            row_tile(conv_width),
            edge_of(-1), edge_of(0), edge_of(1),
            row_tile(d_model),
            pl.BlockSpec(memory_space=pl.ANY),
            full(conv_w.shape), full(conv_b.shape), full(norm_final.shape),
        ],
        out_specs=row_tile(d_model),
        out_shape=jax.ShapeDtypeStruct((rows, d_model), f32),
        scratch_shapes=_weight_stage_scratch(w),
        compiler_params=pltpu.CompilerParams(
            dimension_semantics=("arbitrary",),
            vmem_limit_bytes=_VMEM_LIMIT),
        name="out_projection",
    )(attn, conv, edge, edge, edge, x2, w, conv_w, conv_b, norm_final)


def kernel(x, norm_in, w_in, q_norm, k_norm, conv_w, conv_b, w_out, norm_final):
    batch, seq, d_model = x.shape
    assert norm_in.shape[0] == 1, "single-layer block"
    assert _PROJ_ROWS == _OUT_ROWS, "conv edge rows are exchanged per row tile"
    assert seq % _PROJ_ROWS == 0 and seq % _ATTN_TK == 0 and seq % (_ATTN_TQ * _ATTN_TILES_PER_ITER) == 0
    x2 = x.reshape(batch * seq, d_model)
    cos_t, sin_t = _rope_tables(seq)
    q, k, v, sg, conv, edge = _in_projection(
        x2, norm_in, w_in[0], q_norm, k_norm, cos_t, sin_t, conv_w[0], conv_b, batch, seq)
    attn = _attention(q, k, v, sg, q_norm, k_norm, batch, seq)
    out = _out_projection(attn, conv, edge, x2, w_out[0], conv_w[0], conv_b,
                          norm_final.reshape(1, d_model), seq)
    return out.reshape(batch, seq, d_model)
```

```python
import functools

import jax
import jax.numpy as jnp
import numpy as np
from jax import lax
from jax.experimental import pallas as pl
from jax.experimental.pallas import tpu as pltpu

HEAD_DIM = 128
N_Q_HEADS = 8
N_KV_HEADS = 2
GQA_GROUP = N_Q_HEADS // N_KV_HEADS
ATTN_WIDTH = N_Q_HEADS * HEAD_DIM
CONV_K = 3
GRID_W = 64
ROPE_THETA = 10000.0
ROPE_AXIS_DIM = HEAD_DIM // 2
EPS = 1e-6
_LOG2_E = 1.4426950408889634
_SCORE_BOUND = 64.0

_V7X_VMEM_BYTES = 64 * 2**20
_VMEM_LIMIT = _V7X_VMEM_BYTES - 8 * 2**20
_BF16_SUBLANES = 16
_F32_SUBLANES = 8
_EDGE_ROWS = 4 * _F32_SUBLANES
_W_BLOCK = 256
_W_STAGE_BYTES = 2 * 2**20
_W_STAGE_SLOTS = 4

_PROJ_ROWS = 512
_ATTN_TQ = 128
_ATTN_TK = 1024
_ATTN_TILES_PER_ITER = 8
_OUT_ROWS = 512

f32 = jnp.float32
bf16 = jnp.bfloat16


def _silu(x):
    return x / (1.0 + jnp.exp(-x))


def _head_norm(x, g):
    ms = jnp.mean(x * x, axis=-1, keepdims=True)
    return x * lax.rsqrt(ms + EPS) * g


def _rope(x, c, s):
    lane = lax.broadcasted_iota(jnp.int32, x.shape, 1)
    first_half = (lane & (ROPE_AXIS_DIM // 2)) == 0
    swapped = jnp.where(first_half,
                        pltpu.roll(x, HEAD_DIM - ROPE_AXIS_DIM // 2, 1),
                        pltpu.roll(x, ROPE_AXIS_DIM // 2, 1))
    return x * c + swapped * s


def _rope_tables(seq_len):
    rows = seq_len // GRID_W
    row = np.repeat(np.arange(rows, dtype=np.float64), GRID_W)
    col = np.tile(np.arange(GRID_W, dtype=np.float64), rows)
    inv_freq = ROPE_THETA ** (-np.arange(0, ROPE_AXIS_DIM, 2, dtype=np.float64) / ROPE_AXIS_DIM)
    ang_r = row[:, None] * inv_freq[None, :]
    ang_c = col[:, None] * inv_freq[None, :]
    cr, sr, cc, sc = np.cos(ang_r), np.sin(ang_r), np.cos(ang_c), np.sin(ang_c)
    cos_t = np.concatenate([cr, cr, cc, cc], axis=-1).astype(np.float32)
    sin_t = np.concatenate([-sr, sr, -sc, sc], axis=-1).astype(np.float32)
    return cos_t, sin_t


_Q_BLOCK0, _K_BLOCK, _V_BLOCK, _G_BLOCK0 = 0, 4, 5, 6
_CB_BLOCK0, _CC_BLOCK0, _CX_BLOCK0, _GC_BLOCK0 = 10, 14, 18, 22
_N_CONV_BLOCKS = 4


def _rope_t(x_t, cos_tt, sin_tt):
    q = ROPE_AXIS_DIM // 2
    swapped = jnp.concatenate([x_t[q:2 * q], x_t[0:q], x_t[3 * q:4 * q], x_t[2 * q:3 * q]], axis=0)
    return x_t * cos_tt + swapped * sin_tt


def _weight_stage_scratch(w):
    rows = 1 << ((_W_STAGE_BYTES // (4 * w.shape[1])).bit_length() - 1)
    assert w.shape[0] % rows == 0 and rows % _BF16_SUBLANES == 0
    stage = (_W_STAGE_SLOTS, rows, w.shape[1])
    return [pltpu.VMEM(w.shape, bf16), pltpu.VMEM(stage, f32), pltpu.SemaphoreType.DMA((_W_STAGE_SLOTS,))]


def _weight_stream(chunk_src, chunk_dst, n_chunks, stage_ref, sem):
    slots = stage_ref.shape[0]

    def copy(c):
        return pltpu.make_async_copy(chunk_src(c), stage_ref.at[c % slots], sem.at[c % slots])

    for c in range(min(slots - 1, n_chunks)):
        copy(c).start()

    def fetch(c):
        if c + slots - 1 < n_chunks:
            copy(c + slots - 1).start()
        copy(c).wait()
        chunk_dst(c)[...] = stage_ref[c % slots].astype(bf16)

    return fetch


def _load_weights_as_bf16(w_hbm, w_ref, stage_ref, sem):
    chunk = stage_ref.shape[1]
    n_chunks = w_ref.shape[0] // chunk
    fetch = _weight_stream(lambda c: w_hbm.at[pl.ds(c * chunk, chunk), :],
                           lambda c: w_ref.at[c * chunk:(c + 1) * chunk, :],
                           n_chunks, stage_ref, sem)
    for c in range(n_chunks):
        fetch(c)


def _conv_taps(u_prev, u_mid, u_next, cw, cb):
    y = cb + u_prev * cw[0:1, :]
    y = y + u_mid * cw[1:2, :]
    return y + u_next * cw[2:3, :]


def _proj_kernel(x_ref, nin_ref, w_hbm, qg_ref, kn_ref, cos_ref, sin_ref, cost_ref, sint_ref, cw_ref, cb_ref,
                 q_ref, k_ref, v_ref, sg_ref, conv_ref, edge_ref, hn_ref, w_ref, stage_ref, sem):
    tm = hn_ref.shape[0]

    @pl.when(pl.program_id(0) == 0)
    def _():
        _load_weights_as_bf16(w_hbm, w_ref, stage_ref, sem)

    def row_tile():
        x = x_ref[...]
        ms = jnp.mean(x * x, axis=-1, keepdims=True)
        hn_ref[...] = (x * lax.rsqrt(ms + EPS) * nin_ref[...]).astype(bf16)

        def proj(block):
            w = w_ref[:, block * _W_BLOCK:(block + 1) * _W_BLOCK]
            return jnp.dot(hn_ref[...], w, preferred_element_type=f32)

        def head(mat, hh):
            return mat[:, hh * HEAD_DIM:(hh + 1) * HEAD_DIM]

        kk = proj(_K_BLOCK)
        for hh in range(N_KV_HEADS):
            k_ref[0, hh] = _rope(_head_norm(head(kk, hh), kn_ref[...]), cos_ref[...], sin_ref[...]).astype(bf16)

        cos_tt, sin_tt = cost_ref[...], sint_ref[...]
        gain = jnp.tile(jnp.broadcast_to(qg_ref[...], (HEAD_DIM, HEAD_DIM)).T, (1, tm // HEAD_DIM))
        scale = HEAD_DIM ** -0.5 * _LOG2_E
        for pair in range(N_Q_HEADS // 2):
            qq = proj(_Q_BLOCK0 + pair)
            for hh in range(2):
                q_t = head(qq, hh).T
                ms = jnp.mean(q_t * q_t, axis=0, keepdims=True)
                r = lax.rsqrt(ms + EPS) * scale
                q_ref[0, 2 * pair + hh] = (_rope_t(q_t * gain, cos_tt, sin_tt) * r).astype(bf16)

        vv = proj(_V_BLOCK)
        for hh in range(N_KV_HEADS):
            v_ref[0, hh] = head(vv, hh).T.astype(bf16)

        for c in range(ATTN_WIDTH // _W_BLOCK):
            sg_ref[:, c * _W_BLOCK:(c + 1) * _W_BLOCK] = _silu(proj(_G_BLOCK0 + c)).astype(bf16)

        g = _F32_SUBLANES
        for c in range(_N_CONV_BLOCKS):
            cols = slice(c * _W_BLOCK, (c + 1) * _W_BLOCK)
            u = proj(_CC_BLOCK0 + c) * proj(_CX_BLOCK0 + c)
            e = proj(_CB_BLOCK0 + c) * _silu(proj(_GC_BLOCK0 + c))
            y = _conv_taps(pltpu.roll(u, 1, 0), u, pltpu.roll(u, tm - 1, 0), cw_ref[:, cols], cb_ref[:, cols])
            conv_ref[:, cols] = (e * y).astype(bf16)
            edge_ref[0, 0 * g:1 * g, cols] = u[:g]
            edge_ref[0, 1 * g:2 * g, cols] = u[tm - g:]
            edge_ref[0, 2 * g:3 * g, cols] = e[:g]
            edge_ref[0, 3 * g:4 * g, cols] = e[tm - g:]

    row_tile()


def _in_projection(x2, norm_in, w, q_norm, k_norm, cos_t, sin_t, conv_w, conv_b, batch, seq):
    rows, d_model = x2.shape
    tm = _PROJ_ROWS
    nt = seq // tm
    full = lambda shape: pl.BlockSpec(shape, lambda i: (0,) * len(shape))
    table = pl.BlockSpec((tm, HEAD_DIM), lambda i: (i % nt, 0))
    table_t = pl.BlockSpec((HEAD_DIM, tm), lambda i: (0, i % nt))
    row_tile = lambda width: pl.BlockSpec((tm, width), lambda i: (i, 0))
    conv_width = _N_CONV_BLOCKS * _W_BLOCK
    out_shape = (
        jax.ShapeDtypeStruct((batch, N_Q_HEADS, HEAD_DIM, seq), bf16),
        jax.ShapeDtypeStruct((batch, N_KV_HEADS, seq, HEAD_DIM), bf16),
        jax.ShapeDtypeStruct((batch, N_KV_HEADS, HEAD_DIM, seq), bf16),
        jax.ShapeDtypeStruct((rows, ATTN_WIDTH), bf16),
        jax.ShapeDtypeStruct((rows, conv_width), bf16),
        jax.ShapeDtypeStruct((rows // tm, _EDGE_ROWS, conv_width), f32),
    )
    out_specs = (
        pl.BlockSpec((1, N_Q_HEADS, HEAD_DIM, tm), lambda i: (i // nt, 0, 0, i % nt)),
        pl.BlockSpec((1, N_KV_HEADS, tm, HEAD_DIM), lambda i: (i // nt, 0, i % nt, 0)),
        pl.BlockSpec((1, N_KV_HEADS, HEAD_DIM, tm), lambda i: (i // nt, 0, 0, i % nt)),
        row_tile(ATTN_WIDTH), row_tile(conv_width),
        pl.BlockSpec((1, _EDGE_ROWS, conv_width), lambda i: (i, 0, 0)),
    )
    return pl.pallas_call(
        _proj_kernel,
        grid=(rows // tm,),
        in_specs=[
            row_tile(d_model),
            full((1, d_model)),
            pl.BlockSpec(memory_space=pl.ANY),
            full((1, HEAD_DIM)), full((1, HEAD_DIM)),
            table, table, table_t, table_t,
            full(conv_w.shape), full(conv_b.shape),
        ],
        out_specs=out_specs,
        out_shape=out_shape,
        scratch_shapes=[pltpu.VMEM((tm, d_model), bf16)] + _weight_stage_scratch(w),
        compiler_params=pltpu.CompilerParams(
            dimension_semantics=("arbitrary",),
            vmem_limit_bytes=_VMEM_LIMIT),
        name="in_projection",
    )(x2, norm_in, w, q_norm, k_norm, cos_t, sin_t, np.ascontiguousarray(cos_t.T), np.ascontiguousarray(sin_t.T),
      conv_w, conv_b)


def _attn_kernel(q_ref, k_ref, v_ref, sg_ref, o_ref, acc_ref, *maybe_s_ref, tq, tk, nk, nq, bounded):
    cols = GQA_GROUP * tq

    def q_tile(i):
        off = pl.multiple_of(i * tq, tq)
        return jnp.concatenate([q_ref[0, h, :, pl.ds(off, tq)] for h in range(GQA_GROUP)], axis=1)

    def key_chunk(j):
        return k_ref[0, 0, pl.ds(pl.multiple_of(j * tk, tk), tk), :]

    ones_rows = jnp.ones((_BF16_SUBLANES, tk), bf16)

    def value_chunk(j):
        off = pl.multiple_of(j * tk, tk)
        return jnp.concatenate([v_ref[0, 0, :, pl.ds(off, tk)], ones_rows], axis=0)

    def finish_tile(i, acc):
        o_t = acc[:HEAD_DIM, :] / acc[HEAD_DIM:HEAD_DIM + 1, :]
        rows = pl.ds(pl.multiple_of(i * tq, tq), tq)
        for h in range(GQA_GROUP):
            lanes = slice(h * HEAD_DIM, (h + 1) * HEAD_DIM)
            gate = sg_ref[rows, lanes].astype(f32)
            o_ref[rows, lanes] = (o_t[:, h * tq:(h + 1) * tq].T * gate).astype(bf16)

    if bounded:
        def tile_group(g, carry):
            for t in range(_ATTN_TILES_PER_ITER):
                i = g * _ATTN_TILES_PER_ITER + t
                q_t = q_tile(i)
                acc = acc_ref.at[t]
                for c in range(nk):
                    p_t = jnp.exp2(jnp.dot(key_chunk(c), q_t, preferred_element_type=f32)).astype(bf16)
                    pv_t = jnp.dot(value_chunk(c), p_t, preferred_element_type=f32)
                    acc[...] = pv_t if c == 0 else acc[...] + pv_t
                finish_tile(i, acc)
            return carry

        lax.fori_loop(0, nq // _ATTN_TILES_PER_ITER, tile_group, 0)
        return

    s_ref, = maybe_s_ref

    def scores(q_mat, j, slot):
        s_t = jnp.dot(key_chunk(j), q_mat, preferred_element_type=f32)
        s_ref[slot] = s_t
        return jnp.max(s_t, axis=0, keepdims=True)

    def step(j, slot, acc, m_prev, chunk_max, q_ahead, j_ahead):
        ahead_max = scores(q_ahead, j_ahead, 1 - slot)
        m_new = jnp.maximum(m_prev, chunk_max)
        alpha = jnp.exp2(m_prev - m_new)
        p_t = jnp.exp2(s_ref[slot] - m_new).astype(bf16)
        pv_t = jnp.dot(value_chunk(j), p_t, preferred_element_type=f32)
        acc[...] = alpha * acc[...] + pv_t
        return m_new, ahead_max

    def query_tile(i, acc, chunk_max):
        q_t = q_tile(i)
        q_next = q_tile(jnp.minimum(i + 1, nq - 1))
        acc[...] = jnp.zeros(acc.shape, f32)
        m_run = jnp.full((1, cols), -jnp.inf, f32)
        for c in range(nk):
            last = c == nk - 1
            m_run, chunk_max = step(c, c % 2, acc, m_run, chunk_max, q_next if last else q_t, 0 if last else c + 1)
        finish_tile(i, acc)
        return chunk_max

    def tile_group(g, chunk_max):
        for t in range(_ATTN_TILES_PER_ITER):
            chunk_max = query_tile(g * _ATTN_TILES_PER_ITER + t, acc_ref.at[t], chunk_max)
        return chunk_max

    lax.fori_loop(0, nq // _ATTN_TILES_PER_ITER, tile_group, scores(q_tile(0), 0, 0))


def _attention_call(q_t, k, v_t, sg, batch, seq, bounded):
    tq, tk = _ATTN_TQ, _ATTN_TK
    cols = GQA_GROUP * tq
    group_width = GQA_GROUP * HEAD_DIM
    gate_spec = pl.BlockSpec((seq, group_width), lambda b, h: (b, h))
    scratch = [pltpu.VMEM((_ATTN_TILES_PER_ITER, HEAD_DIM + _BF16_SUBLANES, cols), f32)]
    if not bounded:
        scratch.append(pltpu.VMEM((2, tk, cols), f32))
    return pl.pallas_call(
        functools.partial(_attn_kernel, tq=tq, tk=tk, nk=seq // tk, nq=seq // tq, bounded=bounded),
        grid=(batch, N_KV_HEADS),
        in_specs=[
            pl.BlockSpec((1, GQA_GROUP, HEAD_DIM, seq), lambda b, h: (b, h, 0, 0)),
            pl.BlockSpec((1, 1, seq, HEAD_DIM), lambda b, h: (b, h, 0, 0)),
            pl.BlockSpec((1, 1, HEAD_DIM, seq), lambda b, h: (b, h, 0, 0)),
            gate_spec,
        ],
        out_specs=gate_spec,
        out_shape=jax.ShapeDtypeStruct((batch * seq, ATTN_WIDTH), bf16),
        scratch_shapes=scratch,
        compiler_params=pltpu.CompilerParams(
            dimension_semantics=("arbitrary", "arbitrary"),
            vmem_limit_bytes=_VMEM_LIMIT),
        name="gqa_attention_bounded" if bounded else "gqa_attention",
    )(q_t, k, v_t, sg)


def _attention(q_t, k, v_t, sg, q_norm, k_norm, batch, seq):
    g_max = jnp.max(jnp.abs(jnp.concatenate([q_norm, k_norm], axis=0)))
    bound = (_LOG2_E * HEAD_DIM ** 0.5 * (1.0 + 2.0 ** -7)) * g_max * g_max
    return lax.cond(bound <= _SCORE_BOUND,
                    functools.partial(_attention_call, batch=batch, seq=seq, bounded=True),
                    functools.partial(_attention_call, batch=batch, seq=seq, bounded=False),
                    q_t, k, v_t, sg)


def _out_kernel(attn_ref, conv_ref, eprev_ref, edge_ref, enext_ref, x_ref, w_hbm, cw_ref, cb_ref, nf_ref,
                o_ref, w_ref, stage_ref, sem, *, tm, seq):
    i = pl.program_id(0)

    @pl.when(i == 0)
    def _():
        _load_weights_as_bf16(w_hbm, w_ref, stage_ref, sem)

    g = _F32_SUBLANES
    t0 = (i * tm) % seq
    cw, cb = cw_ref[...], cb_ref[...]
    row = lambda ref, r: ref[0, r:r + 1, :]
    u_before = jnp.where(t0 == 0, 0.0, row(eprev_ref, 2 * g - 1))
    u_after = jnp.where(t0 + tm == seq, 0.0, row(enext_ref, 0))
    first = row(edge_ref, 2 * g) * _conv_taps(u_before, row(edge_ref, 0), row(edge_ref, 1), cw, cb)
    last = row(edge_ref, 4 * g - 1) * _conv_taps(row(edge_ref, 2 * g - 2), row(edge_ref, 2 * g - 1), u_after, cw, cb)
    p = _BF16_SUBLANES
    sub = lax.broadcasted_iota(jnp.int32, (p, conv_ref.shape[1]), 0)
    top = jnp.where(sub == 0, first, conv_ref[:p, :].astype(f32)).astype(bf16)
    bottom = jnp.where(sub == p - 1, last, conv_ref[tm - p:, :].astype(f32)).astype(bf16)
    conv = jnp.concatenate([top, conv_ref[p:tm - p, :], bottom], axis=0)

    acc = jnp.dot(attn_ref[...], w_ref[:ATTN_WIDTH, :], preferred_element_type=f32)
    acc = acc + jnp.dot(conv, w_ref[ATTN_WIDTH:, :], preferred_element_type=f32)
    h = x_ref[...] + acc
    ms = jnp.mean(h * h, axis=-1, keepdims=True)
    o_ref[...] = h * lax.rsqrt(ms + EPS) * nf_ref[...]


def _out_projection(attn, conv, edge, x2, w, conv_w, conv_b, norm_final, seq):
    rows, d_model = x2.shape
    tm = _OUT_ROWS
    n_tiles = rows // tm
    assert edge.shape[0] == n_tiles, "edge rows are saved per projection tile"
    conv_width = conv.shape[1]
    row_tile = lambda width: pl.BlockSpec((tm, width), lambda i: (i, 0))
    full = lambda shape: pl.BlockSpec(shape, lambda i: (0,) * len(shape))
    edge_of = lambda shift: pl.BlockSpec((1, _EDGE_ROWS, conv_width),
                                         lambda i: (jnp.clip(i + shift, 0, n_tiles - 1), 0, 0))
    return pl.pallas_call(
        functools.partial(_out_kernel, tm=tm, seq=seq),
        grid=(n_tiles,),
        in_specs=[
            row_tile(ATTN_WIDTH),
            row_tile(conv_width),
            edge_of(-1), edge_of(0), edge_of(1),
            row_tile(d_model),
            pl.BlockSpec(memory_space=pl.ANY),
            full(conv_w.shape), full(conv_b.shape), full(norm_final.shape),
        ],
        out_specs=row_tile(d_model),
        out_shape=jax.ShapeDtypeStruct((rows, d_model), f32),
        scratch_shapes=_weight_stage_scratch(w),
        compiler_params=pltpu.CompilerParams(
            dimension_semantics=("arbitrary",),
            vmem_limit_bytes=_VMEM_LIMIT),
        name="out_projection",
    )(attn, conv, edge, edge, edge, x2, w, conv_w, conv_b, norm_final)


def kernel(x, norm_in, w_in, q_norm, k_norm, conv_w, conv_b, w_out, norm_final):
    batch, seq, d_model = x.shape
    assert norm_in.shape[0] == 1, "single-layer block"
    assert _PROJ_ROWS == _OUT_ROWS, "conv edge rows are exchanged per row tile"
    assert seq % _PROJ_ROWS == 0 and seq % _ATTN_TK == 0 and seq % (_ATTN_TQ * _ATTN_TILES_PER_ITER) == 0
    x2 = x.reshape(batch * seq, d_model)
    cos_t, sin_t = _rope_tables(seq)
    q, k, v, sg, conv, edge = _in_projection(
        x2, norm_in, w_in[0], q_norm, k_norm, cos_t, sin_t, conv_w[0], conv_b, batch, seq)
    attn = _attention(q, k, v, sg, q_norm, k_norm, batch, seq)
    out = _out_projection(attn, conv, edge, x2, w_out[0], conv_w[0], conv_b,
                          norm_final.reshape(1, d_model), seq)
    return out.reshape(batch, seq, d_model)
```

```python
import functools

import jax
import jax.numpy as jnp
import numpy as np
from jax import lax
from jax.experimental import pallas as pl
from jax.experimental.pallas import tpu as pltpu

HEAD_DIM = 128
N_Q_HEADS = 8
N_KV_HEADS = 2
GQA_GROUP = N_Q_HEADS // N_KV_HEADS
ATTN_WIDTH = N_Q_HEADS * HEAD_DIM
CONV_K = 3
GRID_W = 64
ROPE_THETA = 10000.0
ROPE_AXIS_DIM = HEAD_DIM // 2
EPS = 1e-6
_LOG2_E = 1.4426950408889634
_SCORE_BOUND = 64.0

_V7X_VMEM_BYTES = 64 * 2**20
_VMEM_LIMIT = _V7X_VMEM_BYTES - 8 * 2**20
_BF16_SUBLANES = 16
_F32_SUBLANES = 8
_EDGE_ROWS = 4 * _F32_SUBLANES
_W_BLOCK = 256
_W_STAGE_BYTES = 2 * 2**20
_W_STAGE_SLOTS = 4

_PROJ_ROWS = 512
_ATTN_TQ = 128
_ATTN_TK = 1024
_ATTN_TILES_PER_ITER = 8
_OUT_SUBTILES = 2

f32 = jnp.float32
bf16 = jnp.bfloat16


def _silu(x):
    return x / (1.0 + jnp.exp(-x))


def _head_norm(x, g):
    ms = jnp.mean(x * x, axis=-1, keepdims=True)
    return x * lax.rsqrt(ms + EPS) * g


def _rope(x, c, s):
    lane = lax.broadcasted_iota(jnp.int32, x.shape, 1)
    first_half = (lane & (ROPE_AXIS_DIM // 2)) == 0
    swapped = jnp.where(first_half,
                        pltpu.roll(x, HEAD_DIM - ROPE_AXIS_DIM // 2, 1),
                        pltpu.roll(x, ROPE_AXIS_DIM // 2, 1))
    return x * c + swapped * s


def _rope_t(x_t, cos_tt, sin_tt):
    q = ROPE_AXIS_DIM // 2
    swapped = jnp.concatenate([x_t[q:2 * q], x_t[0:q], x_t[3 * q:4 * q], x_t[2 * q:3 * q]], axis=0)
    return x_t * cos_tt + swapped * sin_tt


def _rope_tables(seq_len):
    rows = seq_len // GRID_W
    row = np.repeat(np.arange(rows, dtype=np.float64), GRID_W)
    col = np.tile(np.arange(GRID_W, dtype=np.float64), rows)
    inv_freq = ROPE_THETA ** (-np.arange(0, ROPE_AXIS_DIM, 2, dtype=np.float64) / ROPE_AXIS_DIM)
    ang_r = row[:, None] * inv_freq[None, :]
    ang_c = col[:, None] * inv_freq[None, :]
    cr, sr, cc, sc = np.cos(ang_r), np.sin(ang_r), np.cos(ang_c), np.sin(ang_c)
    cos_t = np.concatenate([cr, cr, cc, cc], axis=-1).astype(np.float32)
    sin_t = np.concatenate([-sr, sr, -sc, sc], axis=-1).astype(np.float32)
    return cos_t, sin_t


def _conv_taps(u_prev, u_mid, u_next, cw, cb):
    y = cb + u_prev * cw[0:1, :]
    y = y + u_mid * cw[1:2, :]
    return y + u_next * cw[2:3, :]


_Q_BLOCK0, _K_BLOCK, _V_BLOCK, _G_BLOCK0 = 0, 4, 5, 6
_CB_BLOCK0, _CC_BLOCK0, _CX_BLOCK0, _GC_BLOCK0 = 10, 14, 18, 22
_N_CONV_BLOCKS = 4


def _weight_stage_scratch(w):
    rows = 1 << ((_W_STAGE_BYTES // (4 * w.shape[1])).bit_length() - 1)
    assert w.shape[0] % rows == 0 and rows % _BF16_SUBLANES == 0
    stage = (_W_STAGE_SLOTS, rows, w.shape[1])
    return [pltpu.VMEM(w.shape, bf16), pltpu.VMEM(stage, f32), pltpu.SemaphoreType.DMA((_W_STAGE_SLOTS,))]


def _load_weights_as_bf16(w_hbm, w_ref, stage_ref, sem):
    slots, chunk = stage_ref.shape[0], stage_ref.shape[1]
    n_chunks = w_ref.shape[0] // chunk

    def copy(c):
        slot = c % slots
        return pltpu.make_async_copy(w_hbm.at[pl.ds(c * chunk, chunk), :], stage_ref.at[slot], sem.at[slot])

    for c in range(min(slots - 1, n_chunks)):
        copy(c).start()
    for c in range(n_chunks):
        if c + slots - 1 < n_chunks:
            copy(c + slots - 1).start()
        copy(c).wait()
        w_ref[c * chunk:(c + 1) * chunk, :] = stage_ref[c % slots].astype(bf16)


def _proj_kernel(x_ref, nin_ref, w_hbm, qg_ref, kn_ref, cos_ref, sin_ref, cost_ref, sint_ref, cw_ref, cb_ref,
                 q_ref, k_ref, v_ref, sg_ref, conv_ref, edge_ref, hn_ref, w_ref, stage_ref, sem):
    tm = hn_ref.shape[0]

    @pl.when(pl.program_id(0) == 0)
    def _():
        _load_weights_as_bf16(w_hbm, w_ref, stage_ref, sem)

    x = x_ref[...]
    ms = jnp.mean(x * x, axis=-1, keepdims=True)
    hn_ref[...] = (x * lax.rsqrt(ms + EPS) * nin_ref[...]).astype(bf16)

    def proj(block):
        w = w_ref[:, block * _W_BLOCK:(block + 1) * _W_BLOCK]
        return jnp.dot(hn_ref[...], w, preferred_element_type=f32)

    def head(mat, hh):
        return mat[:, hh * HEAD_DIM:(hh + 1) * HEAD_DIM]

    kk = proj(_K_BLOCK)
    for hh in range(N_KV_HEADS):
        k_ref[0, hh] = _rope(_head_norm(head(kk, hh), kn_ref[...]), cos_ref[...], sin_ref[...]).astype(bf16)

    cos_tt, sin_tt = cost_ref[...], sint_ref[...]
    gain = jnp.tile(jnp.broadcast_to(qg_ref[...], (HEAD_DIM, HEAD_DIM)).T, (1, tm // HEAD_DIM))
    scale = HEAD_DIM ** -0.5 * _LOG2_E
    for pair in range(N_Q_HEADS // 2):
        qq = proj(_Q_BLOCK0 + pair)
        for hh in range(2):
            q_t = head(qq, hh).T
            ms = jnp.mean(q_t * q_t, axis=0, keepdims=True)
            r = lax.rsqrt(ms + EPS) * scale
            q_ref[0, 2 * pair + hh] = (_rope_t(q_t * gain, cos_tt, sin_tt) * r).astype(bf16)

    vv = proj(_V_BLOCK)
    for hh in range(N_KV_HEADS):
        v_ref[0, hh] = head(vv, hh).T.astype(bf16)

    for c in range(ATTN_WIDTH // _W_BLOCK):
        sg_ref[:, c * _W_BLOCK:(c + 1) * _W_BLOCK] = _silu(proj(_G_BLOCK0 + c)).astype(bf16)

    g = _F32_SUBLANES
    for c in range(_N_CONV_BLOCKS):
        cols = slice(c * _W_BLOCK, (c + 1) * _W_BLOCK)
        u = proj(_CC_BLOCK0 + c) * proj(_CX_BLOCK0 + c)
        e = proj(_CB_BLOCK0 + c) * _silu(proj(_GC_BLOCK0 + c))
        y = _conv_taps(pltpu.roll(u, 1, 0), u, pltpu.roll(u, tm - 1, 0), cw_ref[:, cols], cb_ref[:, cols])
        conv_ref[:, cols] = (e * y).astype(bf16)
        edge_ref[0, 0 * g:1 * g, cols] = u[:g]
        edge_ref[0, 1 * g:2 * g, cols] = u[tm - g:]
        edge_ref[0, 2 * g:3 * g, cols] = e[:g]
        edge_ref[0, 3 * g:4 * g, cols] = e[tm - g:]


def _in_projection(x2, norm_in, w, q_norm, k_norm, cos_t, sin_t, conv_w, conv_b, batch, seq):
    rows, d_model = x2.shape
    tm = _PROJ_ROWS
    nt = seq // tm
    full = lambda shape: pl.BlockSpec(shape, lambda i: (0,) * len(shape))
    table = pl.BlockSpec((tm, HEAD_DIM), lambda i: (i % nt, 0))
    table_t = pl.BlockSpec((HEAD_DIM, tm), lambda i: (0, i % nt))
    row_tile = lambda width: pl.BlockSpec((tm, width), lambda i: (i, 0))
    conv_width = _N_CONV_BLOCKS * _W_BLOCK
    out_shape = (
        jax.ShapeDtypeStruct((batch, N_Q_HEADS, HEAD_DIM, seq), bf16),
        jax.ShapeDtypeStruct((batch, N_KV_HEADS, seq, HEAD_DIM), bf16),
        jax.ShapeDtypeStruct((batch, N_KV_HEADS, HEAD_DIM, seq), bf16),
        jax.ShapeDtypeStruct((rows, ATTN_WIDTH), bf16),
        jax.ShapeDtypeStruct((rows, conv_width), bf16),
        jax.ShapeDtypeStruct((rows // tm, _EDGE_ROWS, conv_width), f32),
    )
    out_specs = (
        pl.BlockSpec((1, N_Q_HEADS, HEAD_DIM, tm), lambda i: (i // nt, 0, 0, i % nt)),
        pl.BlockSpec((1, N_KV_HEADS, tm, HEAD_DIM), lambda i: (i // nt, 0, i % nt, 0)),
        pl.BlockSpec((1, N_KV_HEADS, HEAD_DIM, tm), lambda i: (i // nt, 0, 0, i % nt)),
        row_tile(ATTN_WIDTH), row_tile(conv_width),
        pl.BlockSpec((1, _EDGE_ROWS, conv_width), lambda i: (i, 0, 0)),
    )
    return pl.pallas_call(
        _proj_kernel,
        grid=(rows // tm,),
        in_specs=[
            row_tile(d_model),
            full((1, d_model)),
            pl.BlockSpec(memory_space=pl.ANY),
            full((1, HEAD_DIM)), full((1, HEAD_DIM)),
            table, table, table_t, table_t,
            full(conv_w.shape), full(conv_b.shape),
        ],
        out_specs=out_specs,
        out_shape=out_shape,
        scratch_shapes=[pltpu.VMEM((tm, d_model), bf16)] + _weight_stage_scratch(w),
        compiler_params=pltpu.CompilerParams(
            dimension_semantics=("arbitrary",),
            vmem_limit_bytes=_VMEM_LIMIT),
        name="in_projection",
    )(x2, norm_in, w, q_norm, k_norm, cos_t, sin_t, np.ascontiguousarray(cos_t.T), np.ascontiguousarray(sin_t.T),
      conv_w, conv_b)


def _attn_kernel(q_ref, k_ref, v_ref, sg_ref, wout_ref, o_ref, wbf_ref, acc_ref, *maybe_s_ref,
                 tq, tk, nk, nq, bounded):
    cols = GQA_GROUP * tq

    wbf_ref[...] = wout_ref[...].astype(bf16)

    def q_tile(i):
        off = pl.multiple_of(i * tq, tq)
        return jnp.concatenate([q_ref[0, h, :, pl.ds(off, tq)] for h in range(GQA_GROUP)], axis=1)

    def key_chunk(j):
        return k_ref[0, 0, pl.ds(pl.multiple_of(j * tk, tk), tk), :]

    ones_rows = jnp.ones((_BF16_SUBLANES, tk), bf16)

    def value_chunk(j):
        off = pl.multiple_of(j * tk, tk)
        return jnp.concatenate([v_ref[0, 0, :, pl.ds(off, tk)], ones_rows], axis=0)

    def finish_tile(i, acc):
        o_t = acc[:HEAD_DIM, :] / acc[HEAD_DIM:HEAD_DIM + 1, :]
        rows = pl.ds(pl.multiple_of(i * tq, tq), tq)
        for h in range(GQA_GROUP):
            lanes = slice(h * HEAD_DIM, (h + 1) * HEAD_DIM)
            gate = sg_ref[rows, lanes].astype(f32)
            o_ref[rows, lanes] = (o_t[:, h * tq:(h + 1) * tq].T * gate).astype(bf16)

    if bounded:
        def tile_group(g, carry):
            for t in range(_ATTN_TILES_PER_ITER):
                i = g * _ATTN_TILES_PER_ITER + t
                q_t = q_tile(i)
                acc = acc_ref.at[t]
                for c in range(nk):
                    p_t = jnp.exp2(jnp.dot(key_chunk(c), q_t, preferred_element_type=f32)).astype(bf16)
                    pv_t = jnp.dot(value_chunk(c), p_t, preferred_element_type=f32)
                    acc[...] = pv_t if c == 0 else acc[...] + pv_t
                finish_tile(i, acc)
            return carry

        lax.fori_loop(0, nq // _ATTN_TILES_PER_ITER, tile_group, 0)
        return

    s_ref, = maybe_s_ref

    def scores(q_mat, j, slot):
        s_t = jnp.dot(key_chunk(j), q_mat, preferred_element_type=f32)
        s_ref[slot] = s_t
        return jnp.max(s_t, axis=0, keepdims=True)

    def step(j, slot, acc, m_prev, chunk_max, q_ahead, j_ahead):
        ahead_max = scores(q_ahead, j_ahead, 1 - slot)
        m_new = jnp.maximum(m_prev, chunk_max)
        alpha = jnp.exp2(m_prev - m_new)
        p_t = jnp.exp2(s_ref[slot] - m_new).astype(bf16)
        pv_t = jnp.dot(value_chunk(j), p_t, preferred_element_type=f32)
        acc[...] = alpha * acc[...] + pv_t
        return m_new, ahead_max

    def query_tile(i, acc, chunk_max):
        q_t = q_tile(i)
        q_next = q_tile(jnp.minimum(i + 1, nq - 1))
        acc[...] = jnp.zeros(acc.shape, f32)
        m_run = jnp.full((1, cols), -jnp.inf, f32)
        for c in range(nk):
            last = c == nk - 1
            m_run, chunk_max = step(c, c % 2, acc, m_run, chunk_max, q_next if last else q_t, 0 if last else c + 1)
        finish_tile(i, acc)
        return chunk_max

    def tile_group(g, chunk_max):
        for t in range(_ATTN_TILES_PER_ITER):
            chunk_max = query_tile(g * _ATTN_TILES_PER_ITER + t, acc_ref.at[t], chunk_max)
        return chunk_max

    lax.fori_loop(0, nq // _ATTN_TILES_PER_ITER, tile_group, scores(q_tile(0), 0, 0))


def _attention_call(q_t, k, v_t, sg, w_out, batch, seq, bounded):
    tq, tk = _ATTN_TQ, _ATTN_TK
    cols = GQA_GROUP * tq
    group_width = GQA_GROUP * HEAD_DIM
    gate_spec = pl.BlockSpec((seq, group_width), lambda b, h: (b, h))
    w_rows = w_out.shape[0] // (batch * N_KV_HEADS)
    assert w_rows * batch * N_KV_HEADS == w_out.shape[0] and w_rows % _BF16_SUBLANES == 0
    w_spec = pl.BlockSpec((w_rows, w_out.shape[1]), lambda b, h: (b * N_KV_HEADS + h, 0))
    scratch = [pltpu.VMEM((_ATTN_TILES_PER_ITER, HEAD_DIM + _BF16_SUBLANES, cols), f32)]
    if not bounded:
        scratch.append(pltpu.VMEM((2, tk, cols), f32))
    return pl.pallas_call(
        functools.partial(_attn_kernel, tq=tq, tk=tk, nk=seq // tk, nq=seq // tq, bounded=bounded),
        grid=(batch, N_KV_HEADS),
        in_specs=[
            pl.BlockSpec((1, GQA_GROUP, HEAD_DIM, seq), lambda b, h: (b, h, 0, 0)),
            pl.BlockSpec((1, 1, seq, HEAD_DIM), lambda b, h: (b, h, 0, 0)),
            pl.BlockSpec((1, 1, HEAD_DIM, seq), lambda b, h: (b, h, 0, 0)),
            gate_spec, w_spec,
        ],
        out_specs=(gate_spec, w_spec),
        out_shape=(jax.ShapeDtypeStruct((batch * seq, ATTN_WIDTH), bf16),
                   jax.ShapeDtypeStruct(w_out.shape, bf16)),
        scratch_shapes=scratch,
        compiler_params=pltpu.CompilerParams(
            dimension_semantics=("arbitrary", "arbitrary"),
            vmem_limit_bytes=_VMEM_LIMIT),
        name="gqa_attention_bounded" if bounded else "gqa_attention",
    )(q_t, k, v_t, sg, w_out)


def _attention(q_t, k, v_t, sg, w_out, q_norm, k_norm, batch, seq):
    g_max = jnp.max(jnp.abs(jnp.concatenate([q_norm, k_norm], axis=0)))
    bound = (_LOG2_E * HEAD_DIM ** 0.5 * (1.0 + 2.0 ** -7)) * g_max * g_max
    return lax.cond(bound <= _SCORE_BOUND,
                    functools.partial(_attention_call, batch=batch, seq=seq, bounded=True),
                    functools.partial(_attention_call, batch=batch, seq=seq, bounded=False),
                    q_t, k, v_t, sg, w_out)


def _out_kernel(attn_ref, conv_ref, eprev_ref, edge_ref, enext_ref, x_ref, w_ref, cw_ref, cb_ref, nf_ref,
                o_ref, *, tile, seq):
    i = pl.program_id(0)
    n_sub = edge_ref.shape[0]
    g, p = _F32_SUBLANES, _BF16_SUBLANES
    cw, cb = cw_ref[...], cb_ref[...]
    sub = lax.broadcasted_iota(jnp.int32, (p, conv_ref.shape[1]), 0)
    for s in range(n_sub):
        lo, hi = s * tile, (s + 1) * tile
        t0 = ((i * n_sub + s) * tile) % seq
        row = lambda r: edge_ref[s, r:r + 1, :]
        before = eprev_ref[0, 2 * g - 1:2 * g, :] if s == 0 else edge_ref[s - 1, 2 * g - 1:2 * g, :]
        after = enext_ref[0, 0:1, :] if s == n_sub - 1 else edge_ref[s + 1, 0:1, :]
        u_before = jnp.where(t0 == 0, 0.0, before)
        u_after = jnp.where(t0 + tile == seq, 0.0, after)
        first = row(2 * g) * _conv_taps(u_before, row(0), row(1), cw, cb)
        last = row(4 * g - 1) * _conv_taps(row(2 * g - 2), row(2 * g - 1), u_after, cw, cb)
        top = jnp.where(sub == 0, first, conv_ref[lo:lo + p, :].astype(f32)).astype(bf16)
        bottom = jnp.where(sub == p - 1, last, conv_ref[hi - p:hi, :].astype(f32)).astype(bf16)
        conv = jnp.concatenate([top, conv_ref[lo + p:hi - p, :], bottom], axis=0)

        acc = jnp.dot(attn_ref[lo:hi, :], w_ref[:ATTN_WIDTH, :], preferred_element_type=f32)
        acc = acc + jnp.dot(conv, w_ref[ATTN_WIDTH:, :], preferred_element_type=f32)
        h = x_ref[lo:hi, :] + acc
        ms = jnp.mean(h * h, axis=-1, keepdims=True)
        o_ref[lo:hi, :] = h * lax.rsqrt(ms + EPS) * nf_ref[...]


def _out_projection(attn, conv, edge, x2, w_bf, conv_w, conv_b, norm_final, seq):
    rows, d_model = x2.shape
    tile = _PROJ_ROWS
    n_sub = _OUT_SUBTILES
    tm = tile * n_sub
    n_steps = rows // tm
    n_tiles = edge.shape[0]
    assert n_tiles * tile == rows and n_steps * tm == rows
    conv_width = conv.shape[1]
    row_tile = lambda width: pl.BlockSpec((tm, width), lambda i: (i, 0))
    full = lambda shape: pl.BlockSpec(shape, lambda i: (0,) * len(shape))
    edge_at = lambda tile_of_step: pl.BlockSpec(
        (1, _EDGE_ROWS, conv_width), lambda i: (jnp.clip(tile_of_step(i), 0, n_tiles - 1), 0, 0))
    return pl.pallas_call(
        functools.partial(_out_kernel, tile=tile, seq=seq),
        grid=(n_steps,),
        in_specs=[
            row_tile(ATTN_WIDTH),
            row_tile(conv_width),
            edge_at(lambda i: i * n_sub - 1),
            pl.BlockSpec((n_sub, _EDGE_ROWS, conv_width), lambda i: (i, 0, 0)),
            edge_at(lambda i: (i + 1) * n_sub),
            row_tile(d_model),
            pl.BlockSpec(w_bf.shape, lambda i: (0, 0), pipeline_mode=pl.Buffered(1)),
            full(conv_w.shape), full(conv_b.shape), full(norm_final.shape),
        ],
        out_specs=row_tile(d_model),
        out_shape=jax.ShapeDtypeStruct((rows, d_model), f32),
        compiler_params=pltpu.CompilerParams(
            dimension_semantics=("arbitrary",),
            vmem_limit_bytes=_VMEM_LIMIT),
        name="out_projection",
    )(attn, conv, edge, edge, edge, x2, w_bf, conv_w, conv_b, norm_final)


def kernel(x, norm_in, w_in, q_norm, k_norm, conv_w, conv_b, w_out, norm_final):
    batch, seq, d_model = x.shape
    assert norm_in.shape[0] == 1, "single-layer block"
    assert seq % (_PROJ_ROWS * _OUT_SUBTILES) == 0 and seq % _ATTN_TK == 0
    assert seq % (_ATTN_TQ * _ATTN_TILES_PER_ITER) == 0
    x2 = x.reshape(batch * seq, d_model)
    cos_t, sin_t = _rope_tables(seq)
    q, k, v, sg, conv, edge = _in_projection(
        x2, norm_in, w_in[0], q_norm, k_norm, cos_t, sin_t, conv_w[0], conv_b, batch, seq)
    attn, w_out_bf = _attention(q, k, v, sg, w_out[0], q_norm, k_norm, batch, seq)
    out = _out_projection(attn, conv, edge, x2, w_out_bf, conv_w[0], conv_b,
                          norm_final.reshape(1, d_model), seq)
    return out.reshape(batch, seq, d_model)
```

```python
import functools

import jax
import jax.numpy as jnp
import numpy as np
from jax import lax
from jax.experimental import pallas as pl
from jax.experimental.pallas import tpu as pltpu

HEAD_DIM = 128
N_Q_HEADS = 8
N_KV_HEADS = 2
GQA_GROUP = N_Q_HEADS // N_KV_HEADS
ATTN_WIDTH = N_Q_HEADS * HEAD_DIM
CONV_K = 3
GRID_W = 64
ROPE_THETA = 10000.0
ROPE_AXIS_DIM = HEAD_DIM // 2
EPS = 1e-6
_LOG2_E = 1.4426950408889634
_SCORE_BOUND = 64.0

_V7X_VMEM_BYTES = 64 * 2**20
_VMEM_LIMIT = _V7X_VMEM_BYTES - 8 * 2**20
_BF16_SUBLANES = 16
_F32_SUBLANES = 8
_EDGE_ROWS = 4 * _F32_SUBLANES
_W_BLOCK = 256
_W_STAGE_BYTES = 2 * 2**20
_W_STAGE_SLOTS = 4

_PROJ_ROWS = 512
_ATTN_TQ = 128
_ATTN_TK = 1024
_ATTN_TILES_PER_ITER = 8
_OUT_ROWS = 512

f32 = jnp.float32
bf16 = jnp.bfloat16


def _silu(x):
    return x / (1.0 + jnp.exp(-x))


def _head_norm(x, g):
    ms = jnp.mean(x * x, axis=-1, keepdims=True)
    return x * lax.rsqrt(ms + EPS) * g


def _rope(x, c, s):
    lane = lax.broadcasted_iota(jnp.int32, x.shape, 1)
    first_half = (lane & (ROPE_AXIS_DIM // 2)) == 0
    swapped = jnp.where(first_half,
                        pltpu.roll(x, HEAD_DIM - ROPE_AXIS_DIM // 2, 1),
                        pltpu.roll(x, ROPE_AXIS_DIM // 2, 1))
    return x * c + swapped * s


def _rope_t(x_t, cos_tt, sin_tt):
    q = ROPE_AXIS_DIM // 2
    swapped = jnp.concatenate([x_t[q:2 * q], x_t[0:q], x_t[3 * q:4 * q], x_t[2 * q:3 * q]], axis=0)
    return x_t * cos_tt + swapped * sin_tt


def _rope_tables(seq_len):
    rows = seq_len // GRID_W
    row = np.repeat(np.arange(rows, dtype=np.float64), GRID_W)
    col = np.tile(np.arange(GRID_W, dtype=np.float64), rows)
    inv_freq = ROPE_THETA ** (-np.arange(0, ROPE_AXIS_DIM, 2, dtype=np.float64) / ROPE_AXIS_DIM)
    ang_r = row[:, None] * inv_freq[None, :]
    ang_c = col[:, None] * inv_freq[None, :]
    cr, sr, cc, sc = np.cos(ang_r), np.sin(ang_r), np.cos(ang_c), np.sin(ang_c)
    cos_t = np.concatenate([cr, cr, cc, cc], axis=-1).astype(np.float32)
    sin_t = np.concatenate([-sr, sr, -sc, sc], axis=-1).astype(np.float32)
    return cos_t, sin_t


def _conv_taps(u_prev, u_mid, u_next, cw, cb):
    y = cb + u_prev * cw[0:1, :]
    y = y + u_mid * cw[1:2, :]
    return y + u_next * cw[2:3, :]


_Q_BLOCK0, _K_BLOCK, _V_BLOCK, _G_BLOCK0 = 0, 4, 5, 6
_CB_BLOCK0, _CC_BLOCK0, _CX_BLOCK0, _GC_BLOCK0 = 10, 14, 18, 22
_N_CONV_BLOCKS = 4


def _weight_stage_scratch(w):
    rows = 1 << ((_W_STAGE_BYTES // (4 * w.shape[1])).bit_length() - 1)
    assert w.shape[0] % rows == 0 and rows % _BF16_SUBLANES == 0
    stage = (_W_STAGE_SLOTS, rows, w.shape[1])
    return [pltpu.VMEM(w.shape, bf16), pltpu.VMEM(stage, f32), pltpu.SemaphoreType.DMA((_W_STAGE_SLOTS,))]


def _load_weights_as_bf16(w_hbm, w_ref, stage_ref, sem):
    slots, chunk = stage_ref.shape[0], stage_ref.shape[1]
    n_chunks = w_ref.shape[0] // chunk

    def copy(c):
        slot = c % slots
        return pltpu.make_async_copy(w_hbm.at[pl.ds(c * chunk, chunk), :], stage_ref.at[slot], sem.at[slot])

    for c in range(min(slots - 1, n_chunks)):
        copy(c).start()
    for c in range(n_chunks):
        if c + slots - 1 < n_chunks:
            copy(c + slots - 1).start()
        copy(c).wait()
        w_ref[c * chunk:(c + 1) * chunk, :] = stage_ref[c % slots].astype(bf16)


def _proj_kernel(x_ref, nin_ref, w_hbm, qg_ref, kn_ref, cos_ref, sin_ref, cost_ref, sint_ref, cw_ref, cb_ref,
                 q_ref, k_ref, v_ref, sg_ref, conv_ref, edge_ref, hn_ref, w_ref, stage_ref, sem):
    tm = hn_ref.shape[0]

    @pl.when(pl.program_id(0) == 0)
    def _():
        _load_weights_as_bf16(w_hbm, w_ref, stage_ref, sem)

    x = x_ref[...]
    ms = jnp.mean(x * x, axis=-1, keepdims=True)
    hn_ref[...] = (x * lax.rsqrt(ms + EPS) * nin_ref[...]).astype(bf16)

    def proj(block):
        w = w_ref[:, block * _W_BLOCK:(block + 1) * _W_BLOCK]
        return jnp.dot(hn_ref[...], w, preferred_element_type=f32)

    def head(mat, hh):
        return mat[:, hh * HEAD_DIM:(hh + 1) * HEAD_DIM]

    kk = proj(_K_BLOCK)
    for hh in range(N_KV_HEADS):
        k_ref[0, hh] = _rope(_head_norm(head(kk, hh), kn_ref[...]), cos_ref[...], sin_ref[...]).astype(bf16)

    cos_tt, sin_tt = cost_ref[...], sint_ref[...]
    gain = jnp.tile(jnp.broadcast_to(qg_ref[...], (HEAD_DIM, HEAD_DIM)).T, (1, tm // HEAD_DIM))
    scale = HEAD_DIM ** -0.5 * _LOG2_E
    for pair in range(N_Q_HEADS // 2):
        qq = proj(_Q_BLOCK0 + pair)
        for hh in range(2):
            q_t = head(qq, hh).T
            ms = jnp.mean(q_t * q_t, axis=0, keepdims=True)
            r = lax.rsqrt(ms + EPS) * scale
            q_ref[0, 2 * pair + hh] = (_rope_t(q_t * gain, cos_tt, sin_tt) * r).astype(bf16)

    vv = proj(_V_BLOCK)
    for hh in range(N_KV_HEADS):
        v_ref[0, hh] = head(vv, hh).T.astype(bf16)

    for c in range(ATTN_WIDTH // _W_BLOCK):
        sg_ref[:, c * _W_BLOCK:(c + 1) * _W_BLOCK] = _silu(proj(_G_BLOCK0 + c)).astype(bf16)

    g = _F32_SUBLANES
    for c in range(_N_CONV_BLOCKS):
        cols = slice(c * _W_BLOCK, (c + 1) * _W_BLOCK)
        u = proj(_CC_BLOCK0 + c) * proj(_CX_BLOCK0 + c)
        e = proj(_CB_BLOCK0 + c) * _silu(proj(_GC_BLOCK0 + c))
        y = _conv_taps(pltpu.roll(u, 1, 0), u, pltpu.roll(u, tm - 1, 0), cw_ref[:, cols], cb_ref[:, cols])
        conv_ref[:, cols] = (e * y).astype(bf16)
        edge_ref[0, 0 * g:1 * g, cols] = u[:g]
        edge_ref[0, 1 * g:2 * g, cols] = u[tm - g:]
        edge_ref[0, 2 * g:3 * g, cols] = e[:g]
        edge_ref[0, 3 * g:4 * g, cols] = e[tm - g:]


def _in_projection(x2, norm_in, w, q_norm, k_norm, cos_t, sin_t, conv_w, conv_b, batch, seq):
    rows, d_model = x2.shape
    tm = _PROJ_ROWS
    nt = seq // tm
    full = lambda shape: pl.BlockSpec(shape, lambda i: (0,) * len(shape))
    table = pl.BlockSpec((tm, HEAD_DIM), lambda i: (i % nt, 0))
    table_t = pl.BlockSpec((HEAD_DIM, tm), lambda i: (0, i % nt))
    row_tile = lambda width: pl.BlockSpec((tm, width), lambda i: (i, 0))
    conv_width = _N_CONV_BLOCKS * _W_BLOCK
    out_shape = (
        jax.ShapeDtypeStruct((batch, N_Q_HEADS, HEAD_DIM, seq), bf16),
        jax.ShapeDtypeStruct((batch, N_KV_HEADS, seq, HEAD_DIM), bf16),
        jax.ShapeDtypeStruct((batch, N_KV_HEADS, HEAD_DIM, seq), bf16),
        jax.ShapeDtypeStruct((rows, ATTN_WIDTH), bf16),
        jax.ShapeDtypeStruct((rows, conv_width), bf16),
        jax.ShapeDtypeStruct((rows // tm, _EDGE_ROWS, conv_width), f32),
    )
    out_specs = (
        pl.BlockSpec((1, N_Q_HEADS, HEAD_DIM, tm), lambda i: (i // nt, 0, 0, i % nt)),
        pl.BlockSpec((1, N_KV_HEADS, tm, HEAD_DIM), lambda i: (i // nt, 0, i % nt, 0)),
        pl.BlockSpec((1, N_KV_HEADS, HEAD_DIM, tm), lambda i: (i // nt, 0, 0, i % nt)),
        row_tile(ATTN_WIDTH), row_tile(conv_width),
        pl.BlockSpec((1, _EDGE_ROWS, conv_width), lambda i: (i, 0, 0)),
    )
    return pl.pallas_call(
        _proj_kernel,
        grid=(rows // tm,),
        in_specs=[
            row_tile(d_model),
            full((1, d_model)),
            pl.BlockSpec(memory_space=pl.ANY),
            full((1, HEAD_DIM)), full((1, HEAD_DIM)),
            table, table, table_t, table_t,
            full(conv_w.shape), full(conv_b.shape),
        ],
        out_specs=out_specs,
        out_shape=out_shape,
        scratch_shapes=[pltpu.VMEM((tm, d_model), bf16)] + _weight_stage_scratch(w),
        compiler_params=pltpu.CompilerParams(
            dimension_semantics=("arbitrary",),
            vmem_limit_bytes=_VMEM_LIMIT),
        name="in_projection",
    )(x2, norm_in, w, q_norm, k_norm, cos_t, sin_t, np.ascontiguousarray(cos_t.T), np.ascontiguousarray(sin_t.T),
      conv_w, conv_b)


def _attn_kernel(q_ref, k_ref, v_ref, sg_ref, o_ref, acc_ref, *maybe_s_ref, tq, tk, nk, nq, bounded):
    cols = GQA_GROUP * tq

    def q_tile(i):
        off = pl.multiple_of(i * tq, tq)
        return jnp.concatenate([q_ref[0, h, :, pl.ds(off, tq)] for h in range(GQA_GROUP)], axis=1)

    def key_chunk(j):
        return k_ref[0, 0, pl.ds(pl.multiple_of(j * tk, tk), tk), :]

    ones_rows = jnp.ones((_BF16_SUBLANES, tk), bf16)

    def value_chunk(j):
        off = pl.multiple_of(j * tk, tk)
        return jnp.concatenate([v_ref[0, 0, :, pl.ds(off, tk)], ones_rows], axis=0)

    def finish_tile(i, acc):
        o_t = acc[:HEAD_DIM, :] / acc[HEAD_DIM:HEAD_DIM + 1, :]
        rows = pl.ds(pl.multiple_of(i * tq, tq), tq)
        for h in range(GQA_GROUP):
            lanes = slice(h * HEAD_DIM, (h + 1) * HEAD_DIM)
            gate = sg_ref[rows, lanes].astype(f32)
            o_ref[rows, lanes] = (o_t[:, h * tq:(h + 1) * tq].T * gate).astype(bf16)

    if bounded:
        def tile_group(g, carry):
            for t in range(_ATTN_TILES_PER_ITER):
                i = g * _ATTN_TILES_PER_ITER + t
                q_t = q_tile(i)
                acc = acc_ref.at[t]
                for c in range(nk):
                    p_t = jnp.exp2(jnp.dot(key_chunk(c), q_t, preferred_element_type=f32)).astype(bf16)
                    pv_t = jnp.dot(value_chunk(c), p_t, preferred_element_type=f32)
                    acc[...] = pv_t if c == 0 else acc[...] + pv_t
                finish_tile(i, acc)
            return carry

        lax.fori_loop(0, nq // _ATTN_TILES_PER_ITER, tile_group, 0)
        return

    s_ref, = maybe_s_ref

    def scores(q_mat, j, slot):
        s_t = jnp.dot(key_chunk(j), q_mat, preferred_element_type=f32)
        s_ref[slot] = s_t
        return jnp.max(s_t, axis=0, keepdims=True)

    def step(j, slot, acc, m_prev, chunk_max, q_ahead, j_ahead):
        ahead_max = scores(q_ahead, j_ahead, 1 - slot)
        m_new = jnp.maximum(m_prev, chunk_max)
        alpha = jnp.exp2(m_prev - m_new)
        p_t = jnp.exp2(s_ref[slot] - m_new).astype(bf16)
        pv_t = jnp.dot(value_chunk(j), p_t, preferred_element_type=f32)
        acc[...] = alpha * acc[...] + pv_t
        return m_new, ahead_max

    def query_tile(i, acc, chunk_max):
        q_t = q_tile(i)
        q_next = q_tile(jnp.minimum(i + 1, nq - 1))
        acc[...] = jnp.zeros(acc.shape, f32)
        m_run = jnp.full((1, cols), -jnp.inf, f32)
        for c in range(nk):
            last = c == nk - 1
            m_run, chunk_max = step(c, c % 2, acc, m_run, chunk_max, q_next if last else q_t, 0 if last else c + 1)
        finish_tile(i, acc)
        return chunk_max

    def tile_group(g, chunk_max):
        for t in range(_ATTN_TILES_PER_ITER):
            chunk_max = query_tile(g * _ATTN_TILES_PER_ITER + t, acc_ref.at[t], chunk_max)
        return chunk_max

    lax.fori_loop(0, nq // _ATTN_TILES_PER_ITER, tile_group, scores(q_tile(0), 0, 0))


def _attention_call(q_t, k, v_t, sg, batch, seq, bounded):
    tq, tk = _ATTN_TQ, _ATTN_TK
    cols = GQA_GROUP * tq
    group_width = GQA_GROUP * HEAD_DIM
    gate_spec = pl.BlockSpec((seq, group_width), lambda b, h: (b, h))
    scratch = [pltpu.VMEM((_ATTN_TILES_PER_ITER, HEAD_DIM + _BF16_SUBLANES, cols), f32)]
    if not bounded:
        scratch.append(pltpu.VMEM((2, tk, cols), f32))
    return pl.pallas_call(
        functools.partial(_attn_kernel, tq=tq, tk=tk, nk=seq // tk, nq=seq // tq, bounded=bounded),
        grid=(batch, N_KV_HEADS),
        in_specs=[
            pl.BlockSpec((1, GQA_GROUP, HEAD_DIM, seq), lambda b, h: (b, h, 0, 0)),
            pl.BlockSpec((1, 1, seq, HEAD_DIM), lambda b, h: (b, h, 0, 0)),
            pl.BlockSpec((1, 1, HEAD_DIM, seq), lambda b, h: (b, h, 0, 0)),
            gate_spec,
        ],
        out_specs=gate_spec,
        out_shape=jax.ShapeDtypeStruct((batch * seq, ATTN_WIDTH), bf16),
        scratch_shapes=scratch,
        compiler_params=pltpu.CompilerParams(
            dimension_semantics=("arbitrary", "arbitrary"),
            vmem_limit_bytes=_VMEM_LIMIT),
        name="gqa_attention_bounded" if bounded else "gqa_attention",
    )(q_t, k, v_t, sg)


def _attention(q_t, k, v_t, sg, q_norm, k_norm, batch, seq):
    g_max = jnp.max(jnp.abs(jnp.concatenate([q_norm, k_norm], axis=0)))
    bound = (_LOG2_E * HEAD_DIM ** 0.5 * (1.0 + 2.0 ** -7)) * g_max * g_max
    return lax.cond(bound <= _SCORE_BOUND,
                    functools.partial(_attention_call, batch=batch, seq=seq, bounded=True),
                    functools.partial(_attention_call, batch=batch, seq=seq, bounded=False),
                    q_t, k, v_t, sg)


def _out_kernel(attn_ref, conv_ref, eprev_ref, edge_ref, enext_ref, x_ref, w_hbm, cw_ref, cb_ref, nf_ref,
                o_ref, w_ref, stage_ref, sem, *, tm, seq):
    i = pl.program_id(0)

    @pl.when(i == 0)
    def _():
        _load_weights_as_bf16(w_hbm, w_ref, stage_ref, sem)

    g = _F32_SUBLANES
    t0 = (i * tm) % seq
    cw, cb = cw_ref[...], cb_ref[...]
    row = lambda ref, r: ref[0, r:r + 1, :]
    u_before = jnp.where(t0 == 0, 0.0, row(eprev_ref, 2 * g - 1))
    u_after = jnp.where(t0 + tm == seq, 0.0, row(enext_ref, 0))
    first = row(edge_ref, 2 * g) * _conv_taps(u_before, row(edge_ref, 0), row(edge_ref, 1), cw, cb)
    last = row(edge_ref, 4 * g - 1) * _conv_taps(row(edge_ref, 2 * g - 2), row(edge_ref, 2 * g - 1), u_after, cw, cb)
    p = _BF16_SUBLANES
    sub = lax.broadcasted_iota(jnp.int32, (p, conv_ref.shape[1]), 0)
    top = jnp.where(sub == 0, first, conv_ref[:p, :].astype(f32)).astype(bf16)
    bottom = jnp.where(sub == p - 1, last, conv_ref[tm - p:, :].astype(f32)).astype(bf16)
    conv = jnp.concatenate([top, conv_ref[p:tm - p, :], bottom], axis=0)

    acc = jnp.dot(attn_ref[...], w_ref[:ATTN_WIDTH, :], preferred_element_type=f32)
    acc = acc + jnp.dot(conv, w_ref[ATTN_WIDTH:, :], preferred_element_type=f32)
    h = x_ref[...] + acc
    ms = jnp.mean(h * h, axis=-1, keepdims=True)
    o_ref[...] = h * lax.rsqrt(ms + EPS) * nf_ref[...]


def _out_projection(attn, conv, edge, x2, w, conv_w, conv_b, norm_final, seq):
    rows, d_model = x2.shape
    tm = _OUT_ROWS
    n_tiles = rows // tm
    assert edge.shape[0] == n_tiles, "edge rows are saved per projection tile"
    conv_width = conv.shape[1]
    row_tile = lambda width: pl.BlockSpec((tm, width), lambda i: (i, 0))
    full = lambda shape: pl.BlockSpec(shape, lambda i: (0,) * len(shape))
    edge_of = lambda shift: pl.BlockSpec((1, _EDGE_ROWS, conv_width),
                                         lambda i: (jnp.clip(i + shift, 0, n_tiles - 1), 0, 0))
    return pl.pallas_call(
        functools.partial(_out_kernel, tm=tm, seq=seq),
        grid=(n_tiles,),
        in_specs=[
            row_tile(ATTN_WIDTH),
            row_tile(conv_width),
            edge_of(-1), edge_of(0), edge_of(1),
            row_tile(d_model),
            pl.BlockSpec(memory_space=pl.ANY),
            full(conv_w.shape), full(conv_b.shape), full(norm_final.shape),
        ],
        out_specs=row_tile(d_model),
        out_shape=jax.ShapeDtypeStruct((rows, d_model), f32),
        scratch_shapes=_weight_stage_scratch(w),
        compiler_params=pltpu.CompilerParams(
            dimension_semantics=("arbitrary",),
            vmem_limit_bytes=_VMEM_LIMIT),
        name="out_projection",
    )(attn, conv, edge, edge, edge, x2, w, conv_w, conv_b, norm_final)


def kernel(x, norm_in, w_in, q_norm, k_norm, conv_w, conv_b, w_out, norm_final):
    batch, seq, d_model = x.shape
    assert norm_in.shape[0] == 1, "single-layer block"
    assert _PROJ_ROWS == _OUT_ROWS, "conv edge rows are exchanged per row tile"
    assert seq % _PROJ_ROWS == 0 and seq % _ATTN_TK == 0 and seq % (_ATTN_TQ * _ATTN_TILES_PER_ITER) == 0
    x2 = x.reshape(batch * seq, d_model)
    cos_t, sin_t = _rope_tables(seq)
    q, k, v, sg, conv, edge = _in_projection(
        x2, norm_in, w_in[0], q_norm, k_norm, cos_t, sin_t, conv_w[0], conv_b, batch, seq)
    attn = _attention(q, k, v, sg, q_norm, k_norm, batch, seq)
    out = _out_projection(attn, conv, edge, x2, w_out[0], conv_w[0], conv_b,
                          norm_final.reshape(1, d_model), seq)
    return out.reshape(batch, seq, d_model)
```

```python
import functools

import jax
import jax.numpy as jnp
import numpy as np
from jax import lax
from jax.experimental import pallas as pl
from jax.experimental.pallas import tpu as pltpu

HEAD_DIM = 128
N_Q_HEADS = 8
N_KV_HEADS = 2
GQA_GROUP = N_Q_HEADS // N_KV_HEADS
ATTN_WIDTH = N_Q_HEADS * HEAD_DIM
CONV_K = 3
GRID_W = 64
ROPE_THETA = 10000.0
ROPE_AXIS_DIM = HEAD_DIM // 2
EPS = 1e-6
_LOG2_E = 1.4426950408889634
_SCORE_BOUND = 64.0

_V7X_VMEM_BYTES = 64 * 2**20
_VMEM_LIMIT = _V7X_VMEM_BYTES - 8 * 2**20
_BF16_SUBLANES = 16
_F32_SUBLANES = 8
_EDGE_ROWS = 4 * _F32_SUBLANES
_W_BLOCK = 256
_W_STAGE_BYTES = 2 * 2**20
_W_STAGE_SLOTS = 4

_PROJ_ROWS = 512
_ATTN_TQ = 128
_ATTN_TK = 1024
_ATTN_TILES_PER_ITER = 8
_OUT_ROWS = 512

f32 = jnp.float32
bf16 = jnp.bfloat16


def _silu(x):
    return x / (1.0 + jnp.exp(-x))


def _head_norm(x, g):
    ms = jnp.mean(x * x, axis=-1, keepdims=True)
    return x * lax.rsqrt(ms + EPS) * g


def _rope(x, c, s):
    lane = lax.broadcasted_iota(jnp.int32, x.shape, 1)
    first_half = (lane & (ROPE_AXIS_DIM // 2)) == 0
    swapped = jnp.where(first_half,
                        pltpu.roll(x, HEAD_DIM - ROPE_AXIS_DIM // 2, 1),
                        pltpu.roll(x, ROPE_AXIS_DIM // 2, 1))
    return x * c + swapped * s


def _rope_t(x_t, cos_tt, sin_tt):
    q = ROPE_AXIS_DIM // 2
    swapped = jnp.concatenate([x_t[q:2 * q], x_t[0:q], x_t[3 * q:4 * q], x_t[2 * q:3 * q]], axis=0)
    return x_t * cos_tt + swapped * sin_tt


def _rope_tables(seq_len):
    rows = seq_len // GRID_W
    row = np.repeat(np.arange(rows, dtype=np.float64), GRID_W)
    col = np.tile(np.arange(GRID_W, dtype=np.float64), rows)
    inv_freq = ROPE_THETA ** (-np.arange(0, ROPE_AXIS_DIM, 2, dtype=np.float64) / ROPE_AXIS_DIM)
    ang_r = row[:, None] * inv_freq[None, :]
    ang_c = col[:, None] * inv_freq[None, :]
    cr, sr, cc, sc = np.cos(ang_r), np.sin(ang_r), np.cos(ang_c), np.sin(ang_c)
    cos_t = np.concatenate([cr, cr, cc, cc], axis=-1).astype(np.float32)
    sin_t = np.concatenate([-sr, sr, -sc, sc], axis=-1).astype(np.float32)
    return cos_t, sin_t


def _conv_taps(u_prev, u_mid, u_next, cw, cb):
    y = cb + u_prev * cw[0:1, :]
    y = y + u_mid * cw[1:2, :]
    return y + u_next * cw[2:3, :]


_Q_BLOCK0, _K_BLOCK, _V_BLOCK, _G_BLOCK0 = 0, 4, 5, 6
_CB_BLOCK0, _CC_BLOCK0, _CX_BLOCK0, _GC_BLOCK0 = 10, 14, 18, 22
_N_CONV_BLOCKS = 4


def _weight_stage_scratch(w):
    rows = 1 << ((_W_STAGE_BYTES // (4 * w.shape[1])).bit_length() - 1)
    assert w.shape[0] % rows == 0 and rows % _BF16_SUBLANES == 0
    stage = (_W_STAGE_SLOTS, rows, w.shape[1])
    return [pltpu.VMEM(w.shape, bf16), pltpu.VMEM(stage, f32), pltpu.SemaphoreType.DMA((_W_STAGE_SLOTS,))]


def _load_weights_as_bf16(w_hbm, w_ref, stage_ref, sem):
    slots, chunk = stage_ref.shape[0], stage_ref.shape[1]
    n_chunks = w_ref.shape[0] // chunk

    def copy(c):
        slot = c % slots
        return pltpu.make_async_copy(w_hbm.at[pl.ds(c * chunk, chunk), :], stage_ref.at[slot], sem.at[slot])

    for c in range(min(slots - 1, n_chunks)):
        copy(c).start()
    for c in range(n_chunks):
        if c + slots - 1 < n_chunks:
            copy(c + slots - 1).start()
        copy(c).wait()
        w_ref[c * chunk:(c + 1) * chunk, :] = stage_ref[c % slots].astype(bf16)


def _proj_kernel(x_ref, nin_ref, w_hbm, qg_ref, kn_ref, cos_ref, sin_ref, cost_ref, sint_ref, cw_ref, cb_ref,
                 q_ref, k_ref, v_ref, sg_ref, conv_ref, edge_ref, hn_ref, w_ref, stage_ref, sem):
    tm = hn_ref.shape[0]

    @pl.when(pl.program_id(0) == 0)
    def _():
        _load_weights_as_bf16(w_hbm, w_ref, stage_ref, sem)

    x = x_ref[...]
    ms = jnp.mean(x * x, axis=-1, keepdims=True)
    hn_ref[...] = (x * lax.rsqrt(ms + EPS) * nin_ref[...]).astype(bf16)

    def proj(block):
        w = w_ref[:, block * _W_BLOCK:(block + 1) * _W_BLOCK]
        return jnp.dot(hn_ref[...], w, preferred_element_type=f32)

    def head(mat, hh):
        return mat[:, hh * HEAD_DIM:(hh + 1) * HEAD_DIM]

    kk = proj(_K_BLOCK)
    for hh in range(N_KV_HEADS):
        k_ref[0, hh] = _rope(_head_norm(head(kk, hh), kn_ref[...]), cos_ref[...], sin_ref[...]).astype(bf16)

    cos_tt, sin_tt = cost_ref[...], sint_ref[...]
    gain = jnp.tile(jnp.broadcast_to(qg_ref[...], (HEAD_DIM, HEAD_DIM)).T, (1, tm // HEAD_DIM))
    scale = HEAD_DIM ** -0.5 * _LOG2_E
    for pair in range(N_Q_HEADS // 2):
        qq = proj(_Q_BLOCK0 + pair)
        for hh in range(2):
            q_t = head(qq, hh).T
            ms = jnp.mean(q_t * q_t, axis=0, keepdims=True)
            r = lax.rsqrt(ms + EPS) * scale
            q_ref[0, 2 * pair + hh] = (_rope_t(q_t * gain, cos_tt, sin_tt) * r).astype(bf16)

    vv = proj(_V_BLOCK)
    for hh in range(N_KV_HEADS):
        v_ref[0, hh] = head(vv, hh).T.astype(bf16)

    for c in range(ATTN_WIDTH // _W_BLOCK):
        sg_ref[:, c * _W_BLOCK:(c + 1) * _W_BLOCK] = _silu(proj(_G_BLOCK0 + c)).astype(bf16)

    g = _F32_SUBLANES
    for c in range(_N_CONV_BLOCKS):
        cols = slice(c * _W_BLOCK, (c + 1) * _W_BLOCK)
        u = proj(_CC_BLOCK0 + c) * proj(_CX_BLOCK0 + c)
        e = proj(_CB_BLOCK0 + c) * _silu(proj(_GC_BLOCK0 + c))
        y = _conv_taps(pltpu.roll(u, 1, 0), u, pltpu.roll(u, tm - 1, 0), cw_ref[:, cols], cb_ref[:, cols])
        conv_ref[:, cols] = (e * y).astype(bf16)
        edge_ref[0, 0 * g:1 * g, cols] = u[:g]
        edge_ref[0, 1 * g:2 * g, cols] = u[tm - g:]
        edge_ref[0, 2 * g:3 * g, cols] = e[:g]
        edge_ref[0, 3 * g:4 * g, cols] = e[tm - g:]


def _in_projection(x2, norm_in, w, q_norm, k_norm, cos_t, sin_t, conv_w, conv_b, batch, seq):
    rows, d_model = x2.shape
    tm = _PROJ_ROWS
    nt = seq // tm
    full = lambda shape: pl.BlockSpec(shape, lambda i: (0,) * len(shape))
    table = pl.BlockSpec((tm, HEAD_DIM), lambda i: (i % nt, 0))
    table_t = pl.BlockSpec((HEAD_DIM, tm), lambda i: (0, i % nt))
    row_tile = lambda width: pl.BlockSpec((tm, width), lambda i: (i, 0))
    conv_width = _N_CONV_BLOCKS * _W_BLOCK
    out_shape = (
        jax.ShapeDtypeStruct((batch, N_Q_HEADS, HEAD_DIM, seq), bf16),
        jax.ShapeDtypeStruct((batch, N_KV_HEADS, seq, HEAD_DIM), bf16),
        jax.ShapeDtypeStruct((batch, N_KV_HEADS, HEAD_DIM, seq), bf16),
        jax.ShapeDtypeStruct((rows, ATTN_WIDTH), bf16),
        jax.ShapeDtypeStruct((rows, conv_width), bf16),
        jax.ShapeDtypeStruct((rows // tm, _EDGE_ROWS, conv_width), f32),
    )
    out_specs = (
        pl.BlockSpec((1, N_Q_HEADS, HEAD_DIM, tm), lambda i: (i // nt, 0, 0, i % nt)),
        pl.BlockSpec((1, N_KV_HEADS, tm, HEAD_DIM), lambda i: (i // nt, 0, i % nt, 0)),
        pl.BlockSpec((1, N_KV_HEADS, HEAD_DIM, tm), lambda i: (i // nt, 0, 0, i % nt)),
        row_tile(ATTN_WIDTH), row_tile(conv_width),
        pl.BlockSpec((1, _EDGE_ROWS, conv_width), lambda i: (i, 0, 0)),
    )
    return pl.pallas_call(
        _proj_kernel,
        grid=(rows // tm,),
        in_specs=[
            row_tile(d_model),
            full((1, d_model)),
            pl.BlockSpec(memory_space=pl.ANY),
            full((1, HEAD_DIM)), full((1, HEAD_DIM)),
            table, table, table_t, table_t,
            full(conv_w.shape), full(conv_b.shape),
        ],
        out_specs=out_specs,
        out_shape=out_shape,
        scratch_shapes=[pltpu.VMEM((tm, d_model), bf16)] + _weight_stage_scratch(w),
        compiler_params=pltpu.CompilerParams(
            dimension_semantics=("arbitrary",),
            vmem_limit_bytes=_VMEM_LIMIT),
        name="in_projection",
    )(x2, norm_in, w, q_norm, k_norm, cos_t, sin_t, np.ascontiguousarray(cos_t.T), np.ascontiguousarray(sin_t.T),
      conv_w, conv_b)


def _attn_kernel(bound_ref, q_ref, k_ref, v_ref, sg_ref, o_ref, acc_ref, s_ref, *, tq, tk, nk, nq):
    cols = GQA_GROUP * tq
    bounded = bound_ref[0] <= _SCORE_BOUND

    def q_tile(i):
        off = pl.multiple_of(i * tq, tq)
        return jnp.concatenate([q_ref[0, h, :, pl.ds(off, tq)] for h in range(GQA_GROUP)], axis=1)

    def key_chunk(j):
        return k_ref[0, 0, pl.ds(pl.multiple_of(j * tk, tk), tk), :]

    ones_rows = jnp.ones((_BF16_SUBLANES, tk), bf16)

    def value_chunk(j):
        off = pl.multiple_of(j * tk, tk)
        return jnp.concatenate([v_ref[0, 0, :, pl.ds(off, tk)], ones_rows], axis=0)

    def finish_tile(i, acc):
        o_t = acc[:HEAD_DIM, :] / acc[HEAD_DIM:HEAD_DIM + 1, :]
        rows = pl.ds(pl.multiple_of(i * tq, tq), tq)
        for h in range(GQA_GROUP):
            lanes = slice(h * HEAD_DIM, (h + 1) * HEAD_DIM)
            gate = sg_ref[rows, lanes].astype(f32)
            o_ref[rows, lanes] = (o_t[:, h * tq:(h + 1) * tq].T * gate).astype(bf16)

    @pl.when(bounded)
    def _():
        def tile_group(g, carry):
            for t in range(_ATTN_TILES_PER_ITER):
                i = g * _ATTN_TILES_PER_ITER + t
                q_t = q_tile(i)
                acc = acc_ref.at[t]
                for c in range(nk):
                    p_t = jnp.exp2(jnp.dot(key_chunk(c), q_t, preferred_element_type=f32)).astype(bf16)
                    pv_t = jnp.dot(value_chunk(c), p_t, preferred_element_type=f32)
                    acc[...] = pv_t if c == 0 else acc[...] + pv_t
                finish_tile(i, acc)
            return carry

        lax.fori_loop(0, nq // _ATTN_TILES_PER_ITER, tile_group, 0)

    @pl.when(jnp.logical_not(bounded))
    def _():
        def scores(q_mat, j, slot):
            s_t = jnp.dot(key_chunk(j), q_mat, preferred_element_type=f32)
            s_ref[slot] = s_t
            return jnp.max(s_t, axis=0, keepdims=True)

        def step(j, slot, acc, m_prev, chunk_max, q_ahead, j_ahead):
            ahead_max = scores(q_ahead, j_ahead, 1 - slot)
            m_new = jnp.maximum(m_prev, chunk_max)
            alpha = jnp.exp2(m_prev - m_new)
            p_t = jnp.exp2(s_ref[slot] - m_new).astype(bf16)
            pv_t = jnp.dot(value_chunk(j), p_t, preferred_element_type=f32)
            acc[...] = alpha * acc[...] + pv_t
            return m_new, ahead_max

        def query_tile(i, acc, chunk_max):
            q_t = q_tile(i)
            q_next = q_tile(jnp.minimum(i + 1, nq - 1))
            acc[...] = jnp.zeros(acc.shape, f32)
            m_run = jnp.full((1, cols), -jnp.inf, f32)
            for c in range(nk):
                last = c == nk - 1
                m_run, chunk_max = step(c, c % 2, acc, m_run, chunk_max, q_next if last else q_t, 0 if last else c + 1)
            finish_tile(i, acc)
            return chunk_max

        def tile_group(g, chunk_max):
            for t in range(_ATTN_TILES_PER_ITER):
                chunk_max = query_tile(g * _ATTN_TILES_PER_ITER + t, acc_ref.at[t], chunk_max)
            return chunk_max

        lax.fori_loop(0, nq // _ATTN_TILES_PER_ITER, tile_group, scores(q_tile(0), 0, 0))


def _attention(q_t, k, v_t, sg, q_norm, k_norm, batch, seq):
    g_max = jnp.max(jnp.abs(jnp.concatenate([q_norm, k_norm], axis=0)))
    bound = ((_LOG2_E * HEAD_DIM ** 0.5 * (1.0 + 2.0 ** -7)) * g_max * g_max).reshape(1)
    tq, tk = _ATTN_TQ, _ATTN_TK
    cols = GQA_GROUP * tq
    group_width = GQA_GROUP * HEAD_DIM
    gate_spec = pl.BlockSpec((seq, group_width), lambda b, h: (b, h))
    return pl.pallas_call(
        functools.partial(_attn_kernel, tq=tq, tk=tk, nk=seq // tk, nq=seq // tq),
        grid=(batch, N_KV_HEADS),
        in_specs=[
            pl.BlockSpec(memory_space=pltpu.SMEM),
            pl.BlockSpec((1, GQA_GROUP, HEAD_DIM, seq), lambda b, h: (b, h, 0, 0)),
            pl.BlockSpec((1, 1, seq, HEAD_DIM), lambda b, h: (b, h, 0, 0)),
            pl.BlockSpec((1, 1, HEAD_DIM, seq), lambda b, h: (b, h, 0, 0)),
            gate_spec,
        ],
        out_specs=gate_spec,
        out_shape=jax.ShapeDtypeStruct((batch * seq, ATTN_WIDTH), bf16),
        scratch_shapes=[pltpu.VMEM((_ATTN_TILES_PER_ITER, HEAD_DIM + _BF16_SUBLANES, cols), f32),
                        pltpu.VMEM((2, tk, cols), f32)],
        compiler_params=pltpu.CompilerParams(
            dimension_semantics=("arbitrary", "arbitrary"),
            vmem_limit_bytes=_VMEM_LIMIT),
        name="gqa_attention",
    )(bound, q_t, k, v_t, sg)


def _out_kernel(attn_ref, conv_ref, eprev_ref, edge_ref, enext_ref, x_ref, w_hbm, cw_ref, cb_ref, nf_ref,
                o_ref, w_ref, stage_ref, sem, *, tm, seq):
    i = pl.program_id(0)

    @pl.when(i == 0)
    def _():
        _load_weights_as_bf16(w_hbm, w_ref, stage_ref, sem)

    g = _F32_SUBLANES
    t0 = (i * tm) % seq
    cw, cb = cw_ref[...], cb_ref[...]
    row = lambda ref, r: ref[0, r:r + 1, :]
    u_before = jnp.where(t0 == 0, 0.0, row(eprev_ref, 2 * g - 1))
    u_after = jnp.where(t0 + tm == seq, 0.0, row(enext_ref, 0))
    first = row(edge_ref, 2 * g) * _conv_taps(u_before, row(edge_ref, 0), row(edge_ref, 1), cw, cb)
    last = row(edge_ref, 4 * g - 1) * _conv_taps(row(edge_ref, 2 * g - 2), row(edge_ref, 2 * g - 1), u_after, cw, cb)
    p = _BF16_SUBLANES
    sub = lax.broadcasted_iota(jnp.int32, (p, conv_ref.shape[1]), 0)
    top = jnp.where(sub == 0, first, conv_ref[:p, :].astype(f32)).astype(bf16)
    bottom = jnp.where(sub == p - 1, last, conv_ref[tm - p:, :].astype(f32)).astype(bf16)
    conv = jnp.concatenate([top, conv_ref[p:tm - p, :], bottom], axis=0)

    acc = jnp.dot(attn_ref[...], w_ref[:ATTN_WIDTH, :], preferred_element_type=f32)
    acc = acc + jnp.dot(conv, w_ref[ATTN_WIDTH:, :], preferred_element_type=f32)
    h = x_ref[...] + acc
    ms = jnp.mean(h * h, axis=-1, keepdims=True)
    o_ref[...] = h * lax.rsqrt(ms + EPS) * nf_ref[...]


def _out_projection(attn, conv, edge, x2, w, conv_w, conv_b, norm_final, seq):
    rows, d_model = x2.shape
    tm = _OUT_ROWS
    n_tiles = rows // tm
    assert edge.shape[0] == n_tiles, "edge rows are saved per projection tile"
    conv_width = conv.shape[1]
    row_tile = lambda width: pl.BlockSpec((tm, width), lambda i: (i, 0))
    full = lambda shape: pl.BlockSpec(shape, lambda i: (0,) * len(shape))
    edge_of = lambda shift: pl.BlockSpec((1, _EDGE_ROWS, conv_width),
                                         lambda i: (jnp.clip(i + shift, 0, n_tiles - 1), 0, 0))
    return pl.pallas_call(
        functools.partial(_out_kernel, tm=tm, seq=seq),
        grid=(n_tiles,),
        in_specs=[
            row_tile(ATTN_WIDTH),
            row_tile(conv_width),
            edge_of(-1), edge_of(0), edge_of(1),
            row_tile(d_model),
            pl.BlockSpec(memory_space=pl.ANY),
            full(conv_w.shape), full(conv_b.shape), full(norm_final.shape),
        ],
        out_specs=row_tile(d_model),
        out_shape=jax.ShapeDtypeStruct((rows, d_model), f32),
        scratch_shapes=_weight_stage_scratch(w),
        compiler_params=pltpu.CompilerParams(
            dimension_semantics=("arbitrary",),
            vmem_limit_bytes=_VMEM_LIMIT),
        name="out_projection",
    )(attn, conv, edge, edge, edge, x2, w, conv_w, conv_b, norm_final)


def kernel(x, norm_in, w_in, q_norm, k_norm, conv_w, conv_b, w_out, norm_final):
    batch, seq, d_model = x.shape
    assert norm_in.shape[0] == 1, "single-layer block"
    assert _PROJ_ROWS == _OUT_ROWS, "conv edge rows are exchanged per row tile"
    assert seq % _PROJ_ROWS == 0 and seq % _ATTN_TK == 0 and seq % (_ATTN_TQ * _ATTN_TILES_PER_ITER) == 0
    x2 = x.reshape(batch * seq, d_model)
    cos_t, sin_t = _rope_tables(seq)
    q, k, v, sg, conv, edge = _in_projection(
        x2, norm_in, w_in[0], q_norm, k_norm, cos_t, sin_t, conv_w[0], conv_b, batch, seq)
    attn = _attention(q, k, v, sg, q_norm, k_norm, batch, seq)
    out = _out_projection(attn, conv, edge, x2, w_out[0], conv_w[0], conv_b,
                          norm_final.reshape(1, d_model), seq)
    return out.reshape(batch, seq, d_model)
```

```python
import functools

import jax
import jax.numpy as jnp
import numpy as np
from jax import lax
from jax.experimental import pallas as pl
from jax.experimental.pallas import tpu as pltpu

HEAD_DIM = 128
N_Q_HEADS = 8
N_KV_HEADS = 2
GQA_GROUP = N_Q_HEADS // N_KV_HEADS
ATTN_WIDTH = N_Q_HEADS * HEAD_DIM
CONV_K = 3
GRID_W = 64
ROPE_THETA = 10000.0
ROPE_AXIS_DIM = HEAD_DIM // 2
EPS = 1e-6
_LOG2_E = 1.4426950408889634
_SCORE_BOUND = 64.0

_V7X_VMEM_BYTES = 64 * 2**20
_VMEM_LIMIT = _V7X_VMEM_BYTES - 8 * 2**20
_BF16_SUBLANES = 16
_F32_SUBLANES = 8
_EDGE_ROWS = 4 * _F32_SUBLANES
_W_BLOCK = 256
_W_STAGE_BYTES = 2 * 2**20
_W_STAGE_SLOTS = 4

_PROJ_ROWS = 512
_ATTN_TQ = 128
_ATTN_TK = 1024
_ATTN_TILES_PER_ITER = 8
_OUT_ROWS = 512

f32 = jnp.float32
bf16 = jnp.bfloat16


def _silu(x):
    return x / (1.0 + jnp.exp(-x))


def _head_norm(x, g):
    ms = jnp.mean(x * x, axis=-1, keepdims=True)
    return x * lax.rsqrt(ms + EPS) * g


def _rope(x, c, s):
    lane = lax.broadcasted_iota(jnp.int32, x.shape, 1)
    first_half = (lane & (ROPE_AXIS_DIM // 2)) == 0
    swapped = jnp.where(first_half,
                        pltpu.roll(x, HEAD_DIM - ROPE_AXIS_DIM // 2, 1),
                        pltpu.roll(x, ROPE_AXIS_DIM // 2, 1))
    return x * c + swapped * s


def _rope_t(x_t, cos_tt, sin_tt):
    q = ROPE_AXIS_DIM // 2
    swapped = jnp.concatenate([x_t[q:2 * q], x_t[0:q], x_t[3 * q:4 * q], x_t[2 * q:3 * q]], axis=0)
    return x_t * cos_tt + swapped * sin_tt


def _rope_tables(seq_len):
    rows = seq_len // GRID_W
    row = np.repeat(np.arange(rows, dtype=np.float64), GRID_W)
    col = np.tile(np.arange(GRID_W, dtype=np.float64), rows)
    inv_freq = ROPE_THETA ** (-np.arange(0, ROPE_AXIS_DIM, 2, dtype=np.float64) / ROPE_AXIS_DIM)
    ang_r = row[:, None] * inv_freq[None, :]
    ang_c = col[:, None] * inv_freq[None, :]
    cr, sr, cc, sc = np.cos(ang_r), np.sin(ang_r), np.cos(ang_c), np.sin(ang_c)
    cos_t = np.concatenate([cr, cr, cc, cc], axis=-1).astype(np.float32)
    sin_t = np.concatenate([-sr, sr, -sc, sc], axis=-1).astype(np.float32)
    return cos_t, sin_t


def _conv_taps(u_prev, u_mid, u_next, cw, cb):
    y = cb + u_prev * cw[0:1, :]
    y = y + u_mid * cw[1:2, :]
    return y + u_next * cw[2:3, :]


_Q_BLOCK0, _K_BLOCK, _V_BLOCK, _G_BLOCK0 = 0, 4, 5, 6
_CB_BLOCK0, _CC_BLOCK0, _CX_BLOCK0, _GC_BLOCK0 = 10, 14, 18, 22
_N_CONV_BLOCKS = 4


def _weight_stage_scratch(w):
    rows = 1 << ((_W_STAGE_BYTES // (4 * w.shape[1])).bit_length() - 1)
    assert w.shape[0] % rows == 0 and rows % _BF16_SUBLANES == 0
    stage = (_W_STAGE_SLOTS, rows, w.shape[1])
    return [pltpu.VMEM(w.shape, bf16), pltpu.VMEM(stage, f32), pltpu.SemaphoreType.DMA((_W_STAGE_SLOTS,))]


def _load_weights_as_bf16(w_hbm, w_ref, stage_ref, sem):
    slots, chunk = stage_ref.shape[0], stage_ref.shape[1]
    n_chunks = w_ref.shape[0] // chunk

    def chunk_rows(c):
        start = c * chunk
        return pl.ds(start if isinstance(start, int) else pl.multiple_of(start, chunk), chunk)

    def copy(c):
        slot = c % slots
        return pltpu.make_async_copy(w_hbm.at[chunk_rows(c), :], stage_ref.at[slot], sem.at[slot])

    for c in range(min(slots - 1, n_chunks)):
        copy(c).start()

    def convert(c, carry):
        @pl.when(c + slots - 1 < n_chunks)
        def _():
            copy(c + slots - 1).start()

        copy(c).wait()
        w_ref[chunk_rows(c), :] = stage_ref[c % slots].astype(bf16)
        return carry

    lax.fori_loop(0, n_chunks, convert, 0)


def _proj_kernel(x_ref, nin_ref, w_hbm, qg_ref, kn_ref, cos_ref, sin_ref, cost_ref, sint_ref, cw_ref, cb_ref,
                 q_ref, k_ref, v_ref, sg_ref, conv_ref, edge_ref, hn_ref, w_ref, stage_ref, sem):
    tm = hn_ref.shape[0]

    @pl.when(pl.program_id(0) == 0)
    def _():
        _load_weights_as_bf16(w_hbm, w_ref, stage_ref, sem)

    x = x_ref[...]
    ms = jnp.mean(x * x, axis=-1, keepdims=True)
    hn_ref[...] = (x * lax.rsqrt(ms + EPS) * nin_ref[...]).astype(bf16)

    def proj(block):
        w = w_ref[:, block * _W_BLOCK:(block + 1) * _W_BLOCK]
        return jnp.dot(hn_ref[...], w, preferred_element_type=f32)

    def head(mat, hh):
        return mat[:, hh * HEAD_DIM:(hh + 1) * HEAD_DIM]

    kk = proj(_K_BLOCK)
    for hh in range(N_KV_HEADS):
        k_ref[0, hh] = _rope(_head_norm(head(kk, hh), kn_ref[...]), cos_ref[...], sin_ref[...]).astype(bf16)

    cos_tt, sin_tt = cost_ref[...], sint_ref[...]
    gain = jnp.tile(jnp.broadcast_to(qg_ref[...], (HEAD_DIM, HEAD_DIM)).T, (1, tm // HEAD_DIM))
    scale = HEAD_DIM ** -0.5 * _LOG2_E
    for pair in range(N_Q_HEADS // 2):
        qq = proj(_Q_BLOCK0 + pair)
        for hh in range(2):
            q_t = head(qq, hh).T
            ms = jnp.mean(q_t * q_t, axis=0, keepdims=True)
            r = lax.rsqrt(ms + EPS) * scale
            q_ref[0, 2 * pair + hh] = (_rope_t(q_t * gain, cos_tt, sin_tt) * r).astype(bf16)

    vv = proj(_V_BLOCK)
    for hh in range(N_KV_HEADS):
        v_ref[0, hh] = head(vv, hh).T.astype(bf16)

    for c in range(ATTN_WIDTH // _W_BLOCK):
        sg_ref[:, c * _W_BLOCK:(c + 1) * _W_BLOCK] = _silu(proj(_G_BLOCK0 + c)).astype(bf16)

    g = _F32_SUBLANES
    for c in range(_N_CONV_BLOCKS):
        cols = slice(c * _W_BLOCK, (c + 1) * _W_BLOCK)
        u = proj(_CC_BLOCK0 + c) * proj(_CX_BLOCK0 + c)
        e = proj(_CB_BLOCK0 + c) * _silu(proj(_GC_BLOCK0 + c))
        y = _conv_taps(pltpu.roll(u, 1, 0), u, pltpu.roll(u, tm - 1, 0), cw_ref[:, cols], cb_ref[:, cols])
        conv_ref[:, cols] = (e * y).astype(bf16)
        edge_ref[0, 0 * g:1 * g, cols] = u[:g]
        edge_ref[0, 1 * g:2 * g, cols] = u[tm - g:]
        edge_ref[0, 2 * g:3 * g, cols] = e[:g]
        edge_ref[0, 3 * g:4 * g, cols] = e[tm - g:]


def _in_projection(x2, norm_in, w, q_norm, k_norm, cos_t, sin_t, conv_w, conv_b, batch, seq):
    rows, d_model = x2.shape
    tm = _PROJ_ROWS
    nt = seq // tm
    full = lambda shape: pl.BlockSpec(shape, lambda i: (0,) * len(shape))
    table = pl.BlockSpec((tm, HEAD_DIM), lambda i: (i % nt, 0))
    table_t = pl.BlockSpec((HEAD_DIM, tm), lambda i: (0, i % nt))
    row_tile = lambda width: pl.BlockSpec((tm, width), lambda i: (i, 0))
    conv_width = _N_CONV_BLOCKS * _W_BLOCK
    out_shape = (
        jax.ShapeDtypeStruct((batch, N_Q_HEADS, HEAD_DIM, seq), bf16),
        jax.ShapeDtypeStruct((batch, N_KV_HEADS, seq, HEAD_DIM), bf16),
        jax.ShapeDtypeStruct((batch, N_KV_HEADS, HEAD_DIM, seq), bf16),
        jax.ShapeDtypeStruct((rows, ATTN_WIDTH), bf16),
        jax.ShapeDtypeStruct((rows, conv_width), bf16),
        jax.ShapeDtypeStruct((rows // tm, _EDGE_ROWS, conv_width), f32),
    )
    out_specs = (
        pl.BlockSpec((1, N_Q_HEADS, HEAD_DIM, tm), lambda i: (i // nt, 0, 0, i % nt)),
        pl.BlockSpec((1, N_KV_HEADS, tm, HEAD_DIM), lambda i: (i // nt, 0, i % nt, 0)),
        pl.BlockSpec((1, N_KV_HEADS, HEAD_DIM, tm), lambda i: (i // nt, 0, 0, i % nt)),
        row_tile(ATTN_WIDTH), row_tile(conv_width),
        pl.BlockSpec((1, _EDGE_ROWS, conv_width), lambda i: (i, 0, 0)),
    )
    return pl.pallas_call(
        _proj_kernel,
        grid=(rows // tm,),
        in_specs=[
            row_tile(d_model),
            full((1, d_model)),
            pl.BlockSpec(memory_space=pl.ANY),
            full((1, HEAD_DIM)), full((1, HEAD_DIM)),
            table, table, table_t, table_t,
            full(conv_w.shape), full(conv_b.shape),
        ],
        out_specs=out_specs,
        out_shape=out_shape,
        scratch_shapes=[pltpu.VMEM((tm, d_model), bf16)] + _weight_stage_scratch(w),
        compiler_params=pltpu.CompilerParams(
            dimension_semantics=("arbitrary",),
            vmem_limit_bytes=_VMEM_LIMIT),
        name="in_projection",
    )(x2, norm_in, w, q_norm, k_norm, cos_t, sin_t, np.ascontiguousarray(cos_t.T), np.ascontiguousarray(sin_t.T),
      conv_w, conv_b)


def _attn_kernel(q_ref, k_ref, v_ref, sg_ref, o_ref, acc_ref, *maybe_s_ref, tq, tk, nk, nq, bounded):
    cols = GQA_GROUP * tq

    def q_tile(i):
        off = pl.multiple_of(i * tq, tq)
        return jnp.concatenate([q_ref[0, h, :, pl.ds(off, tq)] for h in range(GQA_GROUP)], axis=1)

    def key_chunk(j):
        return k_ref[0, 0, pl.ds(pl.multiple_of(j * tk, tk), tk), :]

    ones_rows = jnp.ones((_BF16_SUBLANES, tk), bf16)

    def value_chunk(j):
        off = pl.multiple_of(j * tk, tk)
        return jnp.concatenate([v_ref[0, 0, :, pl.ds(off, tk)], ones_rows], axis=0)

    def finish_tile(i, acc):
        o_t = acc[:HEAD_DIM, :] / acc[HEAD_DIM:HEAD_DIM + 1, :]
        rows = pl.ds(pl.multiple_of(i * tq, tq), tq)
        for h in range(GQA_GROUP):
            lanes = slice(h * HEAD_DIM, (h + 1) * HEAD_DIM)
            gate = sg_ref[rows, lanes].astype(f32)
            o_ref[rows, lanes] = (o_t[:, h * tq:(h + 1) * tq].T * gate).astype(bf16)

    if bounded:
        def tile_group(g, carry):
            for t in range(_ATTN_TILES_PER_ITER):
                i = g * _ATTN_TILES_PER_ITER + t
                q_t = q_tile(i)
                acc = acc_ref.at[t]
                for c in range(nk):
                    p_t = jnp.exp2(jnp.dot(key_chunk(c), q_t, preferred_element_type=f32)).astype(bf16)
                    pv_t = jnp.dot(value_chunk(c), p_t, preferred_element_type=f32)
                    acc[...] = pv_t if c == 0 else acc[...] + pv_t
                finish_tile(i, acc)
            return carry

        lax.fori_loop(0, nq // _ATTN_TILES_PER_ITER, tile_group, 0)
        return

    s_ref, = maybe_s_ref

    def scores(q_mat, j, slot):
        s_t = jnp.dot(key_chunk(j), q_mat, preferred_element_type=f32)
        s_ref[slot] = s_t
        return jnp.max(s_t, axis=0, keepdims=True)

    def step(j, slot, acc, m_prev, chunk_max, q_ahead, j_ahead):
        ahead_max = scores(q_ahead, j_ahead, 1 - slot)
        m_new = jnp.maximum(m_prev, chunk_max)
        alpha = jnp.exp2(m_prev - m_new)
        p_t = jnp.exp2(s_ref[slot] - m_new).astype(bf16)
        pv_t = jnp.dot(value_chunk(j), p_t, preferred_element_type=f32)
        acc[...] = alpha * acc[...] + pv_t
        return m_new, ahead_max

    def query_tile(i, acc, chunk_max):
        q_t = q_tile(i)
        q_next = q_tile(jnp.minimum(i + 1, nq - 1))
        acc[...] = jnp.zeros(acc.shape, f32)
        m_run = jnp.full((1, cols), -jnp.inf, f32)
        for c in range(nk):
            last = c == nk - 1
            m_run, chunk_max = step(c, c % 2, acc, m_run, chunk_max, q_next if last else q_t, 0 if last else c + 1)
        finish_tile(i, acc)
        return chunk_max

    def tile_group(g, chunk_max):
        for t in range(_ATTN_TILES_PER_ITER):
            chunk_max = query_tile(g * _ATTN_TILES_PER_ITER + t, acc_ref.at[t], chunk_max)
        return chunk_max

    lax.fori_loop(0, nq // _ATTN_TILES_PER_ITER, tile_group, scores(q_tile(0), 0, 0))


def _attention_call(q_t, k, v_t, sg, batch, seq, bounded):
    tq, tk = _ATTN_TQ, _ATTN_TK
    cols = GQA_GROUP * tq
    group_width = GQA_GROUP * HEAD_DIM
    gate_spec = pl.BlockSpec((seq, group_width), lambda b, h: (b, h))
    scratch = [pltpu.VMEM((_ATTN_TILES_PER_ITER, HEAD_DIM + _BF16_SUBLANES, cols), f32)]
    if not bounded:
        scratch.append(pltpu.VMEM((2, tk, cols), f32))
    return pl.pallas_call(
        functools.partial(_attn_kernel, tq=tq, tk=tk, nk=seq // tk, nq=seq // tq, bounded=bounded),
        grid=(batch, N_KV_HEADS),
        in_specs=[
            pl.BlockSpec((1, GQA_GROUP, HEAD_DIM, seq), lambda b, h: (b, h, 0, 0)),
            pl.BlockSpec((1, 1, seq, HEAD_DIM), lambda b, h: (b, h, 0, 0)),
            pl.BlockSpec((1, 1, HEAD_DIM, seq), lambda b, h: (b, h, 0, 0)),
            gate_spec,
        ],
        out_specs=gate_spec,
        out_shape=jax.ShapeDtypeStruct((batch * seq, ATTN_WIDTH), bf16),
        scratch_shapes=scratch,
        compiler_params=pltpu.CompilerParams(
            dimension_semantics=("arbitrary", "arbitrary"),
            vmem_limit_bytes=_VMEM_LIMIT),
        name="gqa_attention_bounded" if bounded else "gqa_attention",
    )(q_t, k, v_t, sg)


def _attention(q_t, k, v_t, sg, q_norm, k_norm, batch, seq):
    g_max = jnp.max(jnp.abs(jnp.concatenate([q_norm, k_norm], axis=0)))
    bound = (_LOG2_E * HEAD_DIM ** 0.5 * (1.0 + 2.0 ** -7)) * g_max * g_max
    return lax.cond(bound <= _SCORE_BOUND,
                    functools.partial(_attention_call, batch=batch, seq=seq, bounded=True),
                    functools.partial(_attention_call, batch=batch, seq=seq, bounded=False),
                    q_t, k, v_t, sg)


def _out_kernel(attn_ref, conv_ref, eprev_ref, edge_ref, enext_ref, x_ref, w_hbm, cw_ref, cb_ref, nf_ref,
                o_ref, w_ref, stage_ref, sem, *, tm, seq):
    i = pl.program_id(0)

    @pl.when(i == 0)
    def _():
        _load_weights_as_bf16(w_hbm, w_ref, stage_ref, sem)

    g = _F32_SUBLANES
    t0 = (i * tm) % seq
    cw, cb = cw_ref[...], cb_ref[...]
    row = lambda ref, r: ref[0, r:r + 1, :]
    u_before = jnp.where(t0 == 0, 0.0, row(eprev_ref, 2 * g - 1))
    u_after = jnp.where(t0 + tm == seq, 0.0, row(enext_ref, 0))
    first = row(edge_ref, 2 * g) * _conv_taps(u_before, row(edge_ref, 0), row(edge_ref, 1), cw, cb)
    last = row(edge_ref, 4 * g - 1) * _conv_taps(row(edge_ref, 2 * g - 2), row(edge_ref, 2 * g - 1), u_after, cw, cb)
    p = _BF16_SUBLANES
    sub = lax.broadcasted_iota(jnp.int32, (p, conv_ref.shape[1]), 0)
    top = jnp.where(sub == 0, first, conv_ref[:p, :].astype(f32)).astype(bf16)
    bottom = jnp.where(sub == p - 1, last, conv_ref[tm - p:, :].astype(f32)).astype(bf16)
    conv = jnp.concatenate([top, conv_ref[p:tm - p, :], bottom], axis=0)

    acc = jnp.dot(attn_ref[...], w_ref[:ATTN_WIDTH, :], preferred_element_type=f32)
    acc = acc + jnp.dot(conv, w_ref[ATTN_WIDTH:, :], preferred_element_type=f32)
    h = x_ref[...] + acc
    ms = jnp.mean(h * h, axis=-1, keepdims=True)
    o_ref[...] = h * lax.rsqrt(ms + EPS) * nf_ref[...]


def _out_projection(attn, conv, edge, x2, w, conv_w, conv_b, norm_final, seq):
    rows, d_model = x2.shape
    tm = _OUT_ROWS
    n_tiles = rows // tm
    assert edge.shape[0] == n_tiles, "edge rows are saved per projection tile"
    conv_width = conv.shape[1]
    row_tile = lambda width: pl.BlockSpec((tm, width), lambda i: (i, 0))
    full = lambda shape: pl.BlockSpec(shape, lambda i: (0,) * len(shape))
    edge_of = lambda shift: pl.BlockSpec((1, _EDGE_ROWS, conv_width),
                                         lambda i: (jnp.clip(i + shift, 0, n_tiles - 1), 0, 0))
    return pl.pallas_call(
        functools.partial(_out_kernel, tm=tm, seq=seq),
        grid=(n_tiles,),
        in_specs=[
            row_tile(ATTN_WIDTH),
            row_tile(conv_width),
            edge_of(-1), edge_of(0), edge_of(1),
            row_tile(d_model),
            pl.BlockSpec(memory_space=pl.ANY),
            full(conv_w.shape), full(conv_b.shape), full(norm_final.shape),
        ],
        out_specs=row_tile(d_model),
        out_shape=jax.ShapeDtypeStruct((rows, d_model), f32),
        scratch_shapes=_weight_stage_scratch(w),
        compiler_params=pltpu.CompilerParams(
            dimension_semantics=("arbitrary",),
            vmem_limit_bytes=_VMEM_LIMIT),
        name="out_projection",
    )(attn, conv, edge, edge, edge, x2, w, conv_w, conv_b, norm_final)


def kernel(x, norm_in, w_in, q_norm, k_norm, conv_w, conv_b, w_out, norm_final):
    batch, seq, d_model = x.shape
    assert norm_in.shape[0] == 1, "single-layer block"
    assert _PROJ_ROWS == _OUT_ROWS, "conv edge rows are exchanged per row tile"
    assert seq % _PROJ_ROWS == 0 and seq % _ATTN_TK == 0 and seq % (_ATTN_TQ * _ATTN_TILES_PER_ITER) == 0
    x2 = x.reshape(batch * seq, d_model)
    cos_t, sin_t = _rope_tables(seq)
    q, k, v, sg, conv, edge = _in_projection(
        x2, norm_in, w_in[0], q_norm, k_norm, cos_t, sin_t, conv_w[0], conv_b, batch, seq)
    attn = _attention(q, k, v, sg, q_norm, k_norm, batch, seq)
    out = _out_projection(attn, conv, edge, x2, w_out[0], conv_w[0], conv_b,
                          norm_final.reshape(1, d_model), seq)
    return out.reshape(batch, seq, d_model)
```

```python
import functools

import jax
import jax.numpy as jnp
import numpy as np
from jax import lax
from jax.experimental import pallas as pl
from jax.experimental.pallas import tpu as pltpu

HEAD_DIM = 128
N_Q_HEADS = 8
N_KV_HEADS = 2
GQA_GROUP = N_Q_HEADS // N_KV_HEADS
ATTN_WIDTH = N_Q_HEADS * HEAD_DIM
CONV_K = 3
GRID_W = 64
ROPE_THETA = 10000.0
ROPE_AXIS_DIM = HEAD_DIM // 2
EPS = 1e-6
_LOG2_E = 1.4426950408889634
_SCORE_BOUND = 64.0

_V7X_VMEM_BYTES = 64 * 2**20
_VMEM_LIMIT = _V7X_VMEM_BYTES - 8 * 2**20
_BF16_SUBLANES = 16
_F32_SUBLANES = 8
_EDGE_ROWS = 4 * _F32_SUBLANES
_W_BLOCK = 256
_W_STAGE_BYTES = 2 * 2**20
_W_STAGE_SLOTS = 4

_PROJ_ROWS = 512
_ATTN_TQ = 128
_ATTN_TK = 1024
_ATTN_TILES_PER_ITER = 8
_OUT_ROWS = 512

f32 = jnp.float32
bf16 = jnp.bfloat16


def _silu(x):
    return x / (1.0 + jnp.exp(-x))


def _head_norm(x, g):
    ms = jnp.mean(x * x, axis=-1, keepdims=True)
    return x * lax.rsqrt(ms + EPS) * g


def _rope(x, c, s):
    lane = lax.broadcasted_iota(jnp.int32, x.shape, 1)
    first_half = (lane & (ROPE_AXIS_DIM // 2)) == 0
    swapped = jnp.where(first_half,
                        pltpu.roll(x, HEAD_DIM - ROPE_AXIS_DIM // 2, 1),
                        pltpu.roll(x, ROPE_AXIS_DIM // 2, 1))
    return x * c + swapped * s


def _rope_t(x_t, cos_tt, sin_tt):
    q = ROPE_AXIS_DIM // 2
    swapped = jnp.concatenate([x_t[q:2 * q], x_t[0:q], x_t[3 * q:4 * q], x_t[2 * q:3 * q]], axis=0)
    return x_t * cos_tt + swapped * sin_tt


def _rope_tables(seq_len):
    rows = seq_len // GRID_W
    row = np.repeat(np.arange(rows, dtype=np.float64), GRID_W)
    col = np.tile(np.arange(GRID_W, dtype=np.float64), rows)
    inv_freq = ROPE_THETA ** (-np.arange(0, ROPE_AXIS_DIM, 2, dtype=np.float64) / ROPE_AXIS_DIM)
    ang_r = row[:, None] * inv_freq[None, :]
    ang_c = col[:, None] * inv_freq[None, :]
    cr, sr, cc, sc = np.cos(ang_r), np.sin(ang_r), np.cos(ang_c), np.sin(ang_c)
    cos_t = np.concatenate([cr, cr, cc, cc], axis=-1).astype(np.float32)
    sin_t = np.concatenate([-sr, sr, -sc, sc], axis=-1).astype(np.float32)
    return cos_t, sin_t


def _conv_taps(u_prev, u_mid, u_next, cw, cb):
    y = cb + u_prev * cw[0:1, :]
    y = y + u_mid * cw[1:2, :]
    return y + u_next * cw[2:3, :]


_Q_BLOCK0, _K_BLOCK, _V_BLOCK, _G_BLOCK0 = 0, 4, 5, 6
_CB_BLOCK0, _CC_BLOCK0, _CX_BLOCK0, _GC_BLOCK0 = 10, 14, 18, 22
_N_CONV_BLOCKS = 4


def _weight_stage_scratch(w):
    rows = 1 << ((_W_STAGE_BYTES // (4 * w.shape[1])).bit_length() - 1)
    assert w.shape[0] % rows == 0 and rows % _BF16_SUBLANES == 0
    stage = (_W_STAGE_SLOTS, rows, w.shape[1])
    return [pltpu.VMEM(w.shape, bf16), pltpu.VMEM(stage, f32), pltpu.SemaphoreType.DMA((_W_STAGE_SLOTS,))]


def _load_weights_as_bf16(w_hbm, w_ref, stage_ref, sem):
    slots, chunk = stage_ref.shape[0], stage_ref.shape[1]
    n_chunks = w_ref.shape[0] // chunk

    def chunk_rows(c):
        start = c * chunk
        return pl.ds(start if isinstance(start, int) else pl.multiple_of(start, chunk), chunk)

    def copy(c):
        slot = c % slots
        return pltpu.make_async_copy(w_hbm.at[chunk_rows(c), :], stage_ref.at[slot], sem.at[slot])

    for c in range(min(slots - 1, n_chunks)):
        copy(c).start()

    def convert(c, carry):
        @pl.when(c + slots - 1 < n_chunks)
        def _():
            copy(c + slots - 1).start()

        copy(c).wait()
        w_ref[chunk_rows(c), :] = stage_ref[c % slots].astype(bf16)
        return carry

    lax.fori_loop(0, n_chunks, convert, 0)


def _proj_kernel(x_ref, nin_ref, w_hbm, qg_ref, kn_ref, cos_ref, sin_ref, cost_ref, sint_ref, cw_ref, cb_ref,
                 q_ref, k_ref, v_ref, sg_ref, conv_ref, edge_ref, hn_ref, w_ref, stage_ref, sem):
    tm = hn_ref.shape[0]

    @pl.when(pl.program_id(0) == 0)
    def _():
        _load_weights_as_bf16(w_hbm, w_ref, stage_ref, sem)

    x = x_ref[...]
    ms = jnp.mean(x * x, axis=-1, keepdims=True)
    hn_ref[...] = (x * lax.rsqrt(ms + EPS) * nin_ref[...]).astype(bf16)

    def proj(block):
        w = w_ref[:, block * _W_BLOCK:(block + 1) * _W_BLOCK]
        return jnp.dot(hn_ref[...], w, preferred_element_type=f32)

    def head(mat, hh):
        return mat[:, hh * HEAD_DIM:(hh + 1) * HEAD_DIM]

    kk = proj(_K_BLOCK)
    for hh in range(N_KV_HEADS):
        k_ref[0, hh] = _rope(_head_norm(head(kk, hh), kn_ref[...]), cos_ref[...], sin_ref[...]).astype(bf16)

    cos_tt, sin_tt = cost_ref[...], sint_ref[...]
    gain = jnp.tile(jnp.broadcast_to(qg_ref[...], (HEAD_DIM, HEAD_DIM)).T, (1, tm // HEAD_DIM))
    scale = HEAD_DIM ** -0.5 * _LOG2_E
    for pair in range(N_Q_HEADS // 2):
        qq = proj(_Q_BLOCK0 + pair)
        for hh in range(2):
            q_t = head(qq, hh).T
            ms = jnp.mean(q_t * q_t, axis=0, keepdims=True)
            r = lax.rsqrt(ms + EPS) * scale
            q_ref[0, 2 * pair + hh] = (_rope_t(q_t * gain, cos_tt, sin_tt) * r).astype(bf16)

    vv = proj(_V_BLOCK)
    for hh in range(N_KV_HEADS):
        v_ref[0, hh] = head(vv, hh).T.astype(bf16)

    for c in range(ATTN_WIDTH // _W_BLOCK):
        sg_ref[:, c * _W_BLOCK:(c + 1) * _W_BLOCK] = _silu(proj(_G_BLOCK0 + c)).astype(bf16)

    g = _F32_SUBLANES
    for c in range(_N_CONV_BLOCKS):
        cols = slice(c * _W_BLOCK, (c + 1) * _W_BLOCK)
        u = proj(_CC_BLOCK0 + c) * proj(_CX_BLOCK0 + c)
        e = proj(_CB_BLOCK0 + c) * _silu(proj(_GC_BLOCK0 + c))
        y = _conv_taps(pltpu.roll(u, 1, 0), u, pltpu.roll(u, tm - 1, 0), cw_ref[:, cols], cb_ref[:, cols])
        conv_ref[:, cols] = (e * y).astype(bf16)
        edge_ref[0, 0 * g:1 * g, cols] = u[:g]
        edge_ref[0, 1 * g:2 * g, cols] = u[tm - g:]
        edge_ref[0, 2 * g:3 * g, cols] = e[:g]
        edge_ref[0, 3 * g:4 * g, cols] = e[tm - g:]


def _in_projection(x2, norm_in, w, q_norm, k_norm, cos_t, sin_t, conv_w, conv_b, batch, seq):
    rows, d_model = x2.shape
    tm = _PROJ_ROWS
    nt = seq // tm
    full = lambda shape: pl.BlockSpec(shape, lambda i: (0,) * len(shape))
    table = pl.BlockSpec((tm, HEAD_DIM), lambda i: (i % nt, 0))
    table_t = pl.BlockSpec((HEAD_DIM, tm), lambda i: (0, i % nt))
    row_tile = lambda width: pl.BlockSpec((tm, width), lambda i: (i, 0))
    conv_width = _N_CONV_BLOCKS * _W_BLOCK
    out_shape = (
        jax.ShapeDtypeStruct((batch, N_Q_HEADS, HEAD_DIM, seq), bf16),
        jax.ShapeDtypeStruct((batch, N_KV_HEADS, seq, HEAD_DIM), bf16),
        jax.ShapeDtypeStruct((batch, N_KV_HEADS, HEAD_DIM, seq), bf16),
        jax.ShapeDtypeStruct((rows, ATTN_WIDTH), bf16),
        jax.ShapeDtypeStruct((rows, conv_width), bf16),
        jax.ShapeDtypeStruct((rows // tm, _EDGE_ROWS, conv_width), f32),
    )
    out_specs = (
        pl.BlockSpec((1, N_Q_HEADS, HEAD_DIM, tm), lambda i: (i // nt, 0, 0, i % nt)),
        pl.BlockSpec((1, N_KV_HEADS, tm, HEAD_DIM), lambda i: (i // nt, 0, i % nt, 0)),
        pl.BlockSpec((1, N_KV_HEADS, HEAD_DIM, tm), lambda i: (i // nt, 0, 0, i % nt)),
        row_tile(ATTN_WIDTH), row_tile(conv_width),
        pl.BlockSpec((1, _EDGE_ROWS, conv_width), lambda i: (i, 0, 0)),
    )
    return pl.pallas_call(
        _proj_kernel,
        grid=(rows // tm,),
        in_specs=[
            row_tile(d_model),
            full((1, d_model)),
            pl.BlockSpec(memory_space=pl.ANY),
            full((1, HEAD_DIM)), full((1, HEAD_DIM)),
            table, table, table_t, table_t,
            full(conv_w.shape), full(conv_b.shape),
        ],
        out_specs=out_specs,
        out_shape=out_shape,
        scratch_shapes=[pltpu.VMEM((tm, d_model), bf16)] + _weight_stage_scratch(w),
        compiler_params=pltpu.CompilerParams(
            dimension_semantics=("arbitrary",),
            vmem_limit_bytes=_VMEM_LIMIT),
        name="in_projection",
    )(x2, norm_in, w, q_norm, k_norm, cos_t, sin_t, np.ascontiguousarray(cos_t.T), np.ascontiguousarray(sin_t.T),
      conv_w, conv_b)


def _attn_kernel(q_ref, k_ref, v_ref, sg_ref, o_ref, acc_ref, *maybe_s_ref, tq, tk, nk, nq, bounded):
    cols = GQA_GROUP * tq

    def q_tile(i):
        off = pl.multiple_of(i * tq, tq)
        return jnp.concatenate([q_ref[0, h, :, pl.ds(off, tq)] for h in range(GQA_GROUP)], axis=1)

    def key_chunk(j):
        return k_ref[0, 0, pl.ds(pl.multiple_of(j * tk, tk), tk), :]

    ones_rows = jnp.ones((_BF16_SUBLANES, tk), bf16)

    def value_chunk(j):
        off = pl.multiple_of(j * tk, tk)
        return jnp.concatenate([v_ref[0, 0, :, pl.ds(off, tk)], ones_rows], axis=0)

    def finish_tile(i, acc):
        o_t = acc[:HEAD_DIM, :] / acc[HEAD_DIM:HEAD_DIM + 1, :]
        rows = pl.ds(pl.multiple_of(i * tq, tq), tq)
        for h in range(GQA_GROUP):
            lanes = slice(h * HEAD_DIM, (h + 1) * HEAD_DIM)
            gate = sg_ref[rows, lanes].astype(f32)
            o_ref[rows, lanes] = (o_t[:, h * tq:(h + 1) * tq].T * gate).astype(bf16)

    if bounded:
        def tile_group(g, carry):
            for t in range(_ATTN_TILES_PER_ITER):
                i = g * _ATTN_TILES_PER_ITER + t
                q_t = q_tile(i)
                acc = acc_ref.at[t]
                for c in range(nk):
                    p_t = jnp.exp2(jnp.dot(key_chunk(c), q_t, preferred_element_type=f32)).astype(bf16)
                    pv_t = jnp.dot(value_chunk(c), p_t, preferred_element_type=f32)
                    acc[...] = pv_t if c == 0 else acc[...] + pv_t
                finish_tile(i, acc)
            return carry

        lax.fori_loop(0, nq // _ATTN_TILES_PER_ITER, tile_group, 0)
        return

    s_ref, = maybe_s_ref

    def scores(q_mat, j, slot):
        s_t = jnp.dot(key_chunk(j), q_mat, preferred_element_type=f32)
        s_ref[slot] = s_t
        return jnp.max(s_t, axis=0, keepdims=True)

    def step(j, slot, acc, m_prev, chunk_max, q_ahead, j_ahead):
        ahead_max = scores(q_ahead, j_ahead, 1 - slot)
        m_new = jnp.maximum(m_prev, chunk_max)
        alpha = jnp.exp2(m_prev - m_new)
        p_t = jnp.exp2(s_ref[slot] - m_new).astype(bf16)
        pv_t = jnp.dot(value_chunk(j), p_t, preferred_element_type=f32)
        acc[...] = alpha * acc[...] + pv_t
        return m_new, ahead_max

    def query_tile(i, acc, chunk_max):
        q_t = q_tile(i)
        q_next = q_tile(jnp.minimum(i + 1, nq - 1))
        acc[...] = jnp.zeros(acc.shape, f32)
        m_run = jnp.full((1, cols), -jnp.inf, f32)
        for c in range(nk):
            last = c == nk - 1
            m_run, chunk_max = step(c, c % 2, acc, m_run, chunk_max, q_next if last else q_t, 0 if last else c + 1)
        finish_tile(i, acc)
        return chunk_max

    def tile_group(g, chunk_max):
        for t in range(_ATTN_TILES_PER_ITER):
            chunk_max = query_tile(g * _ATTN_TILES_PER_ITER + t, acc_ref.at[t], chunk_max)
        return chunk_max

    lax.fori_loop(0, nq // _ATTN_TILES_PER_ITER, tile_group, scores(q_tile(0), 0, 0))


def _attention_call(q_t, k, v_t, sg, batch, seq, bounded):
    tq, tk = _ATTN_TQ, _ATTN_TK
    cols = GQA_GROUP * tq
    group_width = GQA_GROUP * HEAD_DIM
    gate_spec = pl.BlockSpec((seq, group_width), lambda b, h: (b, h))
    scratch = [pltpu.VMEM((_ATTN_TILES_PER_ITER, HEAD_DIM + _BF16_SUBLANES, cols), f32)]
    if not bounded:
        scratch.append(pltpu.VMEM((2, tk, cols), f32))
    return pl.pallas_call(
        functools.partial(_attn_kernel, tq=tq, tk=tk, nk=seq // tk, nq=seq // tq, bounded=bounded),
        grid=(batch, N_KV_HEADS),
        in_specs=[
            pl.BlockSpec((1, GQA_GROUP, HEAD_DIM, seq), lambda b, h: (b, h, 0, 0)),
            pl.BlockSpec((1, 1, seq, HEAD_DIM), lambda b, h: (b, h, 0, 0)),
            pl.BlockSpec((1, 1, HEAD_DIM, seq), lambda b, h: (b, h, 0, 0)),
            gate_spec,
        ],
        out_specs=gate_spec,
        out_shape=jax.ShapeDtypeStruct((batch * seq, ATTN_WIDTH), bf16),
        scratch_shapes=scratch,
        compiler_params=pltpu.CompilerParams(
            dimension_semantics=("arbitrary", "arbitrary"),
            vmem_limit_bytes=_VMEM_LIMIT),
        name="gqa_attention_bounded" if bounded else "gqa_attention",
    )(q_t, k, v_t, sg)


def _attention(q_t, k, v_t, sg, q_norm, k_norm, batch, seq):
    g_max = jnp.max(jnp.abs(jnp.concatenate([q_norm, k_norm], axis=0)))
    bound = (_LOG2_E * HEAD_DIM ** 0.5 * (1.0 + 2.0 ** -7)) * g_max * g_max
    return lax.cond(jnp.logical_not(bound <= _SCORE_BOUND),
                    functools.partial(_attention_call, batch=batch, seq=seq, bounded=False),
                    functools.partial(_attention_call, batch=batch, seq=seq, bounded=True),
                    q_t, k, v_t, sg)


def _out_kernel(attn_ref, conv_ref, eprev_ref, edge_ref, enext_ref, x_ref, w_hbm, cw_ref, cb_ref, nf_ref,
                o_ref, w_ref, stage_ref, sem, *, tm, seq):
    i = pl.program_id(0)

    @pl.when(i == 0)
    def _():
        _load_weights_as_bf16(w_hbm, w_ref, stage_ref, sem)

    g = _F32_SUBLANES
    t0 = (i * tm) % seq
    cw, cb = cw_ref[...], cb_ref[...]
    row = lambda ref, r: ref[0, r:r + 1, :]
    u_before = jnp.where(t0 == 0, 0.0, row(eprev_ref, 2 * g - 1))
    u_after = jnp.where(t0 + tm == seq, 0.0, row(enext_ref, 0))
    first = row(edge_ref, 2 * g) * _conv_taps(u_before, row(edge_ref, 0), row(edge_ref, 1), cw, cb)
    last = row(edge_ref, 4 * g - 1) * _conv_taps(row(edge_ref, 2 * g - 2), row(edge_ref, 2 * g - 1), u_after, cw, cb)
    p = _BF16_SUBLANES
    sub = lax.broadcasted_iota(jnp.int32, (p, conv_ref.shape[1]), 0)
    top = jnp.where(sub == 0, first, conv_ref[:p, :].astype(f32)).astype(bf16)
    bottom = jnp.where(sub == p - 1, last, conv_ref[tm - p:, :].astype(f32)).astype(bf16)
    conv = jnp.concatenate([top, conv_ref[p:tm - p, :], bottom], axis=0)

    acc = jnp.dot(attn_ref[...], w_ref[:ATTN_WIDTH, :], preferred_element_type=f32)
    acc = acc + jnp.dot(conv, w_ref[ATTN_WIDTH:, :], preferred_element_type=f32)
    h = x_ref[...] + acc
    ms = jnp.mean(h * h, axis=-1, keepdims=True)
    o_ref[...] = h * lax.rsqrt(ms + EPS) * nf_ref[...]


def _out_projection(attn, conv, edge, x2, w, conv_w, conv_b, norm_final, seq):
    rows, d_model = x2.shape
    tm = _OUT_ROWS
    n_tiles = rows // tm
    assert edge.shape[0] == n_tiles, "edge rows are saved per projection tile"
    conv_width = conv.shape[1]
    row_tile = lambda width: pl.BlockSpec((tm, width), lambda i: (i, 0))
    full = lambda shape: pl.BlockSpec(shape, lambda i: (0,) * len(shape))
    edge_of = lambda shift: pl.BlockSpec((1, _EDGE_ROWS, conv_width),
                                         lambda i: (jnp.clip(i + shift, 0, n_tiles - 1), 0, 0))
    return pl.pallas_call(
        functools.partial(_out_kernel, tm=tm, seq=seq),
        grid=(n_tiles,),
        in_specs=[
            row_tile(ATTN_WIDTH),
            row_tile(conv_width),
            edge_of(-1), edge_of(0), edge_of(1),
            row_tile(d_model),
            pl.BlockSpec(memory_space=pl.ANY),
            full(conv_w.shape), full(conv_b.shape), full(norm_final.shape),
        ],
        out_specs=row_tile(d_model),
        out_shape=jax.ShapeDtypeStruct((rows, d_model), f32),
        scratch_shapes=_weight_stage_scratch(w),
        compiler_params=pltpu.CompilerParams(
            dimension_semantics=("arbitrary",),
            vmem_limit_bytes=_VMEM_LIMIT),
        name="out_projection",
    )(attn, conv, edge, edge, edge, x2, w, conv_w, conv_b, norm_final)


def kernel(x, norm_in, w_in, q_norm, k_norm, conv_w, conv_b, w_out, norm_final):
    batch, seq, d_model = x.shape
    assert norm_in.shape[0] == 1, "single-layer block"
    assert _PROJ_ROWS == _OUT_ROWS, "conv edge rows are exchanged per row tile"
    assert seq % _PROJ_ROWS == 0 and seq % _ATTN_TK == 0 and seq % (_ATTN_TQ * _ATTN_TILES_PER_ITER) == 0
    x2 = x.reshape(batch * seq, d_model)
    cos_t, sin_t = _rope_tables(seq)
    q, k, v, sg, conv, edge = _in_projection(
        x2, norm_in, w_in[0], q_norm, k_norm, cos_t, sin_t, conv_w[0], conv_b, batch, seq)
    attn = _attention(q, k, v, sg, q_norm, k_norm, batch, seq)
    out = _out_projection(attn, conv, edge, x2, w_out[0], conv_w[0], conv_b,
                          norm_final.reshape(1, d_model), seq)
    return out.reshape(batch, seq, d_model)
```

```python
import functools

import jax
import jax.numpy as jnp
import numpy as np
from jax import lax
from jax.experimental import pallas as pl
from jax.experimental.pallas import tpu as pltpu

HEAD_DIM = 128
N_Q_HEADS = 8
N_KV_HEADS = 2
GQA_GROUP = N_Q_HEADS // N_KV_HEADS
ATTN_WIDTH = N_Q_HEADS * HEAD_DIM
CONV_K = 3
GRID_W = 64
ROPE_THETA = 10000.0
ROPE_AXIS_DIM = HEAD_DIM // 2
EPS = 1e-6
_LOG2_E = 1.4426950408889634
_SCORE_BOUND = 64.0

_V7X_VMEM_BYTES = 64 * 2**20
_VMEM_LIMIT = _V7X_VMEM_BYTES - 8 * 2**20
_BF16_SUBLANES = 16
_F32_SUBLANES = 8
_EDGE_ROWS = 4 * _F32_SUBLANES
_W_BLOCK = 256
_W_STAGE_BYTES = 2 * 2**20
_W_STAGE_SLOTS = 4

_PROJ_ROWS = 512
_ATTN_TQ = 128
_ATTN_TK = 1024
_ATTN_TILES_PER_ITER = 8
_OUT_ROWS = 512

f32 = jnp.float32
bf16 = jnp.bfloat16


def _silu(x):
    return x / (1.0 + jnp.exp(-x))


def _head_norm(x, g):
    ms = jnp.mean(x * x, axis=-1, keepdims=True)
    return x * lax.rsqrt(ms + EPS) * g


def _rope(x, c, s):
    lane = lax.broadcasted_iota(jnp.int32, x.shape, 1)
    first_half = (lane & (ROPE_AXIS_DIM // 2)) == 0
    swapped = jnp.where(first_half,
                        pltpu.roll(x, HEAD_DIM - ROPE_AXIS_DIM // 2, 1),
                        pltpu.roll(x, ROPE_AXIS_DIM // 2, 1))
    return x * c + swapped * s


def _rope_t(x_t, cos_tt, sin_tt):
    q = ROPE_AXIS_DIM // 2
    swapped = jnp.concatenate([x_t[q:2 * q], x_t[0:q], x_t[3 * q:4 * q], x_t[2 * q:3 * q]], axis=0)
    return x_t * cos_tt + swapped * sin_tt


def _rope_tables(seq_len):
    rows = seq_len // GRID_W
    row = np.repeat(np.arange(rows, dtype=np.float64), GRID_W)
    col = np.tile(np.arange(GRID_W, dtype=np.float64), rows)
    inv_freq = ROPE_THETA ** (-np.arange(0, ROPE_AXIS_DIM, 2, dtype=np.float64) / ROPE_AXIS_DIM)
    ang_r = row[:, None] * inv_freq[None, :]
    ang_c = col[:, None] * inv_freq[None, :]
    cr, sr, cc, sc = np.cos(ang_r), np.sin(ang_r), np.cos(ang_c), np.sin(ang_c)
    cos_t = np.concatenate([cr, cr, cc, cc], axis=-1).astype(np.float32)
    sin_t = np.concatenate([-sr, sr, -sc, sc], axis=-1).astype(np.float32)
    return cos_t, sin_t


def _conv_taps(u_prev, u_mid, u_next, cw, cb):
    y = cb + u_prev * cw[0:1, :]
    y = y + u_mid * cw[1:2, :]
    return y + u_next * cw[2:3, :]


_Q_BLOCK0, _K_BLOCK, _V_BLOCK, _G_BLOCK0 = 0, 4, 5, 6
_CB_BLOCK0, _CC_BLOCK0, _CX_BLOCK0, _GC_BLOCK0 = 10, 14, 18, 22
_N_CONV_BLOCKS = 4


def _weight_stage_scratch(w):
    rows = 1 << ((_W_STAGE_BYTES // (4 * w.shape[1])).bit_length() - 1)
    assert w.shape[0] % rows == 0 and rows % _BF16_SUBLANES == 0
    stage = (_W_STAGE_SLOTS, rows, w.shape[1])
    return [pltpu.VMEM(w.shape, bf16), pltpu.VMEM(stage, f32), pltpu.SemaphoreType.DMA((_W_STAGE_SLOTS,))]


def _load_weights_as_bf16(w_hbm, w_ref, stage_ref, sem):
    slots, chunk = stage_ref.shape[0], stage_ref.shape[1]
    n_chunks = w_ref.shape[0] // chunk

    def chunk_rows(c):
        start = c * chunk
        return pl.ds(start if isinstance(start, int) else pl.multiple_of(start, chunk), chunk)

    def copy(c):
        slot = c % slots
        return pltpu.make_async_copy(w_hbm.at[chunk_rows(c), :], stage_ref.at[slot], sem.at[slot])

    for c in range(min(slots - 1, n_chunks)):
        copy(c).start()

    def convert(c, carry):
        @pl.when(c + slots - 1 < n_chunks)
        def _():
            copy(c + slots - 1).start()

        copy(c).wait()
        w_ref[chunk_rows(c), :] = stage_ref[c % slots].astype(bf16)
        return carry

    lax.fori_loop(0, n_chunks, convert, 0)


def _proj_kernel(x_ref, nin_ref, w_hbm, qg_ref, kn_ref, cos_ref, sin_ref, cost_ref, sint_ref, cw_ref, cb_ref,
                 q_ref, k_ref, v_ref, sg_ref, conv_ref, edge_ref, hn_ref, w_ref, stage_ref, sem):
    tm = hn_ref.shape[0]

    @pl.when(pl.program_id(0) == 0)
    def _():
        _load_weights_as_bf16(w_hbm, w_ref, stage_ref, sem)

    x = x_ref[...]
    ms = jnp.mean(x * x, axis=-1, keepdims=True)
    hn_ref[...] = (x * lax.rsqrt(ms + EPS) * nin_ref[...]).astype(bf16)

    def proj(block):
        w = w_ref[:, block * _W_BLOCK:(block + 1) * _W_BLOCK]
        return jnp.dot(hn_ref[...], w, preferred_element_type=f32)

    def head(mat, hh):
        return mat[:, hh * HEAD_DIM:(hh + 1) * HEAD_DIM]

    kk = proj(_K_BLOCK)
    for hh in range(N_KV_HEADS):
        k_ref[0, hh] = _rope(_head_norm(head(kk, hh), kn_ref[...]), cos_ref[...], sin_ref[...]).astype(bf16)

    cos_tt, sin_tt = cost_ref[...], sint_ref[...]
    gain = jnp.tile(jnp.broadcast_to(qg_ref[...], (HEAD_DIM, HEAD_DIM)).T, (1, tm // HEAD_DIM))
    scale = HEAD_DIM ** -0.5 * _LOG2_E
    for pair in range(N_Q_HEADS // 2):
        qq = proj(_Q_BLOCK0 + pair)
        for hh in range(2):
            q_t = head(qq, hh).T
            ms = jnp.mean(q_t * q_t, axis=0, keepdims=True)
            r = lax.rsqrt(ms + EPS) * scale
            q_ref[0, 2 * pair + hh] = (_rope_t(q_t * gain, cos_tt, sin_tt) * r).astype(bf16)

    vv = proj(_V_BLOCK)
    for hh in range(N_KV_HEADS):
        v_ref[0, hh] = head(vv, hh).T.astype(bf16)

    for c in range(ATTN_WIDTH // _W_BLOCK):
        sg_ref[:, c * _W_BLOCK:(c + 1) * _W_BLOCK] = _silu(proj(_G_BLOCK0 + c)).astype(bf16)

    g = _F32_SUBLANES
    for c in range(_N_CONV_BLOCKS):
        cols = slice(c * _W_BLOCK, (c + 1) * _W_BLOCK)
        u = proj(_CC_BLOCK0 + c) * proj(_CX_BLOCK0 + c)
        e = proj(_CB_BLOCK0 + c) * _silu(proj(_GC_BLOCK0 + c))
        y = _conv_taps(pltpu.roll(u, 1, 0), u, pltpu.roll(u, tm - 1, 0), cw_ref[:, cols], cb_ref[:, cols])
        conv_ref[:, cols] = (e * y).astype(bf16)
        edge_ref[0, 0 * g:1 * g, cols] = u[:g]
        edge_ref[0, 1 * g:2 * g, cols] = u[tm - g:]
        edge_ref[0, 2 * g:3 * g, cols] = e[:g]
        edge_ref[0, 3 * g:4 * g, cols] = e[tm - g:]


def _in_projection(x2, norm_in, w, q_norm, k_norm, cos_t, sin_t, conv_w, conv_b, batch, seq):
    rows, d_model = x2.shape
    tm = _PROJ_ROWS
    nt = seq // tm
    full = lambda shape: pl.BlockSpec(shape, lambda i: (0,) * len(shape))
    table = pl.BlockSpec((tm, HEAD_DIM), lambda i: (i % nt, 0))
    table_t = pl.BlockSpec((HEAD_DIM, tm), lambda i: (0, i % nt))
    row_tile = lambda width: pl.BlockSpec((tm, width), lambda i: (i, 0))
    conv_width = _N_CONV_BLOCKS * _W_BLOCK
    out_shape = (
        jax.ShapeDtypeStruct((batch, N_Q_HEADS, HEAD_DIM, seq), bf16),
        jax.ShapeDtypeStruct((batch, N_KV_HEADS, seq, HEAD_DIM), bf16),
        jax.ShapeDtypeStruct((batch, N_KV_HEADS, HEAD_DIM, seq), bf16),
        jax.ShapeDtypeStruct((rows, ATTN_WIDTH), bf16),
        jax.ShapeDtypeStruct((rows, conv_width), bf16),
        jax.ShapeDtypeStruct((rows // tm, _EDGE_ROWS, conv_width), f32),
    )
    out_specs = (
        pl.BlockSpec((1, N_Q_HEADS, HEAD_DIM, tm), lambda i: (i // nt, 0, 0, i % nt)),
        pl.BlockSpec((1, N_KV_HEADS, tm, HEAD_DIM), lambda i: (i // nt, 0, i % nt, 0)),
        pl.BlockSpec((1, N_KV_HEADS, HEAD_DIM, tm), lambda i: (i // nt, 0, 0, i % nt)),
        row_tile(ATTN_WIDTH), row_tile(conv_width),
        pl.BlockSpec((1, _EDGE_ROWS, conv_width), lambda i: (i, 0, 0)),
    )
    return pl.pallas_call(
        _proj_kernel,
        grid=(rows // tm,),
        in_specs=[
            row_tile(d_model),
            full((1, d_model)),
            pl.BlockSpec(memory_space=pl.ANY),
            full((1, HEAD_DIM)), full((1, HEAD_DIM)),
            table, table, table_t, table_t,
            full(conv_w.shape), full(conv_b.shape),
        ],
        out_specs=out_specs,
        out_shape=out_shape,
        scratch_shapes=[pltpu.VMEM((tm, d_model), bf16)] + _weight_stage_scratch(w),
        compiler_params=pltpu.CompilerParams(
            dimension_semantics=("arbitrary",),
            vmem_limit_bytes=_VMEM_LIMIT),
        name="in_projection",
    )(x2, norm_in, w, q_norm, k_norm, cos_t, sin_t, np.ascontiguousarray(cos_t.T), np.ascontiguousarray(sin_t.T),
      conv_w, conv_b)


def _attn_kernel(q_ref, k_ref, v_ref, sg_ref, o_ref, acc_ref, *maybe_s_ref, tq, tk, nk, nq, bounded):
    cols = GQA_GROUP * tq

    def q_tile(i):
        off = pl.multiple_of(i * tq, tq)
        return jnp.concatenate([q_ref[0, h, :, pl.ds(off, tq)] for h in range(GQA_GROUP)], axis=1)

    def key_chunk(j):
        return k_ref[0, 0, pl.ds(pl.multiple_of(j * tk, tk), tk), :]

    ones_rows = jnp.ones((_BF16_SUBLANES, tk), bf16)

    def value_chunk(j):
        off = pl.multiple_of(j * tk, tk)
        return jnp.concatenate([v_ref[0, 0, :, pl.ds(off, tk)], ones_rows], axis=0)

    def finish_tile(i, acc):
        o_t = acc[:HEAD_DIM, :] / acc[HEAD_DIM:HEAD_DIM + 1, :]
        rows = pl.ds(pl.multiple_of(i * tq, tq), tq)
        for h in range(GQA_GROUP):
            lanes = slice(h * HEAD_DIM, (h + 1) * HEAD_DIM)
            gate = sg_ref[rows, lanes].astype(f32)
            o_ref[rows, lanes] = (o_t[:, h * tq:(h + 1) * tq].T * gate).astype(bf16)

    if bounded:
        def tile_group(g, carry):
            for t in range(_ATTN_TILES_PER_ITER):
                i = g * _ATTN_TILES_PER_ITER + t
                q_t = q_tile(i)
                acc = acc_ref.at[t]
                for c in range(nk):
                    p_t = jnp.exp2(jnp.dot(key_chunk(c), q_t, preferred_element_type=f32)).astype(bf16)
                    pv_t = jnp.dot(value_chunk(c), p_t, preferred_element_type=f32)
                    acc[...] = pv_t if c == 0 else acc[...] + pv_t
                finish_tile(i, acc)
            return carry

        lax.fori_loop(0, nq // _ATTN_TILES_PER_ITER, tile_group, 0)
        return

    s_ref, = maybe_s_ref

    def scores(q_mat, j, slot):
        s_t = jnp.dot(key_chunk(j), q_mat, preferred_element_type=f32)
        s_ref[slot] = s_t
        return jnp.max(s_t, axis=0, keepdims=True)

    def step(j, slot, acc, m_prev, chunk_max, q_ahead, j_ahead):
        ahead_max = scores(q_ahead, j_ahead, 1 - slot)
        m_new = jnp.maximum(m_prev, chunk_max)
        alpha = jnp.exp2(m_prev - m_new)
        p_t = jnp.exp2(s_ref[slot] - m_new).astype(bf16)
        pv_t = jnp.dot(value_chunk(j), p_t, preferred_element_type=f32)
        acc[...] = alpha * acc[...] + pv_t
        return m_new, ahead_max

    def query_tile(i, acc, chunk_max):
        q_t = q_tile(i)
        q_next = q_tile(jnp.minimum(i + 1, nq - 1))
        acc[...] = jnp.zeros(acc.shape, f32)
        m_run = jnp.full((1, cols), -jnp.inf, f32)
        for c in range(nk):
            last = c == nk - 1
            m_run, chunk_max = step(c, c % 2, acc, m_run, chunk_max, q_next if last else q_t, 0 if last else c + 1)
        finish_tile(i, acc)
        return chunk_max

    def tile_group(g, chunk_max):
        for t in range(_ATTN_TILES_PER_ITER):
            chunk_max = query_tile(g * _ATTN_TILES_PER_ITER + t, acc_ref.at[t], chunk_max)
        return chunk_max

    lax.fori_loop(0, nq // _ATTN_TILES_PER_ITER, tile_group, scores(q_tile(0), 0, 0))


def _attention_call(q_t, k, v_t, sg, batch, seq, bounded):
    tq, tk = _ATTN_TQ, _ATTN_TK
    cols = GQA_GROUP * tq
    group_width = GQA_GROUP * HEAD_DIM
    gate_spec = pl.BlockSpec((seq, group_width), lambda b, h: (b, h))
    scratch = [pltpu.VMEM((_ATTN_TILES_PER_ITER, HEAD_DIM + _BF16_SUBLANES, cols), f32)]
    if not bounded:
        scratch.append(pltpu.VMEM((2, tk, cols), f32))
    return pl.pallas_call(
        functools.partial(_attn_kernel, tq=tq, tk=tk, nk=seq // tk, nq=seq // tq, bounded=bounded),
        grid=(batch, N_KV_HEADS),
        in_specs=[
            pl.BlockSpec((1, GQA_GROUP, HEAD_DIM, seq), lambda b, h: (b, h, 0, 0)),
            pl.BlockSpec((1, 1, seq, HEAD_DIM), lambda b, h: (b, h, 0, 0)),
            pl.BlockSpec((1, 1, HEAD_DIM, seq), lambda b, h: (b, h, 0, 0)),
            gate_spec,
        ],
        out_specs=gate_spec,
        out_shape=jax.ShapeDtypeStruct((batch * seq, ATTN_WIDTH), bf16),
        scratch_shapes=scratch,
        compiler_params=pltpu.CompilerParams(
            dimension_semantics=("arbitrary", "arbitrary"),
            vmem_limit_bytes=_VMEM_LIMIT),
        name="gqa_attention_bounded" if bounded else "gqa_attention",
    )(q_t, k, v_t, sg)


def _attention(q_t, k, v_t, sg, q_norm, k_norm, batch, seq):
    g_max = jnp.max(jnp.abs(jnp.concatenate([q_norm, k_norm], axis=0)))
    bound = (_LOG2_E * HEAD_DIM ** 0.5 * (1.0 + 2.0 ** -7)) * g_max * g_max
    return lax.cond(bound <= _SCORE_BOUND,
                    functools.partial(_attention_call, batch=batch, seq=seq, bounded=True),
                    functools.partial(_attention_call, batch=batch, seq=seq, bounded=False),
                    q_t, k, v_t, sg)


def _out_kernel(attn_ref, conv_ref, eprev_ref, edge_ref, enext_ref, x_ref, w_hbm, cw_ref, cb_ref, nf_ref,
                o_ref, w_ref, stage_ref, sem, h_ref, *, tm, seq, n_tiles):
    i = pl.program_id(0)

    @pl.when(i == 0)
    def _():
        _load_weights_as_bf16(w_hbm, w_ref, stage_ref, sem)
        h_ref[1] = jnp.zeros(h_ref.shape[1:], f32)

    g = _F32_SUBLANES
    t0 = (jnp.minimum(i, n_tiles - 1) * tm) % seq
    cw, cb = cw_ref[...], cb_ref[...]
    row = lambda ref, r: ref[0, r:r + 1, :]
    u_before = jnp.where(t0 == 0, 0.0, row(eprev_ref, 2 * g - 1))
    u_after = jnp.where(t0 + tm == seq, 0.0, row(enext_ref, 0))
    first = row(edge_ref, 2 * g) * _conv_taps(u_before, row(edge_ref, 0), row(edge_ref, 1), cw, cb)
    last = row(edge_ref, 4 * g - 1) * _conv_taps(row(edge_ref, 2 * g - 2), row(edge_ref, 2 * g - 1), u_after, cw, cb)
    p = _BF16_SUBLANES
    sub = lax.broadcasted_iota(jnp.int32, (p, conv_ref.shape[1]), 0)
    top = jnp.where(sub == 0, first, conv_ref[:p, :].astype(f32)).astype(bf16)
    bottom = jnp.where(sub == p - 1, last, conv_ref[tm - p:, :].astype(f32)).astype(bf16)
    conv = jnp.concatenate([top, conv_ref[p:tm - p, :], bottom], axis=0)

    acc = jnp.dot(attn_ref[...], w_ref[:ATTN_WIDTH, :], preferred_element_type=f32)
    acc = acc + jnp.dot(conv, w_ref[ATTN_WIDTH:, :], preferred_element_type=f32)
    h_ref[i % 2] = x_ref[...] + acc

    h_prev = h_ref[(i + 1) % 2]
    ms_prev = jnp.mean(h_prev * h_prev, axis=-1, keepdims=True)
    o_ref[...] = h_prev * lax.rsqrt(ms_prev + EPS) * nf_ref[...]


def _out_projection(attn, conv, edge, x2, w, conv_w, conv_b, norm_final, seq):
    rows, d_model = x2.shape
    tm = _OUT_ROWS
    n_tiles = rows // tm
    assert edge.shape[0] == n_tiles, "edge rows are saved per projection tile"
    conv_width = conv.shape[1]
    tile = lambda i: jnp.minimum(i, n_tiles - 1)
    row_tile = lambda width: pl.BlockSpec((tm, width), lambda i: (tile(i), 0))
    full = lambda shape: pl.BlockSpec(shape, lambda i: (0,) * len(shape))
    edge_of = lambda shift: pl.BlockSpec((1, _EDGE_ROWS, conv_width),
                                         lambda i: (jnp.clip(tile(i) + shift, 0, n_tiles - 1), 0, 0))
    return pl.pallas_call(
        functools.partial(_out_kernel, tm=tm, seq=seq, n_tiles=n_tiles),
        grid=(n_tiles + 1,),
        in_specs=[
            row_tile(ATTN_WIDTH),
            row_tile(conv_width),
            edge_of(-1), edge_of(0), edge_of(1),
            row_tile(d_model),
            pl.BlockSpec(memory_space=pl.ANY),
            full(conv_w.shape), full(conv_b.shape), full(norm_final.shape),
        ],
        out_specs=pl.BlockSpec((tm, d_model), lambda i: (jnp.maximum(i - 1, 0), 0)),
        out_shape=jax.ShapeDtypeStruct((rows, d_model), f32),
        scratch_shapes=_weight_stage_scratch(w) + [pltpu.VMEM((2, tm, d_model), f32)],
        compiler_params=pltpu.CompilerParams(
            dimension_semantics=("arbitrary",),
            vmem_limit_bytes=_VMEM_LIMIT),
        name="out_projection",
    )(attn, conv, edge, edge, edge, x2, w, conv_w, conv_b, norm_final)


def kernel(x, norm_in, w_in, q_norm, k_norm, conv_w, conv_b, w_out, norm_final):
    batch, seq, d_model = x.shape
    assert norm_in.shape[0] == 1, "single-layer block"
    assert _PROJ_ROWS == _OUT_ROWS, "conv edge rows are exchanged per row tile"
    assert seq % _PROJ_ROWS == 0 and seq % _ATTN_TK == 0 and seq % (_ATTN_TQ * _ATTN_TILES_PER_ITER) == 0
    x2 = x.reshape(batch * seq, d_model)
    cos_t, sin_t = _rope_tables(seq)
    q, k, v, sg, conv, edge = _in_projection(
        x2, norm_in, w_in[0], q_norm, k_norm, cos_t, sin_t, conv_w[0], conv_b, batch, seq)
    attn = _attention(q, k, v, sg, q_norm, k_norm, batch, seq)
    out = _out_projection(attn, conv, edge, x2, w_out[0], conv_w[0], conv_b,
                          norm_final.reshape(1, d_model), seq)
    return out.reshape(batch, seq, d_model)
```

```python
import functools

import jax
import jax.numpy as jnp
import numpy as np
from jax import lax
from jax.experimental import pallas as pl
from jax.experimental.pallas import tpu as pltpu

HEAD_DIM = 128
N_Q_HEADS = 8
N_KV_HEADS = 2
GQA_GROUP = N_Q_HEADS // N_KV_HEADS
ATTN_WIDTH = N_Q_HEADS * HEAD_DIM
CONV_K = 3
GRID_W = 64
ROPE_THETA = 10000.0
ROPE_AXIS_DIM = HEAD_DIM // 2
EPS = 1e-6
_LOG2_E = 1.4426950408889634
_SCORE_BOUND = 64.0

_V7X_VMEM_BYTES = 64 * 2**20
_VMEM_LIMIT = _V7X_VMEM_BYTES - 8 * 2**20
_BF16_SUBLANES = 16
_F32_SUBLANES = 8
_EDGE_ROWS = 4 * _F32_SUBLANES
_W_BLOCK = 256
_W_STAGE_BYTES = 2 * 2**20
_W_STAGE_SLOTS = 4

_PROJ_ROWS = 512
_ATTN_TQ = 128
_ATTN_TK = 1024
_ATTN_TILES_PER_ITER = 8
_OUT_ROWS = 512

f32 = jnp.float32
bf16 = jnp.bfloat16


def _silu(x):
    return x / (1.0 + jnp.exp(-x))


def _head_norm(x, g):
    ms = jnp.mean(x * x, axis=-1, keepdims=True)
    return x * lax.rsqrt(ms + EPS) * g


def _rope(x, c, s):
    lane = lax.broadcasted_iota(jnp.int32, x.shape, 1)
    first_half = (lane & (ROPE_AXIS_DIM // 2)) == 0
    swapped = jnp.where(first_half,
                        pltpu.roll(x, HEAD_DIM - ROPE_AXIS_DIM // 2, 1),
                        pltpu.roll(x, ROPE_AXIS_DIM // 2, 1))
    return x * c + swapped * s


def _rope_t(x_t, cos_tt, sin_tt):
    q = ROPE_AXIS_DIM // 2
    swapped = jnp.concatenate([x_t[q:2 * q], x_t[0:q], x_t[3 * q:4 * q], x_t[2 * q:3 * q]], axis=0)
    return x_t * cos_tt + swapped * sin_tt


def _rope_tables(seq_len):
    rows = seq_len // GRID_W
    row = np.repeat(np.arange(rows, dtype=np.float64), GRID_W)
    col = np.tile(np.arange(GRID_W, dtype=np.float64), rows)
    inv_freq = ROPE_THETA ** (-np.arange(0, ROPE_AXIS_DIM, 2, dtype=np.float64) / ROPE_AXIS_DIM)
    ang_r = row[:, None] * inv_freq[None, :]
    ang_c = col[:, None] * inv_freq[None, :]
    cr, sr, cc, sc = np.cos(ang_r), np.sin(ang_r), np.cos(ang_c), np.sin(ang_c)
    cos_t = np.concatenate([cr, cr, cc, cc], axis=-1).astype(np.float32)
    sin_t = np.concatenate([-sr, sr, -sc, sc], axis=-1).astype(np.float32)
    return cos_t, sin_t


def _conv_taps(u_prev, u_mid, u_next, cw, cb):
    y = cb + u_prev * cw[0:1, :]
    y = y + u_mid * cw[1:2, :]
    return y + u_next * cw[2:3, :]


_Q_BLOCK0, _K_BLOCK, _V_BLOCK, _G_BLOCK0 = 0, 4, 5, 6
_CB_BLOCK0, _CC_BLOCK0, _CX_BLOCK0, _GC_BLOCK0 = 10, 14, 18, 22
_N_CONV_BLOCKS = 4


def _weight_stage_scratch(w):
    rows = 1 << ((_W_STAGE_BYTES // (4 * w.shape[1])).bit_length() - 1)
    assert w.shape[0] % rows == 0 and rows % _BF16_SUBLANES == 0
    stage = (_W_STAGE_SLOTS, rows, w.shape[1])
    return [pltpu.VMEM(w.shape, bf16), pltpu.VMEM(stage, f32), pltpu.SemaphoreType.DMA((_W_STAGE_SLOTS,))]


def _load_weights_as_bf16(w_hbm, w_ref, stage_ref, sem):
    slots, chunk = stage_ref.shape[0], stage_ref.shape[1]
    n_chunks = w_ref.shape[0] // chunk

    def chunk_rows(c):
        start = c * chunk
        return pl.ds(start if isinstance(start, int) else pl.multiple_of(start, chunk), chunk)

    def copy(c):
        slot = c % slots
        return pltpu.make_async_copy(w_hbm.at[chunk_rows(c), :], stage_ref.at[slot], sem.at[slot])

    for c in range(min(slots - 1, n_chunks)):
        copy(c).start()

    def convert(c, carry):
        @pl.when(c + slots - 1 < n_chunks)
        def _():
            copy(c + slots - 1).start()

        copy(c).wait()
        w_ref[chunk_rows(c), :] = stage_ref[c % slots].astype(bf16)
        return carry

    lax.fori_loop(0, n_chunks, convert, 0)


def _proj_kernel(x_ref, nin_ref, w_hbm, qg_ref, kn_ref, cos_ref, sin_ref, cost_ref, sint_ref, cw_ref, cb_ref, wo_ref,
                 q_ref, k_ref, v_ref, sg_ref, conv_ref, edge_ref, wob_ref, hn_ref, w_ref, stage_ref, sem):
    tm = hn_ref.shape[0]
    wob_ref[...] = wo_ref[...].astype(bf16)

    @pl.when(pl.program_id(0) == 0)
    def _():
        _load_weights_as_bf16(w_hbm, w_ref, stage_ref, sem)

    x = x_ref[...]
    ms = jnp.mean(x * x, axis=-1, keepdims=True)
    hn_ref[...] = (x * lax.rsqrt(ms + EPS) * nin_ref[...]).astype(bf16)

    def proj(block):
        w = w_ref[:, block * _W_BLOCK:(block + 1) * _W_BLOCK]
        return jnp.dot(hn_ref[...], w, preferred_element_type=f32)

    def head(mat, hh):
        return mat[:, hh * HEAD_DIM:(hh + 1) * HEAD_DIM]

    kk = proj(_K_BLOCK)
    for hh in range(N_KV_HEADS):
        k_ref[0, hh] = _rope(_head_norm(head(kk, hh), kn_ref[...]), cos_ref[...], sin_ref[...]).astype(bf16)

    cos_tt, sin_tt = cost_ref[...], sint_ref[...]
    gain = jnp.tile(jnp.broadcast_to(qg_ref[...], (HEAD_DIM, HEAD_DIM)).T, (1, tm // HEAD_DIM))
    scale = HEAD_DIM ** -0.5 * _LOG2_E
    for pair in range(N_Q_HEADS // 2):
        qq = proj(_Q_BLOCK0 + pair)
        for hh in range(2):
            q_t = head(qq, hh).T
            ms = jnp.mean(q_t * q_t, axis=0, keepdims=True)
            r = lax.rsqrt(ms + EPS) * scale
            q_ref[0, 2 * pair + hh] = (_rope_t(q_t * gain, cos_tt, sin_tt) * r).astype(bf16)

    vv = proj(_V_BLOCK)
    for hh in range(N_KV_HEADS):
        v_ref[0, hh] = head(vv, hh).T.astype(bf16)

    for c in range(ATTN_WIDTH // _W_BLOCK):
        sg_ref[:, c * _W_BLOCK:(c + 1) * _W_BLOCK] = _silu(proj(_G_BLOCK0 + c)).astype(bf16)

    g = _F32_SUBLANES
    for c in range(_N_CONV_BLOCKS):
        cols = slice(c * _W_BLOCK, (c + 1) * _W_BLOCK)
        u = proj(_CC_BLOCK0 + c) * proj(_CX_BLOCK0 + c)
        e = proj(_CB_BLOCK0 + c) * _silu(proj(_GC_BLOCK0 + c))
        y = _conv_taps(pltpu.roll(u, 1, 0), u, pltpu.roll(u, tm - 1, 0), cw_ref[:, cols], cb_ref[:, cols])
        conv_ref[:, cols] = (e * y).astype(bf16)
        edge_ref[0, 0 * g:1 * g, cols] = u[:g]
        edge_ref[0, 1 * g:2 * g, cols] = u[tm - g:]
        edge_ref[0, 2 * g:3 * g, cols] = e[:g]
        edge_ref[0, 3 * g:4 * g, cols] = e[tm - g:]


def _in_projection(x2, norm_in, w, q_norm, k_norm, cos_t, sin_t, conv_w, conv_b, w_out, batch, seq):
    rows, d_model = x2.shape
    tm = _PROJ_ROWS
    nt = seq // tm
    wo_rows = w_out.shape[0] // (rows // tm)
    assert wo_rows % _BF16_SUBLANES == 0 and wo_rows * (rows // tm) == w_out.shape[0]
    wo_slab = pl.BlockSpec((wo_rows, w_out.shape[1]), lambda i: (i, 0))
    full = lambda shape: pl.BlockSpec(shape, lambda i: (0,) * len(shape))
    table = pl.BlockSpec((tm, HEAD_DIM), lambda i: (i % nt, 0))
    table_t = pl.BlockSpec((HEAD_DIM, tm), lambda i: (0, i % nt))
    row_tile = lambda width: pl.BlockSpec((tm, width), lambda i: (i, 0))
    conv_width = _N_CONV_BLOCKS * _W_BLOCK
    out_shape = (
        jax.ShapeDtypeStruct((batch, N_Q_HEADS, HEAD_DIM, seq), bf16),
        jax.ShapeDtypeStruct((batch, N_KV_HEADS, seq, HEAD_DIM), bf16),
        jax.ShapeDtypeStruct((batch, N_KV_HEADS, HEAD_DIM, seq), bf16),
        jax.ShapeDtypeStruct((rows, ATTN_WIDTH), bf16),
        jax.ShapeDtypeStruct((rows, conv_width), bf16),
        jax.ShapeDtypeStruct((rows // tm, _EDGE_ROWS, conv_width), f32),
        jax.ShapeDtypeStruct(w_out.shape, bf16),
    )
    out_specs = (
        pl.BlockSpec((1, N_Q_HEADS, HEAD_DIM, tm), lambda i: (i // nt, 0, 0, i % nt)),
        pl.BlockSpec((1, N_KV_HEADS, tm, HEAD_DIM), lambda i: (i // nt, 0, i % nt, 0)),
        pl.BlockSpec((1, N_KV_HEADS, HEAD_DIM, tm), lambda i: (i // nt, 0, 0, i % nt)),
        row_tile(ATTN_WIDTH), row_tile(conv_width),
        pl.BlockSpec((1, _EDGE_ROWS, conv_width), lambda i: (i, 0, 0)),
        wo_slab,
    )
    return pl.pallas_call(
        _proj_kernel,
        grid=(rows // tm,),
        in_specs=[
            row_tile(d_model),
            full((1, d_model)),
            pl.BlockSpec(memory_space=pl.ANY),
            full((1, HEAD_DIM)), full((1, HEAD_DIM)),
            table, table, table_t, table_t,
            full(conv_w.shape), full(conv_b.shape),
            wo_slab,
        ],
        out_specs=out_specs,
        out_shape=out_shape,
        scratch_shapes=[pltpu.VMEM((tm, d_model), bf16)] + _weight_stage_scratch(w),
        compiler_params=pltpu.CompilerParams(
            dimension_semantics=("arbitrary",),
            vmem_limit_bytes=_VMEM_LIMIT),
        name="in_projection",
    )(x2, norm_in, w, q_norm, k_norm, cos_t, sin_t, np.ascontiguousarray(cos_t.T), np.ascontiguousarray(sin_t.T),
      conv_w, conv_b, w_out)


def _attn_kernel(q_ref, k_ref, v_ref, sg_ref, o_ref, acc_ref, *maybe_s_ref, tq, tk, nk, nq, bounded):
    cols = GQA_GROUP * tq

    def q_tile(i):
        off = pl.multiple_of(i * tq, tq)
        return jnp.concatenate([q_ref[0, h, :, pl.ds(off, tq)] for h in range(GQA_GROUP)], axis=1)

    def key_chunk(j):
        return k_ref[0, 0, pl.ds(pl.multiple_of(j * tk, tk), tk), :]

    ones_rows = jnp.ones((_BF16_SUBLANES, tk), bf16)

    def value_chunk(j):
        off = pl.multiple_of(j * tk, tk)
        return jnp.concatenate([v_ref[0, 0, :, pl.ds(off, tk)], ones_rows], axis=0)

    def finish_tile(i, acc):
        o_t = acc[:HEAD_DIM, :] / acc[HEAD_DIM:HEAD_DIM + 1, :]
        rows = pl.ds(pl.multiple_of(i * tq, tq), tq)
        for h in range(GQA_GROUP):
            lanes = slice(h * HEAD_DIM, (h + 1) * HEAD_DIM)
            gate = sg_ref[rows, lanes].astype(f32)
            o_ref[rows, lanes] = (o_t[:, h * tq:(h + 1) * tq].T * gate).astype(bf16)

    if bounded:
        def tile_group(g, carry):
            for t in range(_ATTN_TILES_PER_ITER):
                i = g * _ATTN_TILES_PER_ITER + t
                q_t = q_tile(i)
                acc = acc_ref.at[t]
                for c in range(nk):
                    p_t = jnp.exp2(jnp.dot(key_chunk(c), q_t, preferred_element_type=f32)).astype(bf16)
                    pv_t = jnp.dot(value_chunk(c), p_t, preferred_element_type=f32)
                    acc[...] = pv_t if c == 0 else acc[...] + pv_t
                finish_tile(i, acc)
            return carry

        lax.fori_loop(0, nq // _ATTN_TILES_PER_ITER, tile_group, 0)
        return

    s_ref, = maybe_s_ref

    def scores(q_mat, j, slot):
        s_t = jnp.dot(key_chunk(j), q_mat, preferred_element_type=f32)
        s_ref[slot] = s_t
        return jnp.max(s_t, axis=0, keepdims=True)

    def step(j, slot, acc, m_prev, chunk_max, q_ahead, j_ahead):
        ahead_max = scores(q_ahead, j_ahead, 1 - slot)
        m_new = jnp.maximum(m_prev, chunk_max)
        alpha = jnp.exp2(m_prev - m_new)
        p_t = jnp.exp2(s_ref[slot] - m_new).astype(bf16)
        pv_t = jnp.dot(value_chunk(j), p_t, preferred_element_type=f32)
        acc[...] = alpha * acc[...] + pv_t
        return m_new, ahead_max

    def query_tile(i, acc, chunk_max):
        q_t = q_tile(i)
        q_next = q_tile(jnp.minimum(i + 1, nq - 1))
        acc[...] = jnp.zeros(acc.shape, f32)
        m_run = jnp.full((1, cols), -jnp.inf, f32)
        for c in range(nk):
            last = c == nk - 1
            m_run, chunk_max = step(c, c % 2, acc, m_run, chunk_max, q_next if last else q_t, 0 if last else c + 1)
        finish_tile(i, acc)
        return chunk_max

    def tile_group(g, chunk_max):
        for t in range(_ATTN_TILES_PER_ITER):
            chunk_max = query_tile(g * _ATTN_TILES_PER_ITER + t, acc_ref.at[t], chunk_max)
        return chunk_max

    lax.fori_loop(0, nq // _ATTN_TILES_PER_ITER, tile_group, scores(q_tile(0), 0, 0))


def _attention_call(q_t, k, v_t, sg, batch, seq, bounded):
    tq, tk = _ATTN_TQ, _ATTN_TK
    cols = GQA_GROUP * tq
    group_width = GQA_GROUP * HEAD_DIM
    gate_spec = pl.BlockSpec((seq, group_width), lambda b, h: (b, h))
    scratch = [pltpu.VMEM((_ATTN_TILES_PER_ITER, HEAD_DIM + _BF16_SUBLANES, cols), f32)]
    if not bounded:
        scratch.append(pltpu.VMEM((2, tk, cols), f32))
    return pl.pallas_call(
        functools.partial(_attn_kernel, tq=tq, tk=tk, nk=seq // tk, nq=seq // tq, bounded=bounded),
        grid=(batch, N_KV_HEADS),
        in_specs=[
            pl.BlockSpec((1, GQA_GROUP, HEAD_DIM, seq), lambda b, h: (b, h, 0, 0)),
            pl.BlockSpec((1, 1, seq, HEAD_DIM), lambda b, h: (b, h, 0, 0)),
            pl.BlockSpec((1, 1, HEAD_DIM, seq), lambda b, h: (b, h, 0, 0)),
            gate_spec,
        ],
        out_specs=gate_spec,
        out_shape=jax.ShapeDtypeStruct((batch * seq, ATTN_WIDTH), bf16),
        scratch_shapes=scratch,
        compiler_params=pltpu.CompilerParams(
            dimension_semantics=("arbitrary", "arbitrary"),
            vmem_limit_bytes=_VMEM_LIMIT),
        name="gqa_attention_bounded" if bounded else "gqa_attention",
    )(q_t, k, v_t, sg)


def _attention(q_t, k, v_t, sg, q_norm, k_norm, batch, seq):
    g_max = jnp.max(jnp.abs(jnp.concatenate([q_norm, k_norm], axis=0)))
    bound = (_LOG2_E * HEAD_DIM ** 0.5 * (1.0 + 2.0 ** -7)) * g_max * g_max
    return lax.cond(bound <= _SCORE_BOUND,
                    functools.partial(_attention_call, batch=batch, seq=seq, bounded=True),
                    functools.partial(_attention_call, batch=batch, seq=seq, bounded=False),
                    q_t, k, v_t, sg)


def _out_kernel(attn_ref, conv_ref, eprev_ref, edge_ref, enext_ref, x_ref, w_ref, cw_ref, cb_ref, nf_ref,
                o_ref, *, tm, seq):
    i = pl.program_id(0)

    g = _F32_SUBLANES
    t0 = (i * tm) % seq
    cw, cb = cw_ref[...], cb_ref[...]
    row = lambda ref, r: ref[0, r:r + 1, :]
    u_before = jnp.where(t0 == 0, 0.0, row(eprev_ref, 2 * g - 1))
    u_after = jnp.where(t0 + tm == seq, 0.0, row(enext_ref, 0))
    first = row(edge_ref, 2 * g) * _conv_taps(u_before, row(edge_ref, 0), row(edge_ref, 1), cw, cb)
    last = row(edge_ref, 4 * g - 1) * _conv_taps(row(edge_ref, 2 * g - 2), row(edge_ref, 2 * g - 1), u_after, cw, cb)
    p = _BF16_SUBLANES
    sub = lax.broadcasted_iota(jnp.int32, (p, conv_ref.shape[1]), 0)
    top = jnp.where(sub == 0, first, conv_ref[:p, :].astype(f32)).astype(bf16)
    bottom = jnp.where(sub == p - 1, last, conv_ref[tm - p:, :].astype(f32)).astype(bf16)
    conv = jnp.concatenate([top, conv_ref[p:tm - p, :], bottom], axis=0)

    acc = jnp.dot(attn_ref[...], w_ref[:ATTN_WIDTH, :], preferred_element_type=f32)
    acc = acc + jnp.dot(conv, w_ref[ATTN_WIDTH:, :], preferred_element_type=f32)
    h = x_ref[...] + acc
    ms = jnp.mean(h * h, axis=-1, keepdims=True)
    o_ref[...] = h * lax.rsqrt(ms + EPS) * nf_ref[...]


def _out_projection(attn, conv, edge, x2, w, conv_w, conv_b, norm_final, seq):
    rows, d_model = x2.shape
    tm = _OUT_ROWS
    n_tiles = rows // tm
    assert edge.shape[0] == n_tiles, "edge rows are saved per projection tile"
    conv_width = conv.shape[1]
    row_tile = lambda width: pl.BlockSpec((tm, width), lambda i: (i, 0))
    full = lambda shape: pl.BlockSpec(shape, lambda i: (0,) * len(shape))
    edge_of = lambda shift: pl.BlockSpec((1, _EDGE_ROWS, conv_width),
                                         lambda i: (jnp.clip(i + shift, 0, n_tiles - 1), 0, 0))
    return pl.pallas_call(
        functools.partial(_out_kernel, tm=tm, seq=seq),
        grid=(n_tiles,),
        in_specs=[
            row_tile(ATTN_WIDTH),
            row_tile(conv_width),
            edge_of(-1), edge_of(0), edge_of(1),
            row_tile(d_model),
            pl.BlockSpec(w.shape, lambda i: (0, 0), pipeline_mode=pl.Buffered(1)),
            full(conv_w.shape), full(conv_b.shape), full(norm_final.shape),
        ],
        out_specs=row_tile(d_model),
        out_shape=jax.ShapeDtypeStruct((rows, d_model), f32),
        compiler_params=pltpu.CompilerParams(
            dimension_semantics=("arbitrary",),
            vmem_limit_bytes=_VMEM_LIMIT),
        name="out_projection",
    )(attn, conv, edge, edge, edge, x2, w, conv_w, conv_b, norm_final)


def kernel(x, norm_in, w_in, q_norm, k_norm, conv_w, conv_b, w_out, norm_final):
    batch, seq, d_model = x.shape
    assert norm_in.shape[0] == 1, "single-layer block"
    assert _PROJ_ROWS == _OUT_ROWS, "conv edge rows are exchanged per row tile"
    assert seq % _PROJ_ROWS == 0 and seq % _ATTN_TK == 0 and seq % (_ATTN_TQ * _ATTN_TILES_PER_ITER) == 0
    x2 = x.reshape(batch * seq, d_model)
    cos_t, sin_t = _rope_tables(seq)
    q, k, v, sg, conv, edge, w_out_bf16 = _in_projection(
        x2, norm_in, w_in[0], q_norm, k_norm, cos_t, sin_t, conv_w[0], conv_b, w_out[0], batch, seq)
    attn = _attention(q, k, v, sg, q_norm, k_norm, batch, seq)
    out = _out_projection(attn, conv, edge, x2, w_out_bf16, conv_w[0], conv_b,
                          norm_final.reshape(1, d_model), seq)
    return out.reshape(batch, seq, d_model)
```

```python
import functools

import jax
import jax.numpy as jnp
import numpy as np
from jax import lax
from jax.experimental import pallas as pl
from jax.experimental.pallas import tpu as pltpu

HEAD_DIM = 128
N_Q_HEADS = 8
N_KV_HEADS = 2
GQA_GROUP = N_Q_HEADS // N_KV_HEADS
ATTN_WIDTH = N_Q_HEADS * HEAD_DIM
CONV_K = 3
GRID_W = 64
ROPE_THETA = 10000.0
ROPE_AXIS_DIM = HEAD_DIM // 2
EPS = 1e-6
_LOG2_E = 1.4426950408889634
_SCORE_BOUND = 64.0

_V7X_VMEM_BYTES = 64 * 2**20
_VMEM_LIMIT = _V7X_VMEM_BYTES - 8 * 2**20
_BF16_SUBLANES = 16
_F32_SUBLANES = 8
_EDGE_ROWS = 4 * _F32_SUBLANES
_W_BLOCK = 256
_W_STAGE_BYTES = 2 * 2**20
_W_STAGE_SLOTS = 4

_PROJ_ROWS = 512
_PROJ_HEAD_CHUNKS = 2
_ATTN_TQ = 128
_ATTN_TK = 1024
_ATTN_TILES_PER_ITER = 8
_OUT_ROWS = 512

f32 = jnp.float32
bf16 = jnp.bfloat16


def _silu(x):
    return x / (1.0 + jnp.exp(-x))


def _head_norm(x, g):
    ms = jnp.mean(x * x, axis=-1, keepdims=True)
    return x * lax.rsqrt(ms + EPS) * g


def _rope(x, c, s):
    lane = lax.broadcasted_iota(jnp.int32, x.shape, 1)
    first_half = (lane & (ROPE_AXIS_DIM // 2)) == 0
    swapped = jnp.where(first_half,
                        pltpu.roll(x, HEAD_DIM - ROPE_AXIS_DIM // 2, 1),
                        pltpu.roll(x, ROPE_AXIS_DIM // 2, 1))
    return x * c + swapped * s


def _rope_t(x_t, cos_tt, sin_tt):
    q = ROPE_AXIS_DIM // 2
    swapped = jnp.concatenate([x_t[q:2 * q], x_t[0:q], x_t[3 * q:4 * q], x_t[2 * q:3 * q]], axis=0)
    return x_t * cos_tt + swapped * sin_tt


def _rope_tables(seq_len):
    rows = seq_len // GRID_W
    row = np.repeat(np.arange(rows, dtype=np.float64), GRID_W)
    col = np.tile(np.arange(GRID_W, dtype=np.float64), rows)
    inv_freq = ROPE_THETA ** (-np.arange(0, ROPE_AXIS_DIM, 2, dtype=np.float64) / ROPE_AXIS_DIM)
    ang_r = row[:, None] * inv_freq[None, :]
    ang_c = col[:, None] * inv_freq[None, :]
    cr, sr, cc, sc = np.cos(ang_r), np.sin(ang_r), np.cos(ang_c), np.sin(ang_c)
    cos_t = np.concatenate([cr, cr, cc, cc], axis=-1).astype(np.float32)
    sin_t = np.concatenate([-sr, sr, -sc, sc], axis=-1).astype(np.float32)
    return cos_t, sin_t


def _conv_taps(u_prev, u_mid, u_next, cw, cb):
    y = cb + u_prev * cw[0:1, :]
    y = y + u_mid * cw[1:2, :]
    return y + u_next * cw[2:3, :]


_Q_BLOCK0, _K_BLOCK, _V_BLOCK, _G_BLOCK0 = 0, 4, 5, 6
_CB_BLOCK0, _CC_BLOCK0, _CX_BLOCK0, _GC_BLOCK0 = 10, 14, 18, 22
_N_CONV_BLOCKS = 4


def _weight_stage_scratch(w):
    rows = 1 << ((_W_STAGE_BYTES // (4 * w.shape[1])).bit_length() - 1)
    assert w.shape[0] % rows == 0 and rows % _BF16_SUBLANES == 0
    stage = (_W_STAGE_SLOTS, rows, w.shape[1])
    return [pltpu.VMEM(w.shape, bf16), pltpu.VMEM(stage, f32), pltpu.SemaphoreType.DMA((_W_STAGE_SLOTS,))]


def _load_weights_as_bf16(w_hbm, w_ref, stage_ref, sem):
    slots, chunk = stage_ref.shape[0], stage_ref.shape[1]
    n_chunks = w_ref.shape[0] // chunk

    def chunk_rows(c):
        start = c * chunk
        return pl.ds(start if isinstance(start, int) else pl.multiple_of(start, chunk), chunk)

    def copy(c):
        slot = c % slots
        return pltpu.make_async_copy(w_hbm.at[chunk_rows(c), :], stage_ref.at[slot], sem.at[slot])

    for c in range(min(slots - 1, n_chunks)):
        copy(c).start()

    def convert(c, carry):
        @pl.when(c + slots - 1 < n_chunks)
        def _():
            copy(c + slots - 1).start()

        copy(c).wait()
        w_ref[chunk_rows(c), :] = stage_ref[c % slots].astype(bf16)
        return carry

    lax.fori_loop(0, n_chunks, convert, 0)


def _proj_kernel(x_ref, nin_ref, w_hbm, qg_ref, kn_ref, cos_ref, sin_ref, cost_ref, sint_ref, cw_ref, cb_ref, wo_ref,
                 q_ref, k_ref, v_ref, sg_ref, conv_ref, edge_ref, wob_ref, hn_ref, w_ref, stage_ref, sem):
    tm = hn_ref.shape[0]
    wob_ref[...] = wo_ref[...].astype(bf16)

    @pl.when(pl.program_id(0) == 0)
    def _():
        _load_weights_as_bf16(w_hbm, w_ref, stage_ref, sem)

    def weight_block(block):
        return w_ref[:, block * _W_BLOCK:(block + 1) * _W_BLOCK]

    def proj(block):
        return jnp.dot(hn_ref[...], weight_block(block), preferred_element_type=f32)

    def head(mat, hh):
        return mat[:, hh * HEAD_DIM:(hh + 1) * HEAD_DIM]

    rc = tm // _PROJ_HEAD_CHUNKS
    kk, vv = [], []
    for c in range(_PROJ_HEAD_CHUNKS):
        rs = slice(c * rc, (c + 1) * rc)
        x = x_ref[rs, :]
        ms = jnp.mean(x * x, axis=-1, keepdims=True)
        hn = (x * lax.rsqrt(ms + EPS) * nin_ref[...]).astype(bf16)
        hn_ref[rs, :] = hn
        kk.append(jnp.dot(hn, weight_block(_K_BLOCK), preferred_element_type=f32))
        vv.append(jnp.dot(hn, weight_block(_V_BLOCK), preferred_element_type=f32))
    kk, vv = jnp.concatenate(kk, axis=0), jnp.concatenate(vv, axis=0)

    for hh in range(N_KV_HEADS):
        k_ref[0, hh] = _rope(_head_norm(head(kk, hh), kn_ref[...]), cos_ref[...], sin_ref[...]).astype(bf16)

    cos_tt, sin_tt = cost_ref[...], sint_ref[...]
    gain = jnp.tile(jnp.broadcast_to(qg_ref[...], (HEAD_DIM, HEAD_DIM)).T, (1, tm // HEAD_DIM))
    scale = HEAD_DIM ** -0.5 * _LOG2_E
    for pair in range(N_Q_HEADS // 2):
        qq = proj(_Q_BLOCK0 + pair)
        for hh in range(2):
            q_t = head(qq, hh).T
            ms = jnp.mean(q_t * q_t, axis=0, keepdims=True)
            r = lax.rsqrt(ms + EPS) * scale
            q_ref[0, 2 * pair + hh] = (_rope_t(q_t * gain, cos_tt, sin_tt) * r).astype(bf16)

    for hh in range(N_KV_HEADS):
        v_ref[0, hh] = head(vv, hh).T.astype(bf16)

    for c in range(ATTN_WIDTH // _W_BLOCK):
        sg_ref[:, c * _W_BLOCK:(c + 1) * _W_BLOCK] = _silu(proj(_G_BLOCK0 + c)).astype(bf16)

    g = _F32_SUBLANES
    for c in range(_N_CONV_BLOCKS):
        cols = slice(c * _W_BLOCK, (c + 1) * _W_BLOCK)
        u = proj(_CC_BLOCK0 + c) * proj(_CX_BLOCK0 + c)
        e = proj(_CB_BLOCK0 + c) * _silu(proj(_GC_BLOCK0 + c))
        y = _conv_taps(pltpu.roll(u, 1, 0), u, pltpu.roll(u, tm - 1, 0), cw_ref[:, cols], cb_ref[:, cols])
        conv_ref[:, cols] = (e * y).astype(bf16)
        edge_ref[0, 0 * g:1 * g, cols] = u[:g]
        edge_ref[0, 1 * g:2 * g, cols] = u[tm - g:]
        edge_ref[0, 2 * g:3 * g, cols] = e[:g]
        edge_ref[0, 3 * g:4 * g, cols] = e[tm - g:]


def _in_projection(x2, norm_in, w, q_norm, k_norm, cos_t, sin_t, conv_w, conv_b, w_out, batch, seq):
    rows, d_model = x2.shape
    tm = _PROJ_ROWS
    nt = seq // tm
    wo_rows = w_out.shape[0] // (rows // tm)
    assert wo_rows % _BF16_SUBLANES == 0 and wo_rows * (rows // tm) == w_out.shape[0]
    wo_slab = pl.BlockSpec((wo_rows, w_out.shape[1]), lambda i: (i, 0))
    full = lambda shape: pl.BlockSpec(shape, lambda i: (0,) * len(shape))
    table = pl.BlockSpec((tm, HEAD_DIM), lambda i: (i % nt, 0))
    table_t = pl.BlockSpec((HEAD_DIM, tm), lambda i: (0, i % nt))
    row_tile = lambda width: pl.BlockSpec((tm, width), lambda i: (i, 0))
    conv_width = _N_CONV_BLOCKS * _W_BLOCK
    out_shape = (
        jax.ShapeDtypeStruct((batch, N_Q_HEADS, HEAD_DIM, seq), bf16),
        jax.ShapeDtypeStruct((batch, N_KV_HEADS, seq, HEAD_DIM), bf16),
        jax.ShapeDtypeStruct((batch, N_KV_HEADS, HEAD_DIM, seq), bf16),
        jax.ShapeDtypeStruct((rows, ATTN_WIDTH), bf16),
        jax.ShapeDtypeStruct((rows, conv_width), bf16),
        jax.ShapeDtypeStruct((rows // tm, _EDGE_ROWS, conv_width), f32),
        jax.ShapeDtypeStruct(w_out.shape, bf16),
    )
    out_specs = (
        pl.BlockSpec((1, N_Q_HEADS, HEAD_DIM, tm), lambda i: (i // nt, 0, 0, i % nt)),
        pl.BlockSpec((1, N_KV_HEADS, tm, HEAD_DIM), lambda i: (i // nt, 0, i % nt, 0)),
        pl.BlockSpec((1, N_KV_HEADS, HEAD_DIM, tm), lambda i: (i // nt, 0, 0, i % nt)),
        row_tile(ATTN_WIDTH), row_tile(conv_width),
        pl.BlockSpec((1, _EDGE_ROWS, conv_width), lambda i: (i, 0, 0)),
        wo_slab,
    )
    return pl.pallas_call(
        _proj_kernel,
        grid=(rows // tm,),
        in_specs=[
            row_tile(d_model),
            full((1, d_model)),
            pl.BlockSpec(memory_space=pl.ANY),
            full((1, HEAD_DIM)), full((1, HEAD_DIM)),
            table, table, table_t, table_t,
            full(conv_w.shape), full(conv_b.shape),
            wo_slab,
        ],
        out_specs=out_specs,
        out_shape=out_shape,
        scratch_shapes=[pltpu.VMEM((tm, d_model), bf16)] + _weight_stage_scratch(w),
        compiler_params=pltpu.CompilerParams(
            dimension_semantics=("arbitrary",),
            vmem_limit_bytes=_VMEM_LIMIT),
        name="in_projection",
    )(x2, norm_in, w, q_norm, k_norm, cos_t, sin_t, np.ascontiguousarray(cos_t.T), np.ascontiguousarray(sin_t.T),
      conv_w, conv_b, w_out)


def _attn_kernel(q_ref, k_ref, v_ref, sg_ref, o_ref, acc_ref, *maybe_s_ref, tq, tk, nk, nq, bounded):
    cols = GQA_GROUP * tq

    def q_tile(i):
        off = pl.multiple_of(i * tq, tq)
        return jnp.concatenate([q_ref[0, h, :, pl.ds(off, tq)] for h in range(GQA_GROUP)], axis=1)

    def key_chunk(j):
        return k_ref[0, 0, pl.ds(pl.multiple_of(j * tk, tk), tk), :]

    ones_rows = jnp.ones((_BF16_SUBLANES, tk), bf16)

    def value_chunk(j):
        off = pl.multiple_of(j * tk, tk)
        return jnp.concatenate([v_ref[0, 0, :, pl.ds(off, tk)], ones_rows], axis=0)

    def finish_tile(i, acc):
        o_t = acc[:HEAD_DIM, :] / acc[HEAD_DIM:HEAD_DIM + 1, :]
        rows = pl.ds(pl.multiple_of(i * tq, tq), tq)
        for h in range(GQA_GROUP):
            lanes = slice(h * HEAD_DIM, (h + 1) * HEAD_DIM)
            gate = sg_ref[rows, lanes].astype(f32)
            o_ref[rows, lanes] = (o_t[:, h * tq:(h + 1) * tq].T * gate).astype(bf16)

    if bounded:
        def tile_group(g, carry):
            for t in range(_ATTN_TILES_PER_ITER):
                i = g * _ATTN_TILES_PER_ITER + t
                q_t = q_tile(i)
                acc = acc_ref.at[t]
                for c in range(nk):
                    p_t = jnp.exp2(jnp.dot(key_chunk(c), q_t, preferred_element_type=f32)).astype(bf16)
                    pv_t = jnp.dot(value_chunk(c), p_t, preferred_element_type=f32)
                    acc[...] = pv_t if c == 0 else acc[...] + pv_t
                finish_tile(i, acc)
            return carry

        lax.fori_loop(0, nq // _ATTN_TILES_PER_ITER, tile_group, 0)
        return

    s_ref, = maybe_s_ref

    def scores(q_mat, j, slot):
        s_t = jnp.dot(key_chunk(j), q_mat, preferred_element_type=f32)
        s_ref[slot] = s_t
        return jnp.max(s_t, axis=0, keepdims=True)

    def step(j, slot, acc, m_prev, chunk_max, q_ahead, j_ahead):
        ahead_max = scores(q_ahead, j_ahead, 1 - slot)
        m_new = jnp.maximum(m_prev, chunk_max)
        alpha = jnp.exp2(m_prev - m_new)
        p_t = jnp.exp2(s_ref[slot] - m_new).astype(bf16)
        pv_t = jnp.dot(value_chunk(j), p_t, preferred_element_type=f32)
        acc[...] = alpha * acc[...] + pv_t
        return m_new, ahead_max

    def query_tile(i, acc, chunk_max):
        q_t = q_tile(i)
        q_next = q_tile(jnp.minimum(i + 1, nq - 1))
        acc[...] = jnp.zeros(acc.shape, f32)
        m_run = jnp.full((1, cols), -jnp.inf, f32)
        for c in range(nk):
            last = c == nk - 1
            m_run, chunk_max = step(c, c % 2, acc, m_run, chunk_max, q_next if last else q_t, 0 if last else c + 1)
        finish_tile(i, acc)
        return chunk_max

    def tile_group(g, chunk_max):
        for t in range(_ATTN_TILES_PER_ITER):
            chunk_max = query_tile(g * _ATTN_TILES_PER_ITER + t, acc_ref.at[t], chunk_max)
        return chunk_max

    lax.fori_loop(0, nq // _ATTN_TILES_PER_ITER, tile_group, scores(q_tile(0), 0, 0))


def _attention_call(q_t, k, v_t, sg, batch, seq, bounded):
    tq, tk = _ATTN_TQ, _ATTN_TK
    cols = GQA_GROUP * tq
    group_width = GQA_GROUP * HEAD_DIM
    gate_spec = pl.BlockSpec((seq, group_width), lambda b, h: (b, h))
    scratch = [pltpu.VMEM((_ATTN_TILES_PER_ITER, HEAD_DIM + _BF16_SUBLANES, cols), f32)]
    if not bounded:
        scratch.append(pltpu.VMEM((2, tk, cols), f32))
    return pl.pallas_call(
        functools.partial(_attn_kernel, tq=tq, tk=tk, nk=seq // tk, nq=seq // tq, bounded=bounded),
        grid=(batch, N_KV_HEADS),
        in_specs=[
            pl.BlockSpec((1, GQA_GROUP, HEAD_DIM, seq), lambda b, h: (b, h, 0, 0)),
            pl.BlockSpec((1, 1, seq, HEAD_DIM), lambda b, h: (b, h, 0, 0)),
            pl.BlockSpec((1, 1, HEAD_DIM, seq), lambda b, h: (b, h, 0, 0)),
            gate_spec,
        ],
        out_specs=gate_spec,
        out_shape=jax.ShapeDtypeStruct((batch * seq, ATTN_WIDTH), bf16),
        scratch_shapes=scratch,
        compiler_params=pltpu.CompilerParams(
            dimension_semantics=("arbitrary", "arbitrary"),
            vmem_limit_bytes=_VMEM_LIMIT),
        name="gqa_attention_bounded" if bounded else "gqa_attention",
    )(q_t, k, v_t, sg)


def _attention(q_t, k, v_t, sg, q_norm, k_norm, batch, seq):
    g_max = jnp.max(jnp.abs(jnp.concatenate([q_norm, k_norm], axis=0)))
    bound = (_LOG2_E * HEAD_DIM ** 0.5 * (1.0 + 2.0 ** -7)) * g_max * g_max
    return lax.cond(bound <= _SCORE_BOUND,
                    functools.partial(_attention_call, batch=batch, seq=seq, bounded=True),
                    functools.partial(_attention_call, batch=batch, seq=seq, bounded=False),
                    q_t, k, v_t, sg)


def _out_kernel(attn_ref, conv_ref, eprev_ref, edge_ref, enext_ref, x_ref, w_ref, cw_ref, cb_ref, nf_ref,
                o_ref, *, tm, seq):
    i = pl.program_id(0)

    g = _F32_SUBLANES
    t0 = (i * tm) % seq
    cw, cb = cw_ref[...], cb_ref[...]
    row = lambda ref, r: ref[0, r:r + 1, :]
    u_before = jnp.where(t0 == 0, 0.0, row(eprev_ref, 2 * g - 1))
    u_after = jnp.where(t0 + tm == seq, 0.0, row(enext_ref, 0))
    first = row(edge_ref, 2 * g) * _conv_taps(u_before, row(edge_ref, 0), row(edge_ref, 1), cw, cb)
    last = row(edge_ref, 4 * g - 1) * _conv_taps(row(edge_ref, 2 * g - 2), row(edge_ref, 2 * g - 1), u_after, cw, cb)
    p = _BF16_SUBLANES
    sub = lax.broadcasted_iota(jnp.int32, (p, conv_ref.shape[1]), 0)
    top = jnp.where(sub == 0, first, conv_ref[:p, :].astype(f32)).astype(bf16)
    bottom = jnp.where(sub == p - 1, last, conv_ref[tm - p:, :].astype(f32)).astype(bf16)
    conv = jnp.concatenate([top, conv_ref[p:tm - p, :], bottom], axis=0)

    acc = jnp.dot(attn_ref[...], w_ref[:ATTN_WIDTH, :], preferred_element_type=f32)
    acc = acc + jnp.dot(conv, w_ref[ATTN_WIDTH:, :], preferred_element_type=f32)
    h = x_ref[...] + acc
    ms = jnp.mean(h * h, axis=-1, keepdims=True)
    o_ref[...] = h * lax.rsqrt(ms + EPS) * nf_ref[...]


def _out_projection(attn, conv, edge, x2, w, conv_w, conv_b, norm_final, seq):
    rows, d_model = x2.shape
    tm = _OUT_ROWS
    n_tiles = rows // tm
    assert edge.shape[0] == n_tiles, "edge rows are saved per projection tile"
    conv_width = conv.shape[1]
    row_tile = lambda width: pl.BlockSpec((tm, width), lambda i: (i, 0))
    full = lambda shape: pl.BlockSpec(shape, lambda i: (0,) * len(shape))
    edge_of = lambda shift: pl.BlockSpec((1, _EDGE_ROWS, conv_width),
                                         lambda i: (jnp.clip(i + shift, 0, n_tiles - 1), 0, 0))
    return pl.pallas_call(
        functools.partial(_out_kernel, tm=tm, seq=seq),
        grid=(n_tiles,),
        in_specs=[
            row_tile(ATTN_WIDTH),
            row_tile(conv_width),
            edge_of(-1), edge_of(0), edge_of(1),
            row_tile(d_model),
            pl.BlockSpec(w.shape, lambda i: (0, 0), pipeline_mode=pl.Buffered(1)),
            full(conv_w.shape), full(conv_b.shape), full(norm_final.shape),
        ],
        out_specs=row_tile(d_model),
        out_shape=jax.ShapeDtypeStruct((rows, d_model), f32),
        compiler_params=pltpu.CompilerParams(
            dimension_semantics=("arbitrary",),
            vmem_limit_bytes=_VMEM_LIMIT),
        name="out_projection",
    )(attn, conv, edge, edge, edge, x2, w, conv_w, conv_b, norm_final)


def kernel(x, norm_in, w_in, q_norm, k_norm, conv_w, conv_b, w_out, norm_final):
    batch, seq, d_model = x.shape
    assert norm_in.shape[0] == 1, "single-layer block"
    assert _PROJ_ROWS == _OUT_ROWS, "conv edge rows are exchanged per row tile"
    assert seq % _PROJ_ROWS == 0 and seq % _ATTN_TK == 0 and seq % (_ATTN_TQ * _ATTN_TILES_PER_ITER) == 0
    x2 = x.reshape(batch * seq, d_model)
    cos_t, sin_t = _rope_tables(seq)
    q, k, v, sg, conv, edge, w_out_bf16 = _in_projection(
        x2, norm_in, w_in[0], q_norm, k_norm, cos_t, sin_t, conv_w[0], conv_b, w_out[0], batch, seq)
    attn = _attention(q, k, v, sg, q_norm, k_norm, batch, seq)
    out = _out_projection(attn, conv, edge, x2, w_out_bf16, conv_w[0], conv_b,
                          norm_final.reshape(1, d_model), seq)
    return out.reshape(batch, seq, d_model)
```

```python
import functools

import jax
import jax.numpy as jnp
import numpy as np
from jax import lax
from jax.experimental import pallas as pl
from jax.experimental.pallas import tpu as pltpu

HEAD_DIM = 128
N_Q_HEADS = 8
N_KV_HEADS = 2
GQA_GROUP = N_Q_HEADS // N_KV_HEADS
ATTN_WIDTH = N_Q_HEADS * HEAD_DIM
CONV_K = 3
GRID_W = 64
ROPE_THETA = 10000.0
ROPE_AXIS_DIM = HEAD_DIM // 2
EPS = 1e-6
_LOG2_E = 1.4426950408889634
_SCORE_BOUND = 64.0

_V7X_VMEM_BYTES = 64 * 2**20
_VMEM_LIMIT = _V7X_VMEM_BYTES - 8 * 2**20
_BF16_SUBLANES = 16
_F32_SUBLANES = 8
_EDGE_ROWS = 4 * _F32_SUBLANES
_W_BLOCK = 256
_W_STAGE_BYTES = 2 * 2**20
_W_STAGE_SLOTS = 4

_PROJ_ROWS = 512
_ATTN_TQ = 128
_ATTN_TK = 1024
_ATTN_TILES_PER_ITER = 8
_ATTN_FALLBACK_TILES_PER_ITER = 2
_OUT_ROWS = 512

f32 = jnp.float32
bf16 = jnp.bfloat16


def _silu(x):
    return x / (1.0 + jnp.exp(-x))


def _head_norm(x, g):
    ms = jnp.mean(x * x, axis=-1, keepdims=True)
    return x * lax.rsqrt(ms + EPS) * g


def _rope(x, c, s):
    lane = lax.broadcasted_iota(jnp.int32, x.shape, 1)
    first_half = (lane & (ROPE_AXIS_DIM // 2)) == 0
    swapped = jnp.where(first_half,
                        pltpu.roll(x, HEAD_DIM - ROPE_AXIS_DIM // 2, 1),
                        pltpu.roll(x, ROPE_AXIS_DIM // 2, 1))
    return x * c + swapped * s


def _rope_t(x_t, cos_tt, sin_tt):
    q = ROPE_AXIS_DIM // 2
    swapped = jnp.concatenate([x_t[q:2 * q], x_t[0:q], x_t[3 * q:4 * q], x_t[2 * q:3 * q]], axis=0)
    return x_t * cos_tt + swapped * sin_tt


def _rope_tables(seq_len):
    rows = seq_len // GRID_W
    row = np.repeat(np.arange(rows, dtype=np.float64), GRID_W)
    col = np.tile(np.arange(GRID_W, dtype=np.float64), rows)
    inv_freq = ROPE_THETA ** (-np.arange(0, ROPE_AXIS_DIM, 2, dtype=np.float64) / ROPE_AXIS_DIM)
    ang_r = row[:, None] * inv_freq[None, :]
    ang_c = col[:, None] * inv_freq[None, :]
    cr, sr, cc, sc = np.cos(ang_r), np.sin(ang_r), np.cos(ang_c), np.sin(ang_c)
    cos_t = np.concatenate([cr, cr, cc, cc], axis=-1).astype(np.float32)
    sin_t = np.concatenate([-sr, sr, -sc, sc], axis=-1).astype(np.float32)
    return cos_t, sin_t


def _conv_taps(u_prev, u_mid, u_next, cw, cb):
    y = cb + u_prev * cw[0:1, :]
    y = y + u_mid * cw[1:2, :]
    return y + u_next * cw[2:3, :]


_Q_BLOCK0, _K_BLOCK, _V_BLOCK, _G_BLOCK0 = 0, 4, 5, 6
_CB_BLOCK0, _CC_BLOCK0, _CX_BLOCK0, _GC_BLOCK0 = 10, 14, 18, 22
_N_CONV_BLOCKS = 4


def _weight_stage_scratch(w):
    rows = 1 << ((_W_STAGE_BYTES // (4 * w.shape[1])).bit_length() - 1)
    assert w.shape[0] % rows == 0 and rows % _BF16_SUBLANES == 0
    stage = (_W_STAGE_SLOTS, rows, w.shape[1])
    return [pltpu.VMEM(w.shape, bf16), pltpu.VMEM(stage, f32), pltpu.SemaphoreType.DMA((_W_STAGE_SLOTS,))]


def _load_weights_as_bf16(w_hbm, w_ref, stage_ref, sem):
    slots, chunk = stage_ref.shape[0], stage_ref.shape[1]
    n_chunks = w_ref.shape[0] // chunk

    def chunk_rows(c):
        start = c * chunk
        return pl.ds(start if isinstance(start, int) else pl.multiple_of(start, chunk), chunk)

    def copy(c):
        slot = c % slots
        return pltpu.make_async_copy(w_hbm.at[chunk_rows(c), :], stage_ref.at[slot], sem.at[slot])

    for c in range(min(slots - 1, n_chunks)):
        copy(c).start()

    def convert(c, carry):
        @pl.when(c + slots - 1 < n_chunks)
        def _():
            copy(c + slots - 1).start()

        copy(c).wait()
        w_ref[chunk_rows(c), :] = stage_ref[c % slots].astype(bf16)
        return carry

    lax.fori_loop(0, n_chunks, convert, 0)


def _proj_kernel(x_ref, nin_ref, w_hbm, qg_ref, kn_ref, cos_ref, sin_ref, cost_ref, sint_ref, cw_ref, cb_ref, wo_ref,
                 q_ref, k_ref, v_ref, sg_ref, conv_ref, edge_ref, wob_ref, hn_ref, w_ref, stage_ref, sem):
    tm = hn_ref.shape[0]
    wob_ref[...] = wo_ref[...].astype(bf16)

    @pl.when(pl.program_id(0) == 0)
    def _():
        _load_weights_as_bf16(w_hbm, w_ref, stage_ref, sem)

    x = x_ref[...]
    ms = jnp.mean(x * x, axis=-1, keepdims=True)
    hn_ref[...] = (x * lax.rsqrt(ms + EPS) * nin_ref[...]).astype(bf16)

    def proj(block):
        w = w_ref[:, block * _W_BLOCK:(block + 1) * _W_BLOCK]
        return jnp.dot(hn_ref[...], w, preferred_element_type=f32)

    def head(mat, hh):
        return mat[:, hh * HEAD_DIM:(hh + 1) * HEAD_DIM]

    kk = proj(_K_BLOCK)
    for hh in range(N_KV_HEADS):
        k_ref[0, hh] = _rope(_head_norm(head(kk, hh), kn_ref[...]), cos_ref[...], sin_ref[...]).astype(bf16)

    cos_tt, sin_tt = cost_ref[...], sint_ref[...]
    gain = jnp.tile(jnp.broadcast_to(qg_ref[...], (HEAD_DIM, HEAD_DIM)).T, (1, tm // HEAD_DIM))
    scale = HEAD_DIM ** -0.5 * _LOG2_E
    for pair in range(N_Q_HEADS // 2):
        qq = proj(_Q_BLOCK0 + pair)
        for hh in range(2):
            q_t = head(qq, hh).T
            ms = jnp.mean(q_t * q_t, axis=0, keepdims=True)
            r = lax.rsqrt(ms + EPS) * scale
            q_ref[0, 2 * pair + hh] = (_rope_t(q_t * gain, cos_tt, sin_tt) * r).astype(bf16)

    vv = proj(_V_BLOCK)
    for hh in range(N_KV_HEADS):
        v_ref[0, hh] = head(vv, hh).T.astype(bf16)

    for c in range(ATTN_WIDTH // _W_BLOCK):
        sg_ref[:, c * _W_BLOCK:(c + 1) * _W_BLOCK] = _silu(proj(_G_BLOCK0 + c)).astype(bf16)

    g = _F32_SUBLANES
    for c in range(_N_CONV_BLOCKS):
        cols = slice(c * _W_BLOCK, (c + 1) * _W_BLOCK)
        u = proj(_CC_BLOCK0 + c) * proj(_CX_BLOCK0 + c)
        e = proj(_CB_BLOCK0 + c) * _silu(proj(_GC_BLOCK0 + c))
        y = _conv_taps(pltpu.roll(u, 1, 0), u, pltpu.roll(u, tm - 1, 0), cw_ref[:, cols], cb_ref[:, cols])
        conv_ref[:, cols] = (e * y).astype(bf16)
        edge_ref[0, 0 * g:1 * g, cols] = u[:g]
        edge_ref[0, 1 * g:2 * g, cols] = u[tm - g:]
        edge_ref[0, 2 * g:3 * g, cols] = e[:g]
        edge_ref[0, 3 * g:4 * g, cols] = e[tm - g:]


def _in_projection(x2, norm_in, w, q_norm, k_norm, cos_t, sin_t, conv_w, conv_b, w_out, batch, seq):
    rows, d_model = x2.shape
    tm = _PROJ_ROWS
    nt = seq // tm
    wo_rows = w_out.shape[0] // (rows // tm)
    assert wo_rows % _BF16_SUBLANES == 0 and wo_rows * (rows // tm) == w_out.shape[0]
    wo_slab = pl.BlockSpec((wo_rows, w_out.shape[1]), lambda i: (i, 0))
    full = lambda shape: pl.BlockSpec(shape, lambda i: (0,) * len(shape))
    table = pl.BlockSpec((tm, HEAD_DIM), lambda i: (i % nt, 0))
    table_t = pl.BlockSpec((HEAD_DIM, tm), lambda i: (0, i % nt))
    row_tile = lambda width: pl.BlockSpec((tm, width), lambda i: (i, 0))
    conv_width = _N_CONV_BLOCKS * _W_BLOCK
    out_shape = (
        jax.ShapeDtypeStruct((batch, N_Q_HEADS, HEAD_DIM, seq), bf16),
        jax.ShapeDtypeStruct((batch, N_KV_HEADS, seq, HEAD_DIM), bf16),
        jax.ShapeDtypeStruct((batch, N_KV_HEADS, HEAD_DIM, seq), bf16),
        jax.ShapeDtypeStruct((rows, ATTN_WIDTH), bf16),
        jax.ShapeDtypeStruct((rows, conv_width), bf16),
        jax.ShapeDtypeStruct((rows // tm, _EDGE_ROWS, conv_width), f32),
        jax.ShapeDtypeStruct(w_out.shape, bf16),
    )
    out_specs = (
        pl.BlockSpec((1, N_Q_HEADS, HEAD_DIM, tm), lambda i: (i // nt, 0, 0, i % nt)),
        pl.BlockSpec((1, N_KV_HEADS, tm, HEAD_DIM), lambda i: (i // nt, 0, i % nt, 0)),
        pl.BlockSpec((1, N_KV_HEADS, HEAD_DIM, tm), lambda i: (i // nt, 0, 0, i % nt)),
        row_tile(ATTN_WIDTH), row_tile(conv_width),
        pl.BlockSpec((1, _EDGE_ROWS, conv_width), lambda i: (i, 0, 0)),
        wo_slab,
    )
    return pl.pallas_call(
        _proj_kernel,
        grid=(rows // tm,),
        in_specs=[
            row_tile(d_model),
            full((1, d_model)),
            pl.BlockSpec(memory_space=pl.ANY),
            full((1, HEAD_DIM)), full((1, HEAD_DIM)),
            table, table, table_t, table_t,
            full(conv_w.shape), full(conv_b.shape),
            wo_slab,
        ],
        out_specs=out_specs,
        out_shape=out_shape,
        scratch_shapes=[pltpu.VMEM((tm, d_model), bf16)] + _weight_stage_scratch(w),
        compiler_params=pltpu.CompilerParams(
            dimension_semantics=("arbitrary",),
            vmem_limit_bytes=_VMEM_LIMIT),
        name="in_projection",
    )(x2, norm_in, w, q_norm, k_norm, cos_t, sin_t, np.ascontiguousarray(cos_t.T), np.ascontiguousarray(sin_t.T),
      conv_w, conv_b, w_out)


def _attn_kernel(q_ref, k_ref, v_ref, sg_ref, o_ref, acc_ref, *maybe_s_ref, tq, tk, nk, nq, bounded):
    cols = GQA_GROUP * tq

    def q_tile(i):
        off = pl.multiple_of(i * tq, tq)
        return jnp.concatenate([q_ref[0, h, :, pl.ds(off, tq)] for h in range(GQA_GROUP)], axis=1)

    def key_chunk(j):
        return k_ref[0, 0, pl.ds(pl.multiple_of(j * tk, tk), tk), :]

    ones_rows = jnp.ones((_BF16_SUBLANES, tk), bf16)

    def value_chunk(j):
        off = pl.multiple_of(j * tk, tk)
        return jnp.concatenate([v_ref[0, 0, :, pl.ds(off, tk)], ones_rows], axis=0)

    def finish_tile(i, acc):
        o_t = acc[:HEAD_DIM, :] / acc[HEAD_DIM:HEAD_DIM + 1, :]
        rows = pl.ds(pl.multiple_of(i * tq, tq), tq)
        for h in range(GQA_GROUP):
            lanes = slice(h * HEAD_DIM, (h + 1) * HEAD_DIM)
            gate = sg_ref[rows, lanes].astype(f32)
            o_ref[rows, lanes] = (o_t[:, h * tq:(h + 1) * tq].T * gate).astype(bf16)

    if bounded:
        def tile_group(g, carry):
            for t in range(_ATTN_TILES_PER_ITER):
                i = g * _ATTN_TILES_PER_ITER + t
                q_t = q_tile(i)
                acc = acc_ref.at[t]
                for c in range(nk):
                    p_t = jnp.exp2(jnp.dot(key_chunk(c), q_t, preferred_element_type=f32)).astype(bf16)
                    pv_t = jnp.dot(value_chunk(c), p_t, preferred_element_type=f32)
                    acc[...] = pv_t if c == 0 else acc[...] + pv_t
                finish_tile(i, acc)
            return carry

        lax.fori_loop(0, nq // _ATTN_TILES_PER_ITER, tile_group, 0)
        return

    s_ref, = maybe_s_ref

    def scores(q_mat, j, slot):
        s_t = jnp.dot(key_chunk(j), q_mat, preferred_element_type=f32)
        s_ref[slot] = s_t
        return jnp.max(s_t, axis=0, keepdims=True)

    def step(j, slot, acc, m_prev, chunk_max, q_ahead, j_ahead):
        ahead_max = scores(q_ahead, j_ahead, 1 - slot)
        m_new = jnp.maximum(m_prev, chunk_max)
        alpha = jnp.exp2(m_prev - m_new)
        p_t = jnp.exp2(s_ref[slot] - m_new).astype(bf16)
        pv_t = jnp.dot(value_chunk(j), p_t, preferred_element_type=f32)
        acc[...] = alpha * acc[...] + pv_t
        return m_new, ahead_max

    def query_tile(i, acc, chunk_max):
        q_t = q_tile(i)
        q_next = q_tile(jnp.minimum(i + 1, nq - 1))
        acc[...] = jnp.zeros(acc.shape, f32)
        m_run = jnp.full((1, cols), -jnp.inf, f32)
        for c in range(nk):
            last = c == nk - 1
            m_run, chunk_max = step(c, c % 2, acc, m_run, chunk_max, q_next if last else q_t, 0 if last else c + 1)
        finish_tile(i, acc)
        return chunk_max

    def tile_group(g, chunk_max):
        for t in range(_ATTN_FALLBACK_TILES_PER_ITER):
            chunk_max = query_tile(g * _ATTN_FALLBACK_TILES_PER_ITER + t, acc_ref.at[t], chunk_max)
        return chunk_max

    lax.fori_loop(0, nq // _ATTN_FALLBACK_TILES_PER_ITER, tile_group, scores(q_tile(0), 0, 0))


def _attention_call(q_t, k, v_t, sg, batch, seq, bounded):
    tq, tk = _ATTN_TQ, _ATTN_TK
    cols = GQA_GROUP * tq
    group_width = GQA_GROUP * HEAD_DIM
    gate_spec = pl.BlockSpec((seq, group_width), lambda b, h: (b, h))
    scratch = [pltpu.VMEM((_ATTN_TILES_PER_ITER, HEAD_DIM + _BF16_SUBLANES, cols), f32)]
    if not bounded:
        scratch.append(pltpu.VMEM((2, tk, cols), f32))
    return pl.pallas_call(
        functools.partial(_attn_kernel, tq=tq, tk=tk, nk=seq // tk, nq=seq // tq, bounded=bounded),
        grid=(batch, N_KV_HEADS),
        in_specs=[
            pl.BlockSpec((1, GQA_GROUP, HEAD_DIM, seq), lambda b, h: (b, h, 0, 0)),
            pl.BlockSpec((1, 1, seq, HEAD_DIM), lambda b, h: (b, h, 0, 0)),
            pl.BlockSpec((1, 1, HEAD_DIM, seq), lambda b, h: (b, h, 0, 0)),
            gate_spec,
        ],
        out_specs=gate_spec,
        out_shape=jax.ShapeDtypeStruct((batch * seq, ATTN_WIDTH), bf16),
        scratch_shapes=scratch,
        compiler_params=pltpu.CompilerParams(
            dimension_semantics=("arbitrary", "arbitrary"),
            vmem_limit_bytes=_VMEM_LIMIT),
        name="gqa_attention_bounded" if bounded else "gqa_attention",
    )(q_t, k, v_t, sg)


def _attention(q_t, k, v_t, sg, q_norm, k_norm, batch, seq):
    g_max = jnp.max(jnp.abs(jnp.concatenate([q_norm, k_norm], axis=0)))
    bound = (_LOG2_E * HEAD_DIM ** 0.5 * (1.0 + 2.0 ** -7)) * g_max * g_max
    return lax.cond(bound <= _SCORE_BOUND,
                    functools.partial(_attention_call, batch=batch, seq=seq, bounded=True),
                    functools.partial(_attention_call, batch=batch, seq=seq, bounded=False),
                    q_t, k, v_t, sg)


def _out_kernel(attn_ref, conv_ref, eprev_ref, edge_ref, enext_ref, x_ref, w_ref, cw_ref, cb_ref, nf_ref,
                o_ref, *, tm, seq):
    i = pl.program_id(0)

    g = _F32_SUBLANES
    t0 = (i * tm) % seq
    cw, cb = cw_ref[...], cb_ref[...]
    row = lambda ref, r: ref[0, r:r + 1, :]
    u_before = jnp.where(t0 == 0, 0.0, row(eprev_ref, 2 * g - 1))
    u_after = jnp.where(t0 + tm == seq, 0.0, row(enext_ref, 0))
    first = row(edge_ref, 2 * g) * _conv_taps(u_before, row(edge_ref, 0), row(edge_ref, 1), cw, cb)
    last = row(edge_ref, 4 * g - 1) * _conv_taps(row(edge_ref, 2 * g - 2), row(edge_ref, 2 * g - 1), u_after, cw, cb)
    p = _BF16_SUBLANES
    sub = lax.broadcasted_iota(jnp.int32, (p, conv_ref.shape[1]), 0)
    top = jnp.where(sub == 0, first, conv_ref[:p, :].astype(f32)).astype(bf16)
    bottom = jnp.where(sub == p - 1, last, conv_ref[tm - p:, :].astype(f32)).astype(bf16)
    conv = jnp.concatenate([top, conv_ref[p:tm - p, :], bottom], axis=0)

    acc = jnp.dot(attn_ref[...], w_ref[:ATTN_WIDTH, :], preferred_element_type=f32)
    acc = acc + jnp.dot(conv, w_ref[ATTN_WIDTH:, :], preferred_element_type=f32)
    h = x_ref[...] + acc
    ms = jnp.mean(h * h, axis=-1, keepdims=True)
    o_ref[...] = h * lax.rsqrt(ms + EPS) * nf_ref[...]


def _out_projection(attn, conv, edge, x2, w, conv_w, conv_b, norm_final, seq):
    rows, d_model = x2.shape
    tm = _OUT_ROWS
    n_tiles = rows // tm
    assert edge.shape[0] == n_tiles, "edge rows are saved per projection tile"
    conv_width = conv.shape[1]
    row_tile = lambda width: pl.BlockSpec((tm, width), lambda i: (i, 0))
    full = lambda shape: pl.BlockSpec(shape, lambda i: (0,) * len(shape))
    edge_of = lambda shift: pl.BlockSpec((1, _EDGE_ROWS, conv_width),
                                         lambda i: (jnp.clip(i + shift, 0, n_tiles - 1), 0, 0))
    return pl.pallas_call(
        functools.partial(_out_kernel, tm=tm, seq=seq),
        grid=(n_tiles,),
        in_specs=[
            row_tile(ATTN_WIDTH),
            row_tile(conv_width),
            edge_of(-1), edge_of(0), edge_of(1),
            row_tile(d_model),
            pl.BlockSpec(w.shape, lambda i: (0, 0), pipeline_mode=pl.Buffered(1)),
            full(conv_w.shape), full(conv_b.shape), full(norm_final.shape),
        ],
        out_specs=row_tile(d_model),
        out_shape=jax.ShapeDtypeStruct((rows, d_model), f32),
        compiler_params=pltpu.CompilerParams(
            dimension_semantics=("arbitrary",),
            vmem_limit_bytes=_VMEM_LIMIT),
        name="out_projection",
    )(attn, conv, edge, edge, edge, x2, w, conv_w, conv_b, norm_final)


def kernel(x, norm_in, w_in, q_norm, k_norm, conv_w, conv_b, w_out, norm_final):
    batch, seq, d_model = x.shape
    assert norm_in.shape[0] == 1, "single-layer block"
    assert _PROJ_ROWS == _OUT_ROWS, "conv edge rows are exchanged per row tile"
    assert seq % _PROJ_ROWS == 0 and seq % _ATTN_TK == 0 and seq % (_ATTN_TQ * _ATTN_TILES_PER_ITER) == 0
    x2 = x.reshape(batch * seq, d_model)
    cos_t, sin_t = _rope_tables(seq)
    q, k, v, sg, conv, edge, w_out_bf16 = _in_projection(
        x2, norm_in, w_in[0], q_norm, k_norm, cos_t, sin_t, conv_w[0], conv_b, w_out[0], batch, seq)
    attn = _attention(q, k, v, sg, q_norm, k_norm, batch, seq)
    out = _out_projection(attn, conv, edge, x2, w_out_bf16, conv_w[0], conv_b,
                          norm_final.reshape(1, d_model), seq)
    return out.reshape(batch, seq, d_model)
```

```python
import functools

import jax
import jax.numpy as jnp
import numpy as np
from jax import lax
from jax.experimental import pallas as pl
from jax.experimental.pallas import tpu as pltpu

HEAD_DIM = 128
N_Q_HEADS = 8
N_KV_HEADS = 2
GQA_GROUP = N_Q_HEADS // N_KV_HEADS
ATTN_WIDTH = N_Q_HEADS * HEAD_DIM
CONV_K = 3
GRID_W = 64
ROPE_THETA = 10000.0
ROPE_AXIS_DIM = HEAD_DIM // 2
EPS = 1e-6
_LOG2_E = 1.4426950408889634
_SCORE_BOUND = 64.0

_V7X_VMEM_BYTES = 64 * 2**20
_VMEM_LIMIT = _V7X_VMEM_BYTES - 8 * 2**20
_BF16_SUBLANES = 16
_F32_SUBLANES = 8
_EDGE_ROWS = 4 * _F32_SUBLANES
_W_BLOCK = 256
_W_STAGE_BYTES = 2 * 2**20
_W_STAGE_SLOTS = 4

_PROJ_ROWS = 512
_ATTN_TQ = 128
_ATTN_TK = 1024
_ATTN_TILES_PER_ITER = 16
_ATTN_FALLBACK_TILES_PER_ITER = 2
_OUT_ROWS = 512

f32 = jnp.float32
bf16 = jnp.bfloat16


def _silu(x):
    return x / (1.0 + jnp.exp(-x))


def _head_norm(x, g):
    ms = jnp.mean(x * x, axis=-1, keepdims=True)
    return x * lax.rsqrt(ms + EPS) * g


def _rope(x, c, s):
    lane = lax.broadcasted_iota(jnp.int32, x.shape, 1)
    first_half = (lane & (ROPE_AXIS_DIM // 2)) == 0
    swapped = jnp.where(first_half,
                        pltpu.roll(x, HEAD_DIM - ROPE_AXIS_DIM // 2, 1),
                        pltpu.roll(x, ROPE_AXIS_DIM // 2, 1))
    return x * c + swapped * s


def _rope_t(x_t, cos_tt, sin_tt):
    q = ROPE_AXIS_DIM // 2
    swapped = jnp.concatenate([x_t[q:2 * q], x_t[0:q], x_t[3 * q:4 * q], x_t[2 * q:3 * q]], axis=0)
    return x_t * cos_tt + swapped * sin_tt


def _rope_tables(seq_len):
    rows = seq_len // GRID_W
    row = np.repeat(np.arange(rows, dtype=np.float64), GRID_W)
    col = np.tile(np.arange(GRID_W, dtype=np.float64), rows)
    inv_freq = ROPE_THETA ** (-np.arange(0, ROPE_AXIS_DIM, 2, dtype=np.float64) / ROPE_AXIS_DIM)
    ang_r = row[:, None] * inv_freq[None, :]
    ang_c = col[:, None] * inv_freq[None, :]
    cr, sr, cc, sc = np.cos(ang_r), np.sin(ang_r), np.cos(ang_c), np.sin(ang_c)
    cos_t = np.concatenate([cr, cr, cc, cc], axis=-1).astype(np.float32)
    sin_t = np.concatenate([-sr, sr, -sc, sc], axis=-1).astype(np.float32)
    return cos_t, sin_t


def _conv_taps(u_prev, u_mid, u_next, cw, cb):
    y = cb + u_prev * cw[0:1, :]
    y = y + u_mid * cw[1:2, :]
    return y + u_next * cw[2:3, :]


_Q_BLOCK0, _K_BLOCK, _V_BLOCK, _G_BLOCK0 = 0, 4, 5, 6
_CB_BLOCK0, _CC_BLOCK0, _CX_BLOCK0, _GC_BLOCK0 = 10, 14, 18, 22
_N_CONV_BLOCKS = 4


def _weight_stage_scratch(w):
    rows = 1 << ((_W_STAGE_BYTES // (4 * w.shape[1])).bit_length() - 1)
    assert w.shape[0] % rows == 0 and rows % _BF16_SUBLANES == 0
    stage = (_W_STAGE_SLOTS, rows, w.shape[1])
    return [pltpu.VMEM(w.shape, bf16), pltpu.VMEM(stage, f32), pltpu.SemaphoreType.DMA((_W_STAGE_SLOTS,))]


def _load_weights_as_bf16(w_hbm, w_ref, stage_ref, sem):
    slots, chunk = stage_ref.shape[0], stage_ref.shape[1]
    n_chunks = w_ref.shape[0] // chunk

    def chunk_rows(c):
        start = c * chunk
        return pl.ds(start if isinstance(start, int) else pl.multiple_of(start, chunk), chunk)

    def copy(c):
        slot = c % slots
        return pltpu.make_async_copy(w_hbm.at[chunk_rows(c), :], stage_ref.at[slot], sem.at[slot])

    for c in range(min(slots - 1, n_chunks)):
        copy(c).start()

    def convert(c, carry):
        @pl.when(c + slots - 1 < n_chunks)
        def _():
            copy(c + slots - 1).start()

        copy(c).wait()
        w_ref[chunk_rows(c), :] = stage_ref[c % slots].astype(bf16)
        return carry

    lax.fori_loop(0, n_chunks, convert, 0)


def _proj_kernel(x_ref, nin_ref, w_hbm, qg_ref, kn_ref, cos_ref, sin_ref, cost_ref, sint_ref, cw_ref, cb_ref, wo_ref,
                 q_ref, k_ref, v_ref, sg_ref, conv_ref, edge_ref, wob_ref, hn_ref, w_ref, stage_ref, sem):
    tm = hn_ref.shape[0]
    wob_ref[...] = wo_ref[...].astype(bf16)

    @pl.when(pl.program_id(0) == 0)
    def _():
        _load_weights_as_bf16(w_hbm, w_ref, stage_ref, sem)

    x = x_ref[...]
    ms = jnp.mean(x * x, axis=-1, keepdims=True)
    hn_ref[...] = (x * lax.rsqrt(ms + EPS) * nin_ref[...]).astype(bf16)

    def proj(block):
        w = w_ref[:, block * _W_BLOCK:(block + 1) * _W_BLOCK]
        return jnp.dot(hn_ref[...], w, preferred_element_type=f32)

    def head(mat, hh):
        return mat[:, hh * HEAD_DIM:(hh + 1) * HEAD_DIM]

    kk = proj(_K_BLOCK)
    for hh in range(N_KV_HEADS):
        k_ref[0, hh] = _rope(_head_norm(head(kk, hh), kn_ref[...]), cos_ref[...], sin_ref[...]).astype(bf16)

    cos_tt, sin_tt = cost_ref[...], sint_ref[...]
    gain = jnp.tile(jnp.broadcast_to(qg_ref[...], (HEAD_DIM, HEAD_DIM)).T, (1, tm // HEAD_DIM))
    scale = HEAD_DIM ** -0.5 * _LOG2_E
    for pair in range(N_Q_HEADS // 2):
        qq = proj(_Q_BLOCK0 + pair)
        for hh in range(2):
            q_t = head(qq, hh).T
            ms = jnp.mean(q_t * q_t, axis=0, keepdims=True)
            r = lax.rsqrt(ms + EPS) * scale
            q_ref[0, 2 * pair + hh] = (_rope_t(q_t * gain, cos_tt, sin_tt) * r).astype(bf16)

    vv = proj(_V_BLOCK)
    for hh in range(N_KV_HEADS):
        v_ref[0, hh] = head(vv, hh).T.astype(bf16)

    for c in range(ATTN_WIDTH // _W_BLOCK):
        sg_ref[:, c * _W_BLOCK:(c + 1) * _W_BLOCK] = _silu(proj(_G_BLOCK0 + c)).astype(bf16)

    g = _F32_SUBLANES
    for c in range(_N_CONV_BLOCKS):
        cols = slice(c * _W_BLOCK, (c + 1) * _W_BLOCK)
        u = proj(_CC_BLOCK0 + c) * proj(_CX_BLOCK0 + c)
        e = proj(_CB_BLOCK0 + c) * _silu(proj(_GC_BLOCK0 + c))
        y = _conv_taps(pltpu.roll(u, 1, 0), u, pltpu.roll(u, tm - 1, 0), cw_ref[:, cols], cb_ref[:, cols])
        conv_ref[:, cols] = (e * y).astype(bf16)
        edge_ref[0, 0 * g:1 * g, cols] = u[:g]
        edge_ref[0, 1 * g:2 * g, cols] = u[tm - g:]
        edge_ref[0, 2 * g:3 * g, cols] = e[:g]
        edge_ref[0, 3 * g:4 * g, cols] = e[tm - g:]


def _in_projection(x2, norm_in, w, q_norm, k_norm, cos_t, sin_t, conv_w, conv_b, w_out, batch, seq):
    rows, d_model = x2.shape
    tm = _PROJ_ROWS
    nt = seq // tm
    wo_rows = w_out.shape[0] // (rows // tm)
    assert wo_rows % _BF16_SUBLANES == 0 and wo_rows * (rows // tm) == w_out.shape[0]
    wo_slab = pl.BlockSpec((wo_rows, w_out.shape[1]), lambda i: (i, 0))
    full = lambda shape: pl.BlockSpec(shape, lambda i: (0,) * len(shape))
    table = pl.BlockSpec((tm, HEAD_DIM), lambda i: (i % nt, 0))
    table_t = pl.BlockSpec((HEAD_DIM, tm), lambda i: (0, i % nt))
    row_tile = lambda width: pl.BlockSpec((tm, width), lambda i: (i, 0))
    conv_width = _N_CONV_BLOCKS * _W_BLOCK
    out_shape = (
        jax.ShapeDtypeStruct((batch, N_Q_HEADS, HEAD_DIM, seq), bf16),
        jax.ShapeDtypeStruct((batch, N_KV_HEADS, seq, HEAD_DIM), bf16),
        jax.ShapeDtypeStruct((batch, N_KV_HEADS, HEAD_DIM, seq), bf16),
        jax.ShapeDtypeStruct((rows, ATTN_WIDTH), bf16),
        jax.ShapeDtypeStruct((rows, conv_width), bf16),
        jax.ShapeDtypeStruct((rows // tm, _EDGE_ROWS, conv_width), f32),
        jax.ShapeDtypeStruct(w_out.shape, bf16),
    )
    out_specs = (
        pl.BlockSpec((1, N_Q_HEADS, HEAD_DIM, tm), lambda i: (i // nt, 0, 0, i % nt)),
        pl.BlockSpec((1, N_KV_HEADS, tm, HEAD_DIM), lambda i: (i // nt, 0, i % nt, 0)),
        pl.BlockSpec((1, N_KV_HEADS, HEAD_DIM, tm), lambda i: (i // nt, 0, 0, i % nt)),
        row_tile(ATTN_WIDTH), row_tile(conv_width),
        pl.BlockSpec((1, _EDGE_ROWS, conv_width), lambda i: (i, 0, 0)),
        wo_slab,
    )
    return pl.pallas_call(
        _proj_kernel,
        grid=(rows // tm,),
        in_specs=[
            row_tile(d_model),
            full((1, d_model)),
            pl.BlockSpec(memory_space=pl.ANY),
            full((1, HEAD_DIM)), full((1, HEAD_DIM)),
            table, table, table_t, table_t,
            full(conv_w.shape), full(conv_b.shape),
            wo_slab,
        ],
        out_specs=out_specs,
        out_shape=out_shape,
        scratch_shapes=[pltpu.VMEM((tm, d_model), bf16)] + _weight_stage_scratch(w),
        compiler_params=pltpu.CompilerParams(
            dimension_semantics=("arbitrary",),
            vmem_limit_bytes=_VMEM_LIMIT),
        name="in_projection",
    )(x2, norm_in, w, q_norm, k_norm, cos_t, sin_t, np.ascontiguousarray(cos_t.T), np.ascontiguousarray(sin_t.T),
      conv_w, conv_b, w_out)


def _attn_kernel(q_ref, k_ref, v_ref, sg_ref, o_ref, acc_ref, *maybe_s_ref, tq, tk, nk, nq, bounded):
    cols = GQA_GROUP * tq

    def q_tile(i):
        off = pl.multiple_of(i * tq, tq)
        return jnp.concatenate([q_ref[0, h, :, pl.ds(off, tq)] for h in range(GQA_GROUP)], axis=1)

    def key_chunk(j):
        return k_ref[0, 0, pl.ds(pl.multiple_of(j * tk, tk), tk), :]

    ones_rows = jnp.ones((_BF16_SUBLANES, tk), bf16)

    def value_chunk(j):
        off = pl.multiple_of(j * tk, tk)
        return jnp.concatenate([v_ref[0, 0, :, pl.ds(off, tk)], ones_rows], axis=0)

    def finish_tile(i, acc):
        o_t = acc[:HEAD_DIM, :] / acc[HEAD_DIM:HEAD_DIM + 1, :]
        rows = pl.ds(pl.multiple_of(i * tq, tq), tq)
        for h in range(GQA_GROUP):
            lanes = slice(h * HEAD_DIM, (h + 1) * HEAD_DIM)
            gate = sg_ref[rows, lanes].astype(f32)
            o_ref[rows, lanes] = (o_t[:, h * tq:(h + 1) * tq].T * gate).astype(bf16)

    if bounded:
        def tile_group(g, carry):
            for t in range(_ATTN_TILES_PER_ITER):
                i = g * _ATTN_TILES_PER_ITER + t
                q_t = q_tile(i)
                acc = acc_ref.at[t]
                for c in range(nk):
                    p_t = jnp.exp2(jnp.dot(key_chunk(c), q_t, preferred_element_type=f32)).astype(bf16)
                    pv_t = jnp.dot(value_chunk(c), p_t, preferred_element_type=f32)
                    acc[...] = pv_t if c == 0 else acc[...] + pv_t
                finish_tile(i, acc)
            return carry

        lax.fori_loop(0, nq // _ATTN_TILES_PER_ITER, tile_group, 0)
        return

    s_ref, = maybe_s_ref

    def scores(q_mat, j, slot):
        s_t = jnp.dot(key_chunk(j), q_mat, preferred_element_type=f32)
        s_ref[slot] = s_t
        return jnp.max(s_t, axis=0, keepdims=True)

    def step(j, slot, acc, m_prev, chunk_max, q_ahead, j_ahead):
        ahead_max = scores(q_ahead, j_ahead, 1 - slot)
        m_new = jnp.maximum(m_prev, chunk_max)
        alpha = jnp.exp2(m_prev - m_new)
        p_t = jnp.exp2(s_ref[slot] - m_new).astype(bf16)
        pv_t = jnp.dot(value_chunk(j), p_t, preferred_element_type=f32)
        acc[...] = alpha * acc[...] + pv_t
        return m_new, ahead_max

    def query_tile(i, acc, chunk_max):
        q_t = q_tile(i)
        q_next = q_tile(jnp.minimum(i + 1, nq - 1))
        acc[...] = jnp.zeros(acc.shape, f32)
        m_run = jnp.full((1, cols), -jnp.inf, f32)
        for c in range(nk):
            last = c == nk - 1
            m_run, chunk_max = step(c, c % 2, acc, m_run, chunk_max, q_next if last else q_t, 0 if last else c + 1)
        finish_tile(i, acc)
        return chunk_max

    def tile_group(g, chunk_max):
        for t in range(_ATTN_FALLBACK_TILES_PER_ITER):
            chunk_max = query_tile(g * _ATTN_FALLBACK_TILES_PER_ITER + t, acc_ref.at[t], chunk_max)
        return chunk_max

    lax.fori_loop(0, nq // _ATTN_FALLBACK_TILES_PER_ITER, tile_group, scores(q_tile(0), 0, 0))


def _attention_call(q_t, k, v_t, sg, batch, seq, bounded):
    tq, tk = _ATTN_TQ, _ATTN_TK
    cols = GQA_GROUP * tq
    group_width = GQA_GROUP * HEAD_DIM
    gate_spec = pl.BlockSpec((seq, group_width), lambda b, h: (b, h))
    scratch = [pltpu.VMEM((_ATTN_TILES_PER_ITER, HEAD_DIM + _BF16_SUBLANES, cols), f32)]
    if not bounded:
        scratch.append(pltpu.VMEM((2, tk, cols), f32))
    return pl.pallas_call(
        functools.partial(_attn_kernel, tq=tq, tk=tk, nk=seq // tk, nq=seq // tq, bounded=bounded),
        grid=(batch, N_KV_HEADS),
        in_specs=[
            pl.BlockSpec((1, GQA_GROUP, HEAD_DIM, seq), lambda b, h: (b, h, 0, 0)),
            pl.BlockSpec((1, 1, seq, HEAD_DIM), lambda b, h: (b, h, 0, 0)),
            pl.BlockSpec((1, 1, HEAD_DIM, seq), lambda b, h: (b, h, 0, 0)),
            gate_spec,
        ],
        out_specs=gate_spec,
        out_shape=jax.ShapeDtypeStruct((batch * seq, ATTN_WIDTH), bf16),
        scratch_shapes=scratch,
        compiler_params=pltpu.CompilerParams(
            dimension_semantics=("arbitrary", "arbitrary"),
            vmem_limit_bytes=_VMEM_LIMIT),
        name="gqa_attention_bounded" if bounded else "gqa_attention",
    )(q_t, k, v_t, sg)


def _attention(q_t, k, v_t, sg, q_norm, k_norm, batch, seq):
    g_max = jnp.max(jnp.abs(jnp.concatenate([q_norm, k_norm], axis=0)))
    bound = (_LOG2_E * HEAD_DIM ** 0.5 * (1.0 + 2.0 ** -7)) * g_max * g_max
    return lax.cond(bound <= _SCORE_BOUND,
                    functools.partial(_attention_call, batch=batch, seq=seq, bounded=True),
                    functools.partial(_attention_call, batch=batch, seq=seq, bounded=False),
                    q_t, k, v_t, sg)


def _out_kernel(attn_ref, conv_ref, eprev_ref, edge_ref, enext_ref, x_ref, w_ref, cw_ref, cb_ref, nf_ref,
                o_ref, *, tm, seq):
    i = pl.program_id(0)

    g = _F32_SUBLANES
    t0 = (i * tm) % seq
    cw, cb = cw_ref[...], cb_ref[...]
    row = lambda ref, r: ref[0, r:r + 1, :]
    u_before = jnp.where(t0 == 0, 0.0, row(eprev_ref, 2 * g - 1))
    u_after = jnp.where(t0 + tm == seq, 0.0, row(enext_ref, 0))
    first = row(edge_ref, 2 * g) * _conv_taps(u_before, row(edge_ref, 0), row(edge_ref, 1), cw, cb)
    last = row(edge_ref, 4 * g - 1) * _conv_taps(row(edge_ref, 2 * g - 2), row(edge_ref, 2 * g - 1), u_after, cw, cb)
    p = _BF16_SUBLANES
    sub = lax.broadcasted_iota(jnp.int32, (p, conv_ref.shape[1]), 0)
    top = jnp.where(sub == 0, first, conv_ref[:p, :].astype(f32)).astype(bf16)
    bottom = jnp.where(sub == p - 1, last, conv_ref[tm - p:, :].astype(f32)).astype(bf16)
    conv = jnp.concatenate([top, conv_ref[p:tm - p, :], bottom], axis=0)

    acc = jnp.dot(attn_ref[...], w_ref[:ATTN_WIDTH, :], preferred_element_type=f32)
    acc = acc + jnp.dot(conv, w_ref[ATTN_WIDTH:, :], preferred_element_type=f32)
    h = x_ref[...] + acc
    ms = jnp.mean(h * h, axis=-1, keepdims=True)
    o_ref[...] = h * lax.rsqrt(ms + EPS) * nf_ref[...]


def _out_projection(attn, conv, edge, x2, w, conv_w, conv_b, norm_final, seq):
    rows, d_model = x2.shape
    tm = _OUT_ROWS
    n_tiles = rows // tm
    assert edge.shape[0] == n_tiles, "edge rows are saved per projection tile"
    conv_width = conv.shape[1]
    row_tile = lambda width: pl.BlockSpec((tm, width), lambda i: (i, 0))
    full = lambda shape: pl.BlockSpec(shape, lambda i: (0,) * len(shape))
    edge_of = lambda shift: pl.BlockSpec((1, _EDGE_ROWS, conv_width),
                                         lambda i: (jnp.clip(i + shift, 0, n_tiles - 1), 0, 0))
    return pl.pallas_call(
        functools.partial(_out_kernel, tm=tm, seq=seq),
        grid=(n_tiles,),
        in_specs=[
            row_tile(ATTN_WIDTH),
            row_tile(conv_width),
            edge_of(-1), edge_of(0), edge_of(1),
            row_tile(d_model),
            pl.BlockSpec(w.shape, lambda i: (0, 0), pipeline_mode=pl.Buffered(1)),
            full(conv_w.shape), full(conv_b.shape), full(norm_final.shape),
        ],
        out_specs=row_tile(d_model),
        out_shape=jax.ShapeDtypeStruct((rows, d_model), f32),
        compiler_params=pltpu.CompilerParams(
            dimension_semantics=("arbitrary",),
            vmem_limit_bytes=_VMEM_LIMIT),
        name="out_projection",
    )(attn, conv, edge, edge, edge, x2, w, conv_w, conv_b, norm_final)


def kernel(x, norm_in, w_in, q_norm, k_norm, conv_w, conv_b, w_out, norm_final):
    batch, seq, d_model = x.shape
    assert norm_in.shape[0] == 1, "single-layer block"
    assert _PROJ_ROWS == _OUT_ROWS, "conv edge rows are exchanged per row tile"
    assert seq % _PROJ_ROWS == 0 and seq % _ATTN_TK == 0 and seq % (_ATTN_TQ * _ATTN_TILES_PER_ITER) == 0
    x2 = x.reshape(batch * seq, d_model)
    cos_t, sin_t = _rope_tables(seq)
    q, k, v, sg, conv, edge, w_out_bf16 = _in_projection(
        x2, norm_in, w_in[0], q_norm, k_norm, cos_t, sin_t, conv_w[0], conv_b, w_out[0], batch, seq)
    attn = _attention(q, k, v, sg, q_norm, k_norm, batch, seq)
    out = _out_projection(attn, conv, edge, x2, w_out_bf16, conv_w[0], conv_b,
                          norm_final.reshape(1, d_model), seq)
    return out.reshape(batch, seq, d_model)
```
